```python
import math
import jax, jax.numpy as jnp
from jax import lax
import numpy as np

D_MODEL = 2048
BATCH = 2
SEQ = 4096
DEPTH = 1
DEC_BATCH = 16
DEC_SEQ = 2048
PAST_LEN = 128

ATTN_HEADS = 16
ATTN_HEAD_DIM = 128
ATTN_WIDTH = ATTN_HEADS * ATTN_HEAD_DIM
DILATED_PATTERNS = ((128, 1), (512, 4), (2048, 16))
N_BUCKETS = 32
BUCKET_MAX_DIST = 1024
SSM_HEADS = 32
SSM_HEAD_DIM = 64
SSM_WIDTH = SSM_HEADS * SSM_HEAD_DIM
SSM_GROUPS = 4
SSM_HEADS_PER_GROUP = SSM_HEADS // SSM_GROUPS
SSM_STATE = 128
CONV_WIDTH = 5
CHUNK = 128
CONV_CH = SSM_WIDTH + 2 * SSM_GROUPS * SSM_STATE
MIX_WIDTH = ATTN_WIDTH + SSM_WIDTH
IN_COLS = 4 * ATTN_WIDTH + SSM_WIDTH + CONV_CH + 2 * SSM_HEADS
EPS = 1e-6

kernel_name = "hybrid_dilated_attn_ssd_encoder"


def rmsnorm(x, g):
    xf = x.astype(jnp.float32)
    y = xf * lax.rsqrt(jnp.mean(xf * xf, axis=-1, keepdims=True) + EPS)
    return (y * g.astype(jnp.float32)).astype(x.dtype)


def t5_bucket(rel):
    nb = N_BUCKETS // 2
    max_exact = nb // 2
    ret = jnp.where(rel > 0, nb, 0)
    n = jnp.abs(rel)
    nf = jnp.maximum(n, 1).astype(jnp.float32)
    large = max_exact + (jnp.log(nf / max_exact) / math.log(BUCKET_MAX_DIST / max_exact)
                         * (nb - max_exact)).astype(jnp.int32)
    large = jnp.minimum(large, nb - 1)
    return ret + jnp.where(n < max_exact, n, large)


def banded_attention(q, k, v, bias, radius):
    bq, n, h, dh = q.shape
    nb = -(-n // radius)
    extra = nb * radius - n
    qb = jnp.pad(q, ((0, 0), (0, extra), (0, 0), (0, 0))).reshape(bq, nb, radius, h, dh)

    def neighbours(t):
        tb = jnp.pad(t, ((0, 0), (radius, extra + radius), (0, 0), (0, 0))).reshape(bq, nb + 2, radius, h, dh)
        return jnp.concatenate([tb[:, :-2], tb[:, 1:-1], tb[:, 2:]], axis=2)

    kb, vb = neighbours(k), neighbours(v)
    s = jnp.einsum("bnqhd,bnkhd->bnhqk", qb, kb, preferred_element_type=jnp.float32) * (1.0 / math.sqrt(dh))
    s = s + bias[None, None]
    qi = jnp.arange(radius)[:, None]
    kj = jnp.arange(3 * radius)[None, :]
    rel = kj - radius - qi
    key_pos = jnp.arange(nb)[:, None, None] * radius + kj[None] - radius
    valid = (jnp.abs(rel) <= radius)[None] & (key_pos >= 0) & (key_pos < n)
    s = jnp.where(valid[None, :, None], s, -jnp.inf)
    m = jnp.max(s, axis=-1, keepdims=True)
    p = jnp.exp(s - m)
    l = jnp.sum(p, axis=-1, keepdims=True)
    o = jnp.einsum("bnhqk,bnkhd->bnqhd", p, vb.astype(jnp.float32)) / jnp.swapaxes(l, 2, 3)
    lse = jnp.swapaxes(m + jnp.log(l), 2, 3)[..., 0]
    o = o.reshape(bq, nb * radius, h, dh)[:, :n]
    lse = lse.reshape(bq, nb * radius, h)[:, :n]
    return o, lse


def dilated_attention(q, k, v, rel_table):
    b, L, h, dh = q.shape
    outs, lses = [], []
    for window, dil in DILATED_PATTERNS:
        r = window // (2 * dil)
        n = L // dil

        def split(t):
            return t.reshape(b, n, dil, h, dh).transpose(0, 2, 1, 3, 4).reshape(b * dil, n, h, dh)

        rel = jnp.arange(3 * r)[None, :] - r - jnp.arange(r)[:, None]
        bias = rel_table[t5_bucket(rel * dil)].transpose(2, 0, 1).astype(jnp.float32)
        o, lse = banded_attention(split(q), split(k), split(v), bias, r)
        outs.append(o.reshape(b, dil, n, h, dh).transpose(0, 2, 1, 3, 4).reshape(b, L, h, dh))
        lses.append(lse.reshape(b, dil, n, h).transpose(0, 2, 1, 3).reshape(b, L, h))
    w = jax.nn.softmax(jnp.stack(lses, axis=0), axis=0)
    return jnp.einsum("pblh,pblhd->blhd", w, jnp.stack(outs, axis=0))


def ssd_scan(x, dt, a, bmat, cmat):
    b, L, g, hg, p = x.shape
    n = bmat.shape[-1]
    nc = L // CHUNK
    x = x.astype(jnp.float32)
    dt = dt.astype(jnp.float32)
    xc = (x * dt[..., None]).reshape(b, nc, CHUNK, g, hg, p)
    adt = (dt * a.astype(jnp.float32)).reshape(b, nc, CHUNK, g, hg)
    bc = bmat.astype(jnp.float32).reshape(b, nc, CHUNK, g, n)
    cc = cmat.astype(jnp.float32).reshape(b, nc, CHUNK, g, n)
    a_cs = jnp.cumsum(adt, axis=2)
    seg = a_cs[:, :, :, None] - a_cs[:, :, None, :]
    causal = jnp.tril(jnp.ones((CHUNK, CHUNK), dtype=bool))[None, None, :, :, None, None]
    decay = jnp.exp(jnp.where(causal, seg, -jnp.inf))
    cb = jnp.einsum("bclgn,bcsgn->bclsg", cc, bc)
    y_diag = jnp.einsum("bclsg,bclsgh,bcsghp->bclghp", cb, decay, xc)
    decay_states = jnp.exp(a_cs[:, :, -1:] - a_cs)
    states = jnp.einsum("bclgn,bclgh,bclghp->bcghpn", bc, decay_states, xc)
    chunk_decay = jnp.exp(a_cs[:, :, -1])

    def step(hstate, inp):
        s_c, d_c = inp
        return hstate * d_c[..., None, None] + s_c, hstate

    h0 = jnp.zeros((b, g, hg, p, n), jnp.float32)
    _, prev = lax.scan(step, h0, (jnp.moveaxis(states, 1, 0), jnp.moveaxis(chunk_decay, 1, 0)))
    prev = jnp.moveaxis(prev, 0, 1)
    y_off = jnp.einsum("bclgn,bcghpn,bclgh->bclghp", cc, prev, jnp.exp(a_cs))
    return (y_diag + y_off).reshape(b, L, g, hg, p)


def ssd_branch(z, xbc, dt_raw, conv_w, conv_b, dt_bias_f, dt_bias_b, a_log_f, a_log_b, d_skip, norm_g):
    b, L, _ = xbc.shape
    half = CONV_WIDTH // 2
    conv = lax.conv_general_dilated(xbc, conv_w[:, None, :], window_strides=(1,), padding=[(half, half)],
                                    dimension_numbers=("NWC", "WIO", "NWC"), feature_group_count=CONV_CH)
    xbc = jax.nn.silu(conv + conv_b)
    xs, bm, cm = jnp.split(xbc, [SSM_WIDTH, SSM_WIDTH + SSM_GROUPS * SSM_STATE], axis=-1)
    xs = xs.reshape(b, L, SSM_GROUPS, SSM_HEADS_PER_GROUP, SSM_HEAD_DIM)
    bm = bm.reshape(b, L, SSM_GROUPS, SSM_STATE)
    cm = cm.reshape(b, L, SSM_GROUPS, SSM_STATE)
    gh = (b, L, SSM_GROUPS, SSM_HEADS_PER_GROUP)
    dt_f = jax.nn.softplus((dt_raw[..., :SSM_HEADS] + dt_bias_f).astype(jnp.float32)).reshape(gh)
    dt_b = jax.nn.softplus((dt_raw[..., SSM_HEADS:] + dt_bias_b).astype(jnp.float32)).reshape(gh)
    a_f = -jnp.exp(a_log_f.astype(jnp.float32)).reshape(SSM_GROUPS, SSM_HEADS_PER_GROUP)
    a_b = -jnp.exp(a_log_b.astype(jnp.float32)).reshape(SSM_GROUPS, SSM_HEADS_PER_GROUP)
    y_f = ssd_scan(xs, dt_f, a_f, bm, cm)
    y_b = ssd_scan(xs[:, ::-1], dt_b[:, ::-1], a_b, bm[:, ::-1], cm[:, ::-1])[:, ::-1]
    y = y_f + y_b + d_skip.reshape(SSM_GROUPS, SSM_HEADS_PER_GROUP)[..., None] * xs
    y = y.reshape(b, L, SSM_WIDTH) * jax.nn.silu(z)
    y = rmsnorm(y.reshape(b, L, SSM_GROUPS, SSM_WIDTH // SSM_GROUPS), norm_g.reshape(SSM_GROUPS, -1))
    return y.reshape(b, L, SSM_WIDTH)


def hybrid_layer(x, pre_norm_g, w_in, rel_bias_table, attn_norm_g, conv_w, conv_b, dt_bias_fwd, dt_bias_bwd,
                 a_log_fwd, a_log_bwd, d_skip, ssm_norm_g, w_out, post_norm_g):
    b, L, _ = x.shape
    h = rmsnorm(x, pre_norm_g)
    proj = h @ w_in
    cuts = [ATTN_WIDTH, 2 * ATTN_WIDTH, 3 * ATTN_WIDTH, 4 * ATTN_WIDTH, 4 * ATTN_WIDTH + SSM_WIDTH,
            4 * ATTN_WIDTH + SSM_WIDTH + CONV_CH]
    q, k, v, z_attn, z_ssm, xbc, dt_raw = jnp.split(proj, cuts, axis=-1)
    hs = (b, L, ATTN_HEADS, ATTN_HEAD_DIM)
    o = dilated_attention(q.reshape(hs), k.reshape(hs), v.reshape(hs), rel_bias_table)
    attn_out = rmsnorm(o.reshape(b, L, ATTN_WIDTH), attn_norm_g) * jax.nn.silu(z_attn)
    ssm_out = ssd_branch(z_ssm, xbc, dt_raw, conv_w, conv_b, dt_bias_fwd, dt_bias_bwd, a_log_fwd, a_log_bwd,
                         d_skip, ssm_norm_g)
    mix = jnp.concatenate([attn_out.astype(x.dtype), ssm_out.astype(x.dtype)], axis=-1) @ w_out
    return (x + rmsnorm(mix, post_norm_g)).astype(x.dtype)


def setup_inputs(seed: int = 0) -> dict:
    key = jax.random.key(seed)
    ks = jax.random.split(key, 16)
    f32 = jnp.float32
    dt0 = jnp.exp(jax.random.uniform(ks[8], (DEPTH, SSM_HEADS), f32) * (math.log(0.1) - math.log(0.001))
                  + math.log(0.001))
    dt1 = jnp.exp(jax.random.uniform(ks[9], (DEPTH, SSM_HEADS), f32) * (math.log(0.1) - math.log(0.001))
                  + math.log(0.001))
    return {
        "x_prompt": jax.random.normal(ks[0], (BATCH, SEQ, D_MODEL), f32),
        "x_sample": jax.random.normal(ks[1], (DEC_BATCH, DEC_SEQ, D_MODEL), f32),
        "pre_norm_g": 1.0 + 0.1 * jax.random.normal(ks[2], (DEPTH, D_MODEL), f32),
        "w_in": jax.random.normal(ks[3], (DEPTH, D_MODEL, IN_COLS), f32) * D_MODEL ** -0.5,
        "rel_bias_table": 0.5 * jax.random.normal(ks[4], (N_BUCKETS, ATTN_HEADS), f32),
        "attn_norm_g": 1.0 + 0.1 * jax.random.normal(ks[5], (DEPTH, ATTN_WIDTH), f32),
        "conv_w": jax.random.normal(ks[6], (DEPTH, CONV_WIDTH, CONV_CH), f32) * CONV_WIDTH ** -0.5,
        "conv_b": 0.02 * jax.random.normal(ks[7], (DEPTH, CONV_CH), f32),
        "dt_bias_fwd": dt0 + jnp.log(-jnp.expm1(-dt0)),
        "dt_bias_bwd": dt1 + jnp.log(-jnp.expm1(-dt1)),
        "a_log_fwd": jnp.log(jax.random.uniform(ks[10], (DEPTH, SSM_HEADS), f32, 1.0, 16.0)),
        "a_log_bwd": jnp.log(jax.random.uniform(ks[11], (DEPTH, SSM_HEADS), f32, 1.0, 16.0)),
        "d_skip": 1.0 + 0.1 * jax.random.normal(ks[12], (DEPTH, SSM_HEADS), f32),
        "ssm_norm_g": 1.0 + 0.1 * jax.random.normal(ks[13], (DEPTH, SSM_WIDTH), f32),
        "w_out": jax.random.normal(ks[14], (DEPTH, MIX_WIDTH, D_MODEL), f32) * MIX_WIDTH ** -0.5,
        "post_norm_g": 1.0 + 0.1 * jax.random.normal(ks[15], (DEPTH, D_MODEL), f32),
    }


def reference(x_prompt, x_sample, pre_norm_g, w_in, rel_bias_table, attn_norm_g, conv_w, conv_b, dt_bias_fwd,
              dt_bias_bwd, a_log_fwd, a_log_bwd, d_skip, ssm_norm_g, w_out, post_norm_g):
    y_prompt = x_prompt
    y_sample = x_sample
    for i in range(DEPTH):
        y_prompt = hybrid_layer(y_prompt, pre_norm_g[i], w_in[i], rel_bias_table, attn_norm_g[i], conv_w[i],
                                conv_b[i], dt_bias_fwd[i], dt_bias_bwd[i], a_log_fwd[i], a_log_bwd[i], d_skip[i],
                                ssm_norm_g[i], w_out[i], post_norm_g[i])
        y_sample = hybrid_layer(y_sample, pre_norm_g[i], w_in[i], rel_bias_table, attn_norm_g[i], conv_w[i],
                                conv_b[i], dt_bias_fwd[i], dt_bias_bwd[i], a_log_fwd[i], a_log_bwd[i], d_skip[i],
                                ssm_norm_g[i], w_out[i], post_norm_g[i])
    return (y_prompt, y_sample)
```

```python
import functools
import math

import numpy as np
import jax
import jax.numpy as jnp
from jax import lax
from jax.experimental import pallas as pl
from jax.experimental.pallas import tpu as pltpu

D_MODEL = 2048
ATTN_HEADS = 16
HEAD_DIM = 128
ATTN_WIDTH = ATTN_HEADS * HEAD_DIM
DILATIONS = (1, 4, 16)
RADIUS = 64
N_BUCKETS = 32
BUCKET_MAX_DIST = 1024
SSM_HEADS = 32
SSM_HEAD_DIM = 64
SSM_WIDTH = SSM_HEADS * SSM_HEAD_DIM
SSM_GROUPS = 4
HEADS_PER_GROUP = SSM_HEADS // SSM_GROUPS
GROUP_WIDTH = SSM_WIDTH // SSM_GROUPS
SSM_STATE = 128
CONV_WIDTH = 5
CHUNK = 128
CONV_CH = SSM_WIDTH + 2 * SSM_GROUPS * SSM_STATE
MIX_WIDTH = ATTN_WIDTH + SSM_WIDTH
PROJ_MAIN = 4 * ATTN_WIDTH + SSM_WIDTH + CONV_CH
EPS = 1e-6

LANES = 128
BF16_ROWS = 16
QBLK = 128
KBLK = QBLK + 2 * RADIUS
NEG = -1e30
VMEM_LIMIT = 56 * 1024 * 1024

COL_Q, COL_K, COL_V, COL_ZA = 0, ATTN_WIDTH, 2 * ATTN_WIDTH, 3 * ATTN_WIDTH
COL_ZS = 4 * ATTN_WIDTH
COL_X = COL_ZS + SSM_WIDTH
COL_B = COL_X + SSM_WIDTH
COL_C = COL_B + SSM_GROUPS * SSM_STATE

f32 = jnp.float32
bf16 = jnp.bfloat16


def _params(*sem):
    return pltpu.CompilerParams(dimension_semantics=sem, vmem_limit_bytes=VMEM_LIMIT)


def _bucket_tiles():
    qi = np.arange(QBLK)[:, None]
    kj = np.arange(KBLK)[None, :]
    rel = kj - RADIUS - qi
    nb = N_BUCKETS // 2
    max_exact = nb // 2
    tiles = []
    for dil in DILATIONS:
        r = rel * dil
        n = np.abs(r)
        nf = np.maximum(n, 1).astype(np.float32)
        large = max_exact + (np.log(nf / np.float32(max_exact)) / np.float32(math.log(BUCKET_MAX_DIST / max_exact))
                             * np.float32(nb - max_exact)).astype(np.int32)
        large = np.minimum(large, nb - 1)
        bucket = np.where(r > 0, nb, 0) + np.where(n < max_exact, n, large)
        tiles.append(np.where(np.abs(rel) <= RADIUS, bucket, -1))
    return np.stack(tiles).astype(np.int32)


def _relbias_kernel(tab_ref, idx_ref, out_ref):
    h = pl.program_id(0)
    for p in range(len(DILATIONS)):
        idx = idx_ref[p]
        acc = jnp.full((QBLK, KBLK), NEG, f32)
        for b in range(N_BUCKETS):
            acc = jnp.where(idx == b, tab_ref[b, h], acc)
        out_ref[p, 0] = acc


def _relbias(rel_table):
    idx = jnp.asarray(_bucket_tiles())
    np_ = len(DILATIONS)
    return pl.pallas_call(
        _relbias_kernel,
        grid=(ATTN_HEADS,),
        in_specs=[pl.BlockSpec(memory_space=pltpu.SMEM),
                  pl.BlockSpec((np_, QBLK, KBLK), lambda h: (0, 0, 0))],
        out_specs=pl.BlockSpec((np_, 1, QBLK, KBLK), lambda h: (0, h, 0, 0)),
        out_shape=jax.ShapeDtypeStruct((np_, ATTN_HEADS, QBLK, KBLK), f32),
        compiler_params=_params("arbitrary"),
        name="relbias",
    )(rel_table, idx)


INPROJ_TM = 1024
INPROJ_TN = 1024
NORM_ROWS = 128


def _inproj_kernel(x_ref, g_ref, w_ref, wdt_ref, proj_ref, dt_ref, h_ref):
    @pl.when(pl.program_id(1) == 0)
    def _():
        g = g_ref[...]

        def rows(i, carry):
            r = pl.ds(pl.multiple_of(i * NORM_ROWS, NORM_ROWS), NORM_ROWS)
            x = x_ref[r, :]
            y = x * lax.rsqrt(jnp.mean(x * x, axis=-1, keepdims=True) + EPS)
            h_ref[r, :] = (y * g).astype(bf16)
            return carry

        lax.fori_loop(0, INPROJ_TM // NORM_ROWS, rows, 0)
        dt_ref[...] = jnp.dot(h_ref[...], wdt_ref[...], preferred_element_type=f32)

    proj_ref[...] = jnp.dot(h_ref[...], w_ref[...], preferred_element_type=f32).astype(bf16)


def _inproj(x2, g, w_main, w_dt):
    m = x2.shape[0]
    return pl.pallas_call(
        _inproj_kernel,
        grid=(m // INPROJ_TM, PROJ_MAIN // INPROJ_TN),
        in_specs=[pl.BlockSpec((INPROJ_TM, D_MODEL), lambda i, j: (i, 0)),
                  pl.BlockSpec((1, D_MODEL), lambda i, j: (0, 0)),
                  pl.BlockSpec((D_MODEL, INPROJ_TN), lambda i, j: (0, j)),
                  pl.BlockSpec((D_MODEL, LANES), lambda i, j: (0, 0))],
        out_specs=[pl.BlockSpec((INPROJ_TM, INPROJ_TN), lambda i, j: (i, j)),
                   pl.BlockSpec((INPROJ_TM, LANES), lambda i, j: (i, 0))],
        out_shape=[jax.ShapeDtypeStruct((m, PROJ_MAIN), bf16),
                   jax.ShapeDtypeStruct((m, LANES), f32)],
        scratch_shapes=[pltpu.VMEM((INPROJ_TM, D_MODEL), bf16)],
        compiler_params=_params("arbitrary", "arbitrary"),
        name="inproj",
    )(x2, g, w_main, w_dt)


def _attn_kernel(q_ref, k_ref, v_ref, bias_ref, o_ref,
                 qs_ref, ks_ref, vs_ref, qd_ref, kd_ref, vd_ref, bias4_ref, od_ref, lsed_ref, *, seq):
    nblk = seq // QBLK
    scale = 1.0 / math.sqrt(HEAD_DIM)

    col = lax.broadcasted_iota(jnp.int32, (QBLK, KBLK), 1)
    before = col < RADIUS
    after = col >= QBLK + RADIUS
    for p in range(len(DILATIONS)):
        b = bias_ref[p, 0]
        bias4_ref[p, 0] = b
        bias4_ref[p, 1] = jnp.where(before, NEG, b)
        bias4_ref[p, 2] = jnp.where(after, NEG, b)
        bias4_ref[p, 3] = jnp.where(before | after, NEG, b)

    def stage(i, carry):
        r = pl.ds(pl.multiple_of(i * QBLK, QBLK), QBLK)
        qs_ref[r, :] = q_ref[0, r, :].astype(f32)
        ks_ref[r, :] = k_ref[0, r, :].astype(f32)
        vs_ref[r, :] = v_ref[0, r, :].astype(f32)
        return carry

    lax.fori_loop(0, nblk, stage, 0)

    pad = jnp.zeros((RADIUS, HEAD_DIM), bf16)
    for ref in (kd_ref, vd_ref):
        ref[0:RADIUS, :] = pad
        ref[RADIUS + seq:2 * RADIUS + seq, :] = pad

    def gather_classes(dil):
        n = seq // dil
        for c in range(dil):
            rows = pl.ds(c, n, stride=dil)
            qd_ref[c * n:(c + 1) * n, :] = qs_ref[rows, :].astype(bf16)
            kd_ref[RADIUS + c * n:RADIUS + (c + 1) * n, :] = ks_ref[rows, :].astype(bf16)
            vd_ref[RADIUS + c * n:RADIUS + (c + 1) * n, :] = vs_ref[rows, :].astype(bf16)

    def tile(t, p, tiles_per_class):
        r0 = pl.multiple_of(t * QBLK, QBLK)
        q = qd_ref[pl.ds(r0, QBLK), :]
        k = kd_ref[pl.ds(r0, KBLK), :]
        v = vd_ref[pl.ds(r0, KBLK), :]
        jj = lax.rem(t, tiles_per_class)
        variant = (jj == 0).astype(jnp.int32) + 2 * (jj == tiles_per_class - 1).astype(jnp.int32)
        s = lax.dot_general(q, k, (((1,), (1,)), ((), ())), preferred_element_type=f32)
        s = s * scale + bias4_ref[p, variant]
        m = jnp.max(s, axis=-1, keepdims=True)
        e = jnp.exp(s - m)
        l = jnp.sum(e, axis=-1, keepdims=True)
        o = jnp.dot(e.astype(bf16), v, preferred_element_type=f32)
        return o / l, m + jnp.log(l)

    for slot, p in ((0, 2), (1, 1)):
        dil = DILATIONS[p]
        tiles_per_class = seq // dil // QBLK
        gather_classes(dil)

        def dilated(t, carry, p=p, dil=dil, tiles_per_class=tiles_per_class, slot=slot):
            o, lse = tile(t, p, tiles_per_class)
            c = t // tiles_per_class
            jj = lax.rem(t, tiles_per_class)
            rows = pl.ds(jj * (QBLK * dil) + c, QBLK, stride=dil)
            od_ref[slot, rows, :] = o
            lsed_ref[slot, rows, :] = jnp.broadcast_to(lse, (QBLK, HEAD_DIM))
            return carry

        lax.fori_loop(0, nblk, dilated, 0)

    def copy_rows(i, carry):
        r0 = pl.multiple_of(i * QBLK, QBLK)
        r = pl.ds(r0, QBLK)
        qd_ref[r, :] = q_ref[0, r, :]
        kd_ref[pl.ds(r0 + RADIUS, QBLK), :] = k_ref[0, r, :]
        vd_ref[pl.ds(r0 + RADIUS, QBLK), :] = v_ref[0, r, :]
        return carry

    lax.fori_loop(0, nblk, copy_rows, 0)

    def dense(t, carry):
        o1, lse1 = tile(t, 0, nblk)
        r = pl.ds(pl.multiple_of(t * QBLK, QBLK), QBLK)
        lse16, lse4 = lsed_ref[0, r, :], lsed_ref[1, r, :]
        top = jnp.maximum(jnp.maximum(lse16, lse4), lse1)
        w1, w16, w4 = jnp.exp(lse1 - top), jnp.exp(lse16 - top), jnp.exp(lse4 - top)
        mix = w1 * o1 + w16 * od_ref[0, r, :] + w4 * od_ref[1, r, :]
        o_ref[0, r, :] = (mix / (w1 + w16 + w4)).astype(o_ref.dtype)
        return carry

    lax.fori_loop(0, nblk, dense, 0)


def _attention(proj3, bias):
    b, seq, _ = proj3.shape
    np_ = len(DILATIONS)
    qkv = lambda off: pl.BlockSpec((1, seq, HEAD_DIM), lambda i, h: (i, 0, off // HEAD_DIM + h))
    return pl.pallas_call(
        functools.partial(_attn_kernel, seq=seq),
        grid=(b, ATTN_HEADS),
        in_specs=[qkv(COL_Q), qkv(COL_K), qkv(COL_V),
                  pl.BlockSpec((np_, 1, QBLK, KBLK), lambda i, h: (0, h, 0, 0))],
        out_specs=pl.BlockSpec((1, seq, HEAD_DIM), lambda i, h: (i, 0, h)),
        out_shape=jax.ShapeDtypeStruct((b, seq, ATTN_WIDTH), bf16),
        scratch_shapes=[pltpu.VMEM((seq, HEAD_DIM), f32)] * 3
                       + [pltpu.VMEM((seq, HEAD_DIM), bf16)]
                       + [pltpu.VMEM((seq + 2 * RADIUS, HEAD_DIM), bf16)] * 2
                       + [pltpu.VMEM((np_, 4, QBLK, KBLK), f32),
                          pltpu.VMEM((2, seq, HEAD_DIM), f32),
                          pltpu.VMEM((2, seq, HEAD_DIM), f32)],
        compiler_params=_params("arbitrary", "arbitrary"),
        name="attn",
    )(proj3, proj3, proj3, bias)


HALO = BF16_ROWS


def _silu(x):
    return x / (1.0 + jnp.exp(-x))


def _softplus(x):
    return jnp.maximum(x, 0.0) + jnp.log1p(jnp.exp(-jnp.abs(x)))


def _ssd_kernel(z_ref, x_ref, bm_ref, cm_ref, dt_ref, cwx_ref, cbx_ref, cwb_ref, cbb_ref, cwc_ref, cbc_ref,
                hp_ref, dskip_ref, ng_ref, out_ref,
                xs_ref, bs_ref, cs_ref, y_ref, hf_ref, hb_ref, tr_ref, *, seq):
    nc = seq // CHUNK
    g = pl.program_id(1)
    half = CONV_WIDTH // 2

    def conv_chunk(i, carry):
        r0 = pl.multiple_of(i * CHUNK, CHUNK)
        lo = pl.multiple_of(jnp.maximum(r0 - HALO, 0), HALO)
        hi = pl.multiple_of(jnp.minimum(r0 + CHUNK, seq - HALO), HALO)
        has_lo = (i > 0).astype(f32)
        has_hi = (i < nc - 1).astype(f32)
        for src, w_ref, b_ref, dst in ((x_ref, cwx_ref, cbx_ref, xs_ref),
                                       (bm_ref, cwb_ref, cbb_ref, bs_ref),
                                       (cm_ref, cwc_ref, cbc_ref, cs_ref)):
            ext = jnp.concatenate([
                src[0, pl.ds(lo, HALO), :].astype(f32) * has_lo,
                src[0, pl.ds(r0, CHUNK), :].astype(f32),
                src[0, pl.ds(hi, HALO), :].astype(f32) * has_hi], axis=0)
            acc = b_ref[...]
            for tap in range(CONV_WIDTH):
                start = HALO - half + tap
                acc = acc + w_ref[tap:tap + 1, :] * ext[start:start + CHUNK, :]
            dst[pl.ds(r0, CHUNK), :] = _silu(acc).astype(bf16)
        return carry

    lax.fori_loop(0, nc, conv_chunk, 0)

    lane = lax.broadcasted_iota(jnp.int32, (CHUNK, LANES), 1)
    li = lax.broadcasted_iota(jnp.int32, (CHUNK, CHUNK), 0)
    si = lax.broadcasted_iota(jnp.int32, (CHUNK, CHUNK), 1)
    lower = li >= si
    upper = si >= li
    tril = lower.astype(f32)
    dt_bias = hp_ref[0, 0:1, :]
    a_row = -jnp.exp(hp_ref[0, 1:2, :])
    shift = lax.rem(LANES - 2 * HEADS_PER_GROUP * g, LANES)

    def chunk_terms(c):
        r = pl.ds(pl.multiple_of(c * CHUNK, CHUNK), CHUNK)
        raw = pltpu.roll(dt_ref[0, r, :], shift, 1)
        dtv = _softplus(raw + dt_bias)
        adt = dtv * a_row
        cum = jnp.dot(tril, adt, preferred_element_type=f32, precision=lax.Precision.HIGHEST)
        total = cum[CHUNK - 1:CHUNK, :]
        suffix = total - cum + adt
        acs = jnp.where(lane < HEADS_PER_GROUP, cum, suffix)
        tr_ref[0] = acs.T
        tr_ref[1] = dtv.T
        return r, acs, total

    hf_ref[...] = jnp.zeros_like(hf_ref)
    hb_ref[...] = jnp.zeros_like(hb_ref)

    def forward(c, carry):
        r, acs, total = chunk_terms(c)
        cmat = cs_ref[r, :]
        bmat = bs_ref[r, :]
        cf = cmat.astype(f32)
        bt = bmat.astype(f32).T
        cb = lax.dot_general(cmat, bmat, (((1,), (1,)), ((), ())), preferred_element_type=f32)
        for j in range(HEADS_PER_GROUP):
            jb = HEADS_PER_GROUP + j
            a_f = jnp.broadcast_to(acs[:, j:j + 1], (CHUNK, CHUNK))
            a_b = jnp.broadcast_to(acs[:, jb:jb + 1], (CHUNK, CHUNK))
            r_f, r_b = tr_ref[0, j:j + 1, :], tr_ref[0, jb:jb + 1, :]
            dt_f, dt_b = tr_ref[1, j:j + 1, :], tr_ref[1, jb:jb + 1, :]
            w = (jnp.exp(jnp.where(lower, a_f - r_f, NEG)) * dt_f
                 + jnp.exp(jnp.where(upper, a_b - r_b, NEG)) * dt_b)
            xh = xs_ref[r, j * SSM_HEAD_DIM:(j + 1) * SSM_HEAD_DIM]
            y = jnp.dot((cb * w).astype(bf16), xh, preferred_element_type=f32)
            y = y + jnp.dot((cf * jnp.exp(a_f)).astype(bf16), hf_ref[j].astype(bf16),
                            preferred_element_type=f32)
            y_ref[r, j * SSM_HEAD_DIM:(j + 1) * SSM_HEAD_DIM] = y
            tot = total[:, j:j + 1]
            inject = (bt * (jnp.exp(tot - r_f) * dt_f)).astype(bf16)
            hf_ref[j] = hf_ref[j] * jnp.exp(tot) + jnp.dot(inject, xh, preferred_element_type=f32)
        y_ref[r, :] = y_ref[r, :] + dskip_ref[...] * xs_ref[r, :].astype(f32)
        return carry

    lax.fori_loop(0, nc, forward, 0)

    def backward(i, carry):
        c = nc - 1 - i
        r, acs, total = chunk_terms(c)
        cf = cs_ref[r, :].astype(f32)
        bt = bs_ref[r, :].astype(f32).T
        for j in range(HEADS_PER_GROUP):
            jb = HEADS_PER_GROUP + j
            a_b = jnp.broadcast_to(acs[:, jb:jb + 1], (CHUNK, CHUNK))
            r_b, dt_b = tr_ref[0, jb:jb + 1, :], tr_ref[1, jb:jb + 1, :]
            cols = slice(j * SSM_HEAD_DIM, (j + 1) * SSM_HEAD_DIM)
            y_ref[r, cols] = y_ref[r, cols] + jnp.dot((cf * jnp.exp(a_b)).astype(bf16), hb_ref[j].astype(bf16),
                                                      preferred_element_type=f32)
            tot = total[:, jb:jb + 1]
            inject = (bt * (jnp.exp(tot - r_b) * dt_b)).astype(bf16)
            hb_ref[j] = hb_ref[j] * jnp.exp(tot) + jnp.dot(inject, xs_ref[r, cols], preferred_element_type=f32)
        y = y_ref[r, :] * _silu(z_ref[0, r, :].astype(f32))
        y = y * lax.rsqrt(jnp.mean(y * y, axis=-1, keepdims=True) + EPS)
        out_ref[0, r, :] = (y * ng_ref[...]).astype(out_ref.dtype)
        return carry

    lax.fori_loop(0, nc, backward, 0)


def _ssd(proj3, dt3, conv_w, conv_b, head_params, dskip_cols, norm_g):
    b, seq, _ = proj3.shape
    gw, ns = GROUP_WIDTH, SSM_STATE
    wide = lambda off: pl.BlockSpec((1, seq, gw), lambda i, g: (i, 0, off // gw + g))
    narrow = lambda off: pl.BlockSpec((1, seq, ns), lambda i, g: (i, 0, off // ns + g))
    xoff, boff, coff = 0, SSM_WIDTH, SSM_WIDTH + SSM_GROUPS * ns
    cw = lambda off, w: pl.BlockSpec((CONV_WIDTH, w), lambda i, g: (0, off // w + g))
    cbias = lambda off, w: pl.BlockSpec((1, w), lambda i, g: (0, off // w + g))
    return pl.pallas_call(
        functools.partial(_ssd_kernel, seq=seq),
        grid=(b, SSM_GROUPS),
        in_specs=[wide(COL_ZS), wide(COL_X), narrow(COL_B), narrow(COL_C),
                  pl.BlockSpec((1, seq, LANES), lambda i, g: (i, 0, 0)),
                  cw(xoff, gw), cbias(xoff, gw), cw(boff, ns), cbias(boff, ns), cw(coff, ns), cbias(coff, ns),
                  pl.BlockSpec((1, 8, LANES), lambda i, g: (g, 0, 0)),
                  pl.BlockSpec((1, gw), lambda i, g: (0, g)),
                  pl.BlockSpec((1, gw), lambda i, g: (0, g))],
        out_specs=pl.BlockSpec((1, seq, gw), lambda i, g: (i, 0, g)),
        out_shape=jax.ShapeDtypeStruct((b, seq, SSM_WIDTH), bf16),
        scratch_shapes=[pltpu.VMEM((seq, gw), bf16),
                        pltpu.VMEM((seq, ns), bf16),
                        pltpu.VMEM((seq, ns), bf16),
                        pltpu.VMEM((seq, gw), f32),
                        pltpu.VMEM((HEADS_PER_GROUP, ns, SSM_HEAD_DIM), f32),
                        pltpu.VMEM((HEADS_PER_GROUP, ns, SSM_HEAD_DIM), f32),
                        pltpu.VMEM((2, LANES, CHUNK), f32)],
        compiler_params=_params("arbitrary", "arbitrary"),
        name="ssd",
    )(proj3, proj3, proj3, proj3, dt3, conv_w, conv_b, conv_w, conv_b, conv_w, conv_b,
      head_params, dskip_cols, norm_g)


OUTPROJ_TM = 256


def _outproj_kernel(o_ref, z_ref, s_ref, x_ref, w_ref, ga_ref, gp_ref, y_ref):
    o = o_ref[...].astype(f32)
    a = o * lax.rsqrt(jnp.mean(o * o, axis=-1, keepdims=True) + EPS) * ga_ref[...]
    a = a * _silu(z_ref[...].astype(f32))
    mix = jnp.dot(a.astype(bf16), w_ref[0:ATTN_WIDTH, :], preferred_element_type=f32)
    mix = mix + jnp.dot(s_ref[...], w_ref[ATTN_WIDTH:MIX_WIDTH, :], preferred_element_type=f32)
    mix = mix * lax.rsqrt(jnp.mean(mix * mix, axis=-1, keepdims=True) + EPS) * gp_ref[...]
    y_ref[...] = x_ref[...] + mix


def _outproj(o2, proj2, s2, x2, w_out, g_attn, g_post):
    m = x2.shape[0]
    tm = OUTPROJ_TM
    row = lambda width, blk: pl.BlockSpec((tm, width), lambda i: (i, blk))
    const = lambda shape: pl.BlockSpec(shape, lambda i: (0, 0))
    return pl.pallas_call(
        _outproj_kernel,
        grid=(m // tm,),
        in_specs=[row(ATTN_WIDTH, 0), row(ATTN_WIDTH, COL_ZA // ATTN_WIDTH), row(SSM_WIDTH, 0), row(D_MODEL, 0),
                  pl.BlockSpec((MIX_WIDTH, D_MODEL), lambda i: (0, 0), pipeline_mode=pl.Buffered(1)),
                  const((1, ATTN_WIDTH)), const((1, D_MODEL))],
        out_specs=row(D_MODEL, 0),
        out_shape=jax.ShapeDtypeStruct((m, D_MODEL), f32),
        compiler_params=_params("arbitrary"),
        name="outproj",
    )(o2, proj2, s2, x2, w_out, g_attn, g_post)


def _group_head_rows(fwd, bwd):
    rows = jnp.concatenate([fwd.reshape(SSM_GROUPS, HEADS_PER_GROUP), bwd.reshape(SSM_GROUPS, HEADS_PER_GROUP)],
                           axis=1)
    return jnp.pad(rows, ((0, 0), (0, LANES - 2 * HEADS_PER_GROUP)))


def _layer(x, bias, pre_g, w_main, w_dt, attn_g, conv_w, conv_b, head_params, dskip_cols, ssm_g, w_out, post_g):
    b, seq, _ = x.shape
    x2 = x.reshape(b * seq, D_MODEL)
    proj2, dt2 = _inproj(x2, pre_g, w_main, w_dt)
    proj3 = proj2.reshape(b, seq, PROJ_MAIN)
    o = _attention(proj3, bias)
    s = _ssd(proj3, dt2.reshape(b, seq, LANES), conv_w, conv_b, head_params, dskip_cols, ssm_g)
    y = _outproj(o.reshape(b * seq, ATTN_WIDTH), proj2, s.reshape(b * seq, SSM_WIDTH), x2, w_out, attn_g, post_g)
    return y.reshape(b, seq, D_MODEL)


def kernel(x_prompt, x_sample, pre_norm_g, w_in, rel_bias_table, attn_norm_g, conv_w, conv_b, dt_bias_fwd,
           dt_bias_bwd, a_log_fwd, a_log_bwd, d_skip, ssm_norm_g, w_out, post_norm_g):
    depth = w_in.shape[0]
    bias = _relbias(rel_bias_table)
    y_prompt, y_sample = x_prompt, x_sample
    for i in range(depth):
        w_main = w_in[i, :, :PROJ_MAIN].astype(bf16)
        w_dt = w_in[i, :, PROJ_MAIN:]
        w_dt = jnp.concatenate([w_dt[:, :SSM_HEADS].reshape(D_MODEL, SSM_GROUPS, HEADS_PER_GROUP),
                                w_dt[:, SSM_HEADS:].reshape(D_MODEL, SSM_GROUPS, HEADS_PER_GROUP)], axis=2)
        w_dt = jnp.pad(w_dt.reshape(D_MODEL, 2 * SSM_HEADS), ((0, 0), (0, LANES - 2 * SSM_HEADS))).astype(bf16)
        head_params = jnp.stack([_group_head_rows(dt_bias_fwd[i], dt_bias_bwd[i]),
                                 _group_head_rows(a_log_fwd[i], a_log_bwd[i])], axis=1)
        head_params = jnp.pad(head_params, ((0, 0), (0, 6), (0, 0)))
        args = (bias, pre_norm_g[i].reshape(1, D_MODEL), w_main, w_dt, attn_norm_g[i].reshape(1, ATTN_WIDTH),
                conv_w[i], conv_b[i].reshape(1, CONV_CH), head_params,
                jnp.repeat(d_skip[i], SSM_HEAD_DIM).reshape(1, SSM_WIDTH), ssm_norm_g[i].reshape(1, SSM_WIDTH),
                w_out[i].astype(bf16), post_norm_g[i].reshape(1, D_MODEL))
        y_prompt = _layer(y_prompt, *args)
        y_sample = _layer(y_sample, *args)
    return (y_prompt, y_sample)
```

```python
import functools
import math

import numpy as np
import jax
import jax.numpy as jnp
from jax import lax
from jax.experimental import pallas as pl
from jax.experimental.pallas import tpu as pltpu

D_MODEL = 2048
ATTN_HEADS = 16
HEAD_DIM = 128
ATTN_WIDTH = ATTN_HEADS * HEAD_DIM
DILATIONS = (1, 4, 16)
RADIUS = 64
N_BUCKETS = 32
BUCKET_MAX_DIST = 1024
SSM_HEADS = 32
SSM_HEAD_DIM = 64
SSM_WIDTH = SSM_HEADS * SSM_HEAD_DIM
SSM_GROUPS = 4
HEADS_PER_GROUP = SSM_HEADS // SSM_GROUPS
GROUP_WIDTH = SSM_WIDTH // SSM_GROUPS
SSM_STATE = 128
CONV_WIDTH = 5
CHUNK = 128
CONV_CH = SSM_WIDTH + 2 * SSM_GROUPS * SSM_STATE
MIX_WIDTH = ATTN_WIDTH + SSM_WIDTH
PROJ_MAIN = 4 * ATTN_WIDTH + SSM_WIDTH + CONV_CH
EPS = 1e-6

LANES = 128
BF16_ROWS = 16
QBLK = 128
KBLK = QBLK + 2 * RADIUS
TILE_UNROLL = 4
NEG = -1e30
VMEM_LIMIT = 56 * 1024 * 1024

COL_Q, COL_K, COL_V, COL_ZA = 0, ATTN_WIDTH, 2 * ATTN_WIDTH, 3 * ATTN_WIDTH
COL_ZS = 4 * ATTN_WIDTH
COL_X = COL_ZS + SSM_WIDTH
COL_B = COL_X + SSM_WIDTH
COL_C = COL_B + SSM_GROUPS * SSM_STATE

f32 = jnp.float32
bf16 = jnp.bfloat16


def _params(*sem):
    return pltpu.CompilerParams(dimension_semantics=sem, vmem_limit_bytes=VMEM_LIMIT)


def _bucket_tiles():
    qi = np.arange(QBLK)[:, None]
    kj = np.arange(KBLK)[None, :]
    rel = kj - RADIUS - qi
    nb = N_BUCKETS // 2
    max_exact = nb // 2
    tiles = []
    for dil in DILATIONS:
        r = rel * dil
        n = np.abs(r)
        nf = np.maximum(n, 1).astype(np.float32)
        large = max_exact + (np.log(nf / np.float32(max_exact)) / np.float32(math.log(BUCKET_MAX_DIST / max_exact))
                             * np.float32(nb - max_exact)).astype(np.int32)
        large = np.minimum(large, nb - 1)
        bucket = np.where(r > 0, nb, 0) + np.where(n < max_exact, n, large)
        tiles.append(np.where(np.abs(rel) <= RADIUS, bucket, -1))
    return np.stack(tiles).astype(np.int32)


def _relbias_kernel(tab_ref, idx_ref, out_ref):
    h = pl.program_id(0)
    for p in range(len(DILATIONS)):
        idx = idx_ref[p]
        acc = jnp.full((QBLK, KBLK), NEG, f32)
        for b in range(N_BUCKETS):
            acc = jnp.where(idx == b, tab_ref[b, h], acc)
        out_ref[p, 0] = acc


def _relbias(rel_table):
    idx = jnp.asarray(_bucket_tiles())
    np_ = len(DILATIONS)
    return pl.pallas_call(
        _relbias_kernel,
        grid=(ATTN_HEADS,),
        in_specs=[pl.BlockSpec(memory_space=pltpu.SMEM),
                  pl.BlockSpec((np_, QBLK, KBLK), lambda h: (0, 0, 0))],
        out_specs=pl.BlockSpec((np_, 1, QBLK, KBLK), lambda h: (0, h, 0, 0)),
        out_shape=jax.ShapeDtypeStruct((np_, ATTN_HEADS, QBLK, KBLK), f32),
        compiler_params=_params("arbitrary"),
        name="relbias",
    )(rel_table, idx)


INPROJ_TM = 1024
INPROJ_TN = 1024
NORM_ROWS = 128


def _inproj_kernel(x_ref, g_ref, w_ref, wdt_ref, proj_ref, dt_ref, h_ref):
    @pl.when(pl.program_id(1) == 0)
    def _():
        g = g_ref[...]

        def rows(i, carry):
            r = pl.ds(pl.multiple_of(i * NORM_ROWS, NORM_ROWS), NORM_ROWS)
            x = x_ref[r, :]
            y = x * lax.rsqrt(jnp.mean(x * x, axis=-1, keepdims=True) + EPS)
            h_ref[r, :] = (y * g).astype(bf16)
            return carry

        lax.fori_loop(0, INPROJ_TM // NORM_ROWS, rows, 0)
        dt_ref[...] = jnp.dot(h_ref[...], wdt_ref[...], preferred_element_type=f32)

    proj_ref[...] = jnp.dot(h_ref[...], w_ref[...], preferred_element_type=f32).astype(bf16)


def _inproj(x2, g, w_main, w_dt):
    m = x2.shape[0]
    return pl.pallas_call(
        _inproj_kernel,
        grid=(m // INPROJ_TM, PROJ_MAIN // INPROJ_TN),
        in_specs=[pl.BlockSpec((INPROJ_TM, D_MODEL), lambda i, j: (i, 0)),
                  pl.BlockSpec((1, D_MODEL), lambda i, j: (0, 0)),
                  pl.BlockSpec((D_MODEL, INPROJ_TN), lambda i, j: (0, j)),
                  pl.BlockSpec((D_MODEL, LANES), lambda i, j: (0, 0))],
        out_specs=[pl.BlockSpec((INPROJ_TM, INPROJ_TN), lambda i, j: (i, j)),
                   pl.BlockSpec((INPROJ_TM, LANES), lambda i, j: (i, 0))],
        out_shape=[jax.ShapeDtypeStruct((m, PROJ_MAIN), bf16),
                   jax.ShapeDtypeStruct((m, LANES), f32)],
        scratch_shapes=[pltpu.VMEM((INPROJ_TM, D_MODEL), bf16)],
        compiler_params=_params("arbitrary", "arbitrary"),
        name="inproj",
    )(x2, g, w_main, w_dt)


def _attn_kernel(q_ref, k_ref, v_ref, bias_ref, o_ref,
                 qs_ref, ks_ref, vs_ref, qd_ref, kd_ref, vd_ref, bias4_ref, od_ref, lsed_ref, *, seq):
    nblk = seq // QBLK
    scale = 1.0 / math.sqrt(HEAD_DIM)

    col = lax.broadcasted_iota(jnp.int32, (QBLK, KBLK), 1)
    before = col < RADIUS
    after = col >= QBLK + RADIUS
    for p in range(len(DILATIONS)):
        b = bias_ref[p, 0]
        bias4_ref[p, 0] = b
        bias4_ref[p, 1] = jnp.where(before, NEG, b)
        bias4_ref[p, 2] = jnp.where(after, NEG, b)
        bias4_ref[p, 3] = jnp.where(before | after, NEG, b)

    def stage(i, carry):
        r = pl.ds(pl.multiple_of(i * QBLK, QBLK), QBLK)
        qs_ref[r, :] = q_ref[0, r, :].astype(f32)
        ks_ref[r, :] = k_ref[0, r, :].astype(f32)
        vs_ref[r, :] = v_ref[0, r, :].astype(f32)
        return carry

    lax.fori_loop(0, nblk, stage, 0)

    pad = jnp.zeros((RADIUS, HEAD_DIM), bf16)
    for ref in (kd_ref, vd_ref):
        ref[0:RADIUS, :] = pad
        ref[RADIUS + seq:2 * RADIUS + seq, :] = pad

    def gather_classes(dil):
        n = seq // dil
        for c in range(dil):
            rows = pl.ds(c, n, stride=dil)
            qd_ref[c * n:(c + 1) * n, :] = qs_ref[rows, :].astype(bf16)
            kd_ref[RADIUS + c * n:RADIUS + (c + 1) * n, :] = ks_ref[rows, :].astype(bf16)
            vd_ref[RADIUS + c * n:RADIUS + (c + 1) * n, :] = vs_ref[rows, :].astype(bf16)

    def tile(t, p, tiles_per_class):
        r0 = pl.multiple_of(t * QBLK, QBLK)
        q = qd_ref[pl.ds(r0, QBLK), :]
        k = kd_ref[pl.ds(r0, KBLK), :]
        v = vd_ref[pl.ds(r0, KBLK), :]
        jj = lax.rem(t, tiles_per_class)
        variant = (jj == 0).astype(jnp.int32) + 2 * (jj == tiles_per_class - 1).astype(jnp.int32)
        s = lax.dot_general(q, k, (((1,), (1,)), ((), ())), preferred_element_type=f32)
        s = s * scale + bias4_ref[p, variant]
        m = jnp.max(s, axis=-1, keepdims=True)
        e = jnp.exp(s - m)
        l = jnp.sum(e, axis=-1, keepdims=True)
        o = jnp.dot(e.astype(bf16), v, preferred_element_type=f32)
        return o / l, m + jnp.log(l)

    for slot, p in ((0, 2), (1, 1)):
        dil = DILATIONS[p]
        tiles_per_class = seq // dil // QBLK
        gather_classes(dil)

        def dilated(t, carry, p=p, dil=dil, tiles_per_class=tiles_per_class, slot=slot):
            o, lse = tile(t, p, tiles_per_class)
            c = t // tiles_per_class
            jj = lax.rem(t, tiles_per_class)
            rows = pl.ds(jj * (QBLK * dil) + c, QBLK, stride=dil)
            od_ref[slot, rows, :] = o
            lsed_ref[slot, rows, :] = jnp.broadcast_to(lse, (QBLK, HEAD_DIM))
            return carry

        lax.fori_loop(0, nblk, dilated, 0, unroll=TILE_UNROLL)

    def copy_rows(i, carry):
        r0 = pl.multiple_of(i * QBLK, QBLK)
        r = pl.ds(r0, QBLK)
        qd_ref[r, :] = q_ref[0, r, :]
        kd_ref[pl.ds(r0 + RADIUS, QBLK), :] = k_ref[0, r, :]
        vd_ref[pl.ds(r0 + RADIUS, QBLK), :] = v_ref[0, r, :]
        return carry

    lax.fori_loop(0, nblk, copy_rows, 0)

    def dense(t, carry):
        o1, lse1 = tile(t, 0, nblk)
        r = pl.ds(pl.multiple_of(t * QBLK, QBLK), QBLK)
        lse16, lse4 = lsed_ref[0, r, :], lsed_ref[1, r, :]
        top = jnp.maximum(jnp.maximum(lse16, lse4), lse1)
        w1, w16, w4 = jnp.exp(lse1 - top), jnp.exp(lse16 - top), jnp.exp(lse4 - top)
        mix = w1 * o1 + w16 * od_ref[0, r, :] + w4 * od_ref[1, r, :]
        o_ref[0, r, :] = (mix / (w1 + w16 + w4)).astype(o_ref.dtype)
        return carry

    lax.fori_loop(0, nblk, dense, 0, unroll=TILE_UNROLL)


def _attention(proj3, bias):
    b, seq, _ = proj3.shape
    np_ = len(DILATIONS)
    qkv = lambda off: pl.BlockSpec((1, seq, HEAD_DIM), lambda i, h: (i, 0, off // HEAD_DIM + h))
    return pl.pallas_call(
        functools.partial(_attn_kernel, seq=seq),
        grid=(b, ATTN_HEADS),
        in_specs=[qkv(COL_Q), qkv(COL_K), qkv(COL_V),
                  pl.BlockSpec((np_, 1, QBLK, KBLK), lambda i, h: (0, h, 0, 0))],
        out_specs=pl.BlockSpec((1, seq, HEAD_DIM), lambda i, h: (i, 0, h)),
        out_shape=jax.ShapeDtypeStruct((b, seq, ATTN_WIDTH), bf16),
        scratch_shapes=[pltpu.VMEM((seq, HEAD_DIM), f32)] * 3
                       + [pltpu.VMEM((seq, HEAD_DIM), bf16)]
                       + [pltpu.VMEM((seq + 2 * RADIUS, HEAD_DIM), bf16)] * 2
                       + [pltpu.VMEM((np_, 4, QBLK, KBLK), f32),
                          pltpu.VMEM((2, seq, HEAD_DIM), f32),
                          pltpu.VMEM((2, seq, HEAD_DIM), f32)],
        compiler_params=_params("arbitrary", "arbitrary"),
        name="attn",
    )(proj3, proj3, proj3, bias)


HALO = BF16_ROWS


def _silu(x):
    return x / (1.0 + jnp.exp(-x))


def _softplus(x):
    return jnp.maximum(x, 0.0) + jnp.log1p(jnp.exp(-jnp.abs(x)))


def _ssd_kernel(z_ref, x_ref, bm_ref, cm_ref, dt_ref, cwx_ref, cbx_ref, cwb_ref, cbb_ref, cwc_ref, cbc_ref,
                hp_ref, dskip_ref, ng_ref, out_ref,
                xs_ref, bs_ref, cs_ref, y_ref, hf_ref, hb_ref, tr_ref, *, seq):
    nc = seq // CHUNK
    g = pl.program_id(1)
    half = CONV_WIDTH // 2

    def conv_chunk(i, carry):
        r0 = pl.multiple_of(i * CHUNK, CHUNK)
        lo = pl.multiple_of(jnp.maximum(r0 - HALO, 0), HALO)
        hi = pl.multiple_of(jnp.minimum(r0 + CHUNK, seq - HALO), HALO)
        has_lo = (i > 0).astype(f32)
        has_hi = (i < nc - 1).astype(f32)
        for src, w_ref, b_ref, dst in ((x_ref, cwx_ref, cbx_ref, xs_ref),
                                       (bm_ref, cwb_ref, cbb_ref, bs_ref),
                                       (cm_ref, cwc_ref, cbc_ref, cs_ref)):
            ext = jnp.concatenate([
                src[0, pl.ds(lo, HALO), :].astype(f32) * has_lo,
                src[0, pl.ds(r0, CHUNK), :].astype(f32),
                src[0, pl.ds(hi, HALO), :].astype(f32) * has_hi], axis=0)
            acc = b_ref[...]
            for tap in range(CONV_WIDTH):
                start = HALO - half + tap
                acc = acc + w_ref[tap:tap + 1, :] * ext[start:start + CHUNK, :]
            dst[pl.ds(r0, CHUNK), :] = _silu(acc).astype(bf16)
        return carry

    lax.fori_loop(0, nc, conv_chunk, 0)

    lane = lax.broadcasted_iota(jnp.int32, (CHUNK, LANES), 1)
    li = lax.broadcasted_iota(jnp.int32, (CHUNK, CHUNK), 0)
    si = lax.broadcasted_iota(jnp.int32, (CHUNK, CHUNK), 1)
    lower = li >= si
    upper = si >= li
    tril = lower.astype(f32)
    dt_bias = hp_ref[0, 0:1, :]
    a_row = -jnp.exp(hp_ref[0, 1:2, :])
    shift = lax.rem(LANES - 2 * HEADS_PER_GROUP * g, LANES)

    def chunk_terms(c):
        r = pl.ds(pl.multiple_of(c * CHUNK, CHUNK), CHUNK)
        raw = pltpu.roll(dt_ref[0, r, :], shift, 1)
        dtv = _softplus(raw + dt_bias)
        adt = dtv * a_row
        cum = jnp.dot(tril, adt, preferred_element_type=f32, precision=lax.Precision.HIGHEST)
        total = cum[CHUNK - 1:CHUNK, :]
        suffix = total - cum + adt
        acs = jnp.where(lane < HEADS_PER_GROUP, cum, suffix)
        tr_ref[0] = acs.T
        tr_ref[1] = dtv.T
        return r, acs, total

    hf_ref[...] = jnp.zeros_like(hf_ref)
    hb_ref[...] = jnp.zeros_like(hb_ref)

    def forward(c, carry):
        r, acs, total = chunk_terms(c)
        cmat = cs_ref[r, :]
        bmat = bs_ref[r, :]
        cf = cmat.astype(f32)
        bt = bmat.astype(f32).T
        cb = lax.dot_general(cmat, bmat, (((1,), (1,)), ((), ())), preferred_element_type=f32)
        for j in range(HEADS_PER_GROUP):
            jb = HEADS_PER_GROUP + j
            a_f = jnp.broadcast_to(acs[:, j:j + 1], (CHUNK, CHUNK))
            a_b = jnp.broadcast_to(acs[:, jb:jb + 1], (CHUNK, CHUNK))
            r_f, r_b = tr_ref[0, j:j + 1, :], tr_ref[0, jb:jb + 1, :]
            dt_f, dt_b = tr_ref[1, j:j + 1, :], tr_ref[1, jb:jb + 1, :]
            w = (jnp.exp(jnp.where(lower, a_f - r_f, NEG)) * dt_f
                 + jnp.exp(jnp.where(upper, a_b - r_b, NEG)) * dt_b)
            xh = xs_ref[r, j * SSM_HEAD_DIM:(j + 1) * SSM_HEAD_DIM]
            y = jnp.dot((cb * w).astype(bf16), xh, preferred_element_type=f32)
            y = y + jnp.dot((cf * jnp.exp(a_f)).astype(bf16), hf_ref[j].astype(bf16),
                            preferred_element_type=f32)
            y_ref[r, j * SSM_HEAD_DIM:(j + 1) * SSM_HEAD_DIM] = y
            tot = total[:, j:j + 1]
            inject = (bt * (jnp.exp(tot - r_f) * dt_f)).astype(bf16)
            hf_ref[j] = hf_ref[j] * jnp.exp(tot) + jnp.dot(inject, xh, preferred_element_type=f32)
        y_ref[r, :] = y_ref[r, :] + dskip_ref[...] * xs_ref[r, :].astype(f32)
        return carry

    lax.fori_loop(0, nc, forward, 0)

    def backward(i, carry):
        c = nc - 1 - i
        r, acs, total = chunk_terms(c)
        cf = cs_ref[r, :].astype(f32)
        bt = bs_ref[r, :].astype(f32).T
        for j in range(HEADS_PER_GROUP):
            jb = HEADS_PER_GROUP + j
            a_b = jnp.broadcast_to(acs[:, jb:jb + 1], (CHUNK, CHUNK))
            r_b, dt_b = tr_ref[0, jb:jb + 1, :], tr_ref[1, jb:jb + 1, :]
            cols = slice(j * SSM_HEAD_DIM, (j + 1) * SSM_HEAD_DIM)
            y_ref[r, cols] = y_ref[r, cols] + jnp.dot((cf * jnp.exp(a_b)).astype(bf16), hb_ref[j].astype(bf16),
                                                      preferred_element_type=f32)
            tot = total[:, jb:jb + 1]
            inject = (bt * (jnp.exp(tot - r_b) * dt_b)).astype(bf16)
            hb_ref[j] = hb_ref[j] * jnp.exp(tot) + jnp.dot(inject, xs_ref[r, cols], preferred_element_type=f32)
        y = y_ref[r, :] * _silu(z_ref[0, r, :].astype(f32))
        y = y * lax.rsqrt(jnp.mean(y * y, axis=-1, keepdims=True) + EPS)
        out_ref[0, r, :] = (y * ng_ref[...]).astype(out_ref.dtype)
        return carry

    lax.fori_loop(0, nc, backward, 0)


def _ssd(proj3, dt3, conv_w, conv_b, head_params, dskip_cols, norm_g):
    b, seq, _ = proj3.shape
    gw, ns = GROUP_WIDTH, SSM_STATE
    wide = lambda off: pl.BlockSpec((1, seq, gw), lambda i, g: (i, 0, off // gw + g))
    narrow = lambda off: pl.BlockSpec((1, seq, ns), lambda i, g: (i, 0, off // ns + g))
    xoff, boff, coff = 0, SSM_WIDTH, SSM_WIDTH + SSM_GROUPS * ns
    cw = lambda off, w: pl.BlockSpec((CONV_WIDTH, w), lambda i, g: (0, off // w + g))
    cbias = lambda off, w: pl.BlockSpec((1, w), lambda i, g: (0, off // w + g))
    return pl.pallas_call(
        functools.partial(_ssd_kernel, seq=seq),
        grid=(b, SSM_GROUPS),
        in_specs=[wide(COL_ZS), wide(COL_X), narrow(COL_B), narrow(COL_C),
                  pl.BlockSpec((1, seq, LANES), lambda i, g: (i, 0, 0)),
                  cw(xoff, gw), cbias(xoff, gw), cw(boff, ns), cbias(boff, ns), cw(coff, ns), cbias(coff, ns),
                  pl.BlockSpec((1, 8, LANES), lambda i, g: (g, 0, 0)),
                  pl.BlockSpec((1, gw), lambda i, g: (0, g)),
                  pl.BlockSpec((1, gw), lambda i, g: (0, g))],
        out_specs=pl.BlockSpec((1, seq, gw), lambda i, g: (i, 0, g)),
        out_shape=jax.ShapeDtypeStruct((b, seq, SSM_WIDTH), bf16),
        scratch_shapes=[pltpu.VMEM((seq, gw), bf16),
                        pltpu.VMEM((seq, ns), bf16),
                        pltpu.VMEM((seq, ns), bf16),
                        pltpu.VMEM((seq, gw), f32),
                        pltpu.VMEM((HEADS_PER_GROUP, ns, SSM_HEAD_DIM), f32),
                        pltpu.VMEM((HEADS_PER_GROUP, ns, SSM_HEAD_DIM), f32),
                        pltpu.VMEM((2, LANES, CHUNK), f32)],
        compiler_params=_params("arbitrary", "arbitrary"),
        name="ssd",
    )(proj3, proj3, proj3, proj3, dt3, conv_w, conv_b, conv_w, conv_b, conv_w, conv_b,
      head_params, dskip_cols, norm_g)


OUTPROJ_TM = 256


def _outproj_kernel(o_ref, z_ref, s_ref, x_ref, w_ref, ga_ref, gp_ref, y_ref):
    o = o_ref[...].astype(f32)
    a = o * lax.rsqrt(jnp.mean(o * o, axis=-1, keepdims=True) + EPS) * ga_ref[...]
    a = a * _silu(z_ref[...].astype(f32))
    mix = jnp.dot(a.astype(bf16), w_ref[0:ATTN_WIDTH, :], preferred_element_type=f32)
    mix = mix + jnp.dot(s_ref[...], w_ref[ATTN_WIDTH:MIX_WIDTH, :], preferred_element_type=f32)
    mix = mix * lax.rsqrt(jnp.mean(mix * mix, axis=-1, keepdims=True) + EPS) * gp_ref[...]
    y_ref[...] = x_ref[...] + mix


def _outproj(o2, proj2, s2, x2, w_out, g_attn, g_post):
    m = x2.shape[0]
    tm = OUTPROJ_TM
    row = lambda width, blk: pl.BlockSpec((tm, width), lambda i: (i, blk))
    const = lambda shape: pl.BlockSpec(shape, lambda i: (0, 0))
    return pl.pallas_call(
        _outproj_kernel,
        grid=(m // tm,),
        in_specs=[row(ATTN_WIDTH, 0), row(ATTN_WIDTH, COL_ZA // ATTN_WIDTH), row(SSM_WIDTH, 0), row(D_MODEL, 0),
                  pl.BlockSpec((MIX_WIDTH, D_MODEL), lambda i: (0, 0), pipeline_mode=pl.Buffered(1)),
                  const((1, ATTN_WIDTH)), const((1, D_MODEL))],
        out_specs=row(D_MODEL, 0),
        out_shape=jax.ShapeDtypeStruct((m, D_MODEL), f32),
        compiler_params=_params("arbitrary"),
        name="outproj",
    )(o2, proj2, s2, x2, w_out, g_attn, g_post)


def _group_head_rows(fwd, bwd):
    rows = jnp.concatenate([fwd.reshape(SSM_GROUPS, HEADS_PER_GROUP), bwd.reshape(SSM_GROUPS, HEADS_PER_GROUP)],
                           axis=1)
    return jnp.pad(rows, ((0, 0), (0, LANES - 2 * HEADS_PER_GROUP)))


def _layer(x, bias, pre_g, w_main, w_dt, attn_g, conv_w, conv_b, head_params, dskip_cols, ssm_g, w_out, post_g):
    b, seq, _ = x.shape
    x2 = x.reshape(b * seq, D_MODEL)
    proj2, dt2 = _inproj(x2, pre_g, w_main, w_dt)
    proj3 = proj2.reshape(b, seq, PROJ_MAIN)
    o = _attention(proj3, bias)
    s = _ssd(proj3, dt2.reshape(b, seq, LANES), conv_w, conv_b, head_params, dskip_cols, ssm_g)
    y = _outproj(o.reshape(b * seq, ATTN_WIDTH), proj2, s.reshape(b * seq, SSM_WIDTH), x2, w_out, attn_g, post_g)
    return y.reshape(b, seq, D_MODEL)


def kernel(x_prompt, x_sample, pre_norm_g, w_in, rel_bias_table, attn_norm_g, conv_w, conv_b, dt_bias_fwd,
           dt_bias_bwd, a_log_fwd, a_log_bwd, d_skip, ssm_norm_g, w_out, post_norm_g):
    depth = w_in.shape[0]
    bias = _relbias(rel_bias_table)
    y_prompt, y_sample = x_prompt, x_sample
    for i in range(depth):
        w_main = w_in[i, :, :PROJ_MAIN].astype(bf16)
        w_dt = w_in[i, :, PROJ_MAIN:]
        w_dt = jnp.concatenate([w_dt[:, :SSM_HEADS].reshape(D_MODEL, SSM_GROUPS, HEADS_PER_GROUP),
                                w_dt[:, SSM_HEADS:].reshape(D_MODEL, SSM_GROUPS, HEADS_PER_GROUP)], axis=2)
        w_dt = jnp.pad(w_dt.reshape(D_MODEL, 2 * SSM_HEADS), ((0, 0), (0, LANES - 2 * SSM_HEADS))).astype(bf16)
        head_params = jnp.stack([_group_head_rows(dt_bias_fwd[i], dt_bias_bwd[i]),
                                 _group_head_rows(a_log_fwd[i], a_log_bwd[i])], axis=1)
        head_params = jnp.pad(head_params, ((0, 0), (0, 6), (0, 0)))
        args = (bias, pre_norm_g[i].reshape(1, D_MODEL), w_main, w_dt, attn_norm_g[i].reshape(1, ATTN_WIDTH),
                conv_w[i], conv_b[i].reshape(1, CONV_CH), head_params,
                jnp.repeat(d_skip[i], SSM_HEAD_DIM).reshape(1, SSM_WIDTH), ssm_norm_g[i].reshape(1, SSM_WIDTH),
                w_out[i].astype(bf16), post_norm_g[i].reshape(1, D_MODEL))
        y_prompt = _layer(y_prompt, *args)
        y_sample = _layer(y_sample, *args)
    return (y_prompt, y_sample)
```

```python
import functools
import math

import numpy as np
import jax
import jax.numpy as jnp
from jax import lax
from jax.experimental import pallas as pl
from jax.experimental.pallas import tpu as pltpu

D_MODEL = 2048
ATTN_HEADS = 16
HEAD_DIM = 128
ATTN_WIDTH = ATTN_HEADS * HEAD_DIM
DILATIONS = (1, 4, 16)
RADIUS = 64
N_BUCKETS = 32
BUCKET_MAX_DIST = 1024
SSM_HEADS = 32
SSM_HEAD_DIM = 64
SSM_WIDTH = SSM_HEADS * SSM_HEAD_DIM
SSM_GROUPS = 4
HEADS_PER_GROUP = SSM_HEADS // SSM_GROUPS
GROUP_WIDTH = SSM_WIDTH // SSM_GROUPS
SSM_STATE = 128
CONV_WIDTH = 5
CHUNK = 128
CONV_CH = SSM_WIDTH + 2 * SSM_GROUPS * SSM_STATE
MIX_WIDTH = ATTN_WIDTH + SSM_WIDTH
PROJ_MAIN = 4 * ATTN_WIDTH + SSM_WIDTH + CONV_CH
EPS = 1e-6

LANES = 128
BF16_ROWS = 16
QBLK = 128
KBLK = QBLK + 2 * RADIUS
TILE_GROUP = 16
LOG2E = math.log2(math.e)
NEG = -1e30
VMEM_LIMIT = 56 * 1024 * 1024

COL_Q, COL_K, COL_V, COL_ZA = 0, ATTN_WIDTH, 2 * ATTN_WIDTH, 3 * ATTN_WIDTH
COL_ZS = 4 * ATTN_WIDTH
COL_X = COL_ZS + SSM_WIDTH
COL_B = COL_X + SSM_WIDTH
COL_C = COL_B + SSM_GROUPS * SSM_STATE

f32 = jnp.float32
bf16 = jnp.bfloat16


def _params(*sem):
    return pltpu.CompilerParams(dimension_semantics=sem, vmem_limit_bytes=VMEM_LIMIT)


def _bucket_tiles():
    qi = np.arange(QBLK)[:, None]
    kj = np.arange(KBLK)[None, :]
    rel = kj - RADIUS - qi
    nb = N_BUCKETS // 2
    max_exact = nb // 2
    tiles = []
    for dil in DILATIONS:
        r = rel * dil
        n = np.abs(r)
        nf = np.maximum(n, 1).astype(np.float32)
        large = max_exact + (np.log(nf / np.float32(max_exact)) / np.float32(math.log(BUCKET_MAX_DIST / max_exact))
                             * np.float32(nb - max_exact)).astype(np.int32)
        large = np.minimum(large, nb - 1)
        bucket = np.where(r > 0, nb, 0) + np.where(n < max_exact, n, large)
        tiles.append(np.where(np.abs(rel) <= RADIUS, bucket, -1))
    return np.stack(tiles).astype(np.int32)


def _relbias_kernel(tab_ref, idx_ref, out_ref):
    h = pl.program_id(0)
    for p in range(len(DILATIONS)):
        idx = idx_ref[p]
        acc = jnp.full((QBLK, KBLK), NEG, f32)
        for b in range(N_BUCKETS):
            acc = jnp.where(idx == b, tab_ref[b, h] * LOG2E, acc)
        out_ref[p, 0] = acc


def _relbias(rel_table):
    idx = jnp.asarray(_bucket_tiles())
    np_ = len(DILATIONS)
    return pl.pallas_call(
        _relbias_kernel,
        grid=(ATTN_HEADS,),
        in_specs=[pl.BlockSpec(memory_space=pltpu.SMEM),
                  pl.BlockSpec((np_, QBLK, KBLK), lambda h: (0, 0, 0))],
        out_specs=pl.BlockSpec((np_, 1, QBLK, KBLK), lambda h: (0, h, 0, 0)),
        out_shape=jax.ShapeDtypeStruct((np_, ATTN_HEADS, QBLK, KBLK), f32),
        compiler_params=_params("arbitrary"),
        name="relbias",
    )(rel_table, idx)


INPROJ_TM = 1024
INPROJ_TN = 1024
NORM_ROWS = 128


def _inproj_kernel(x_ref, g_ref, w_ref, wdt_ref, proj_ref, dt_ref, h_ref):
    @pl.when(pl.program_id(1) == 0)
    def _():
        g = g_ref[...]

        def rows(i, carry):
            r = pl.ds(pl.multiple_of(i * NORM_ROWS, NORM_ROWS), NORM_ROWS)
            x = x_ref[r, :]
            y = x * lax.rsqrt(jnp.mean(x * x, axis=-1, keepdims=True) + EPS)
            h_ref[r, :] = (y * g).astype(bf16)
            return carry

        lax.fori_loop(0, INPROJ_TM // NORM_ROWS, rows, 0)
        dt_ref[...] = jnp.dot(h_ref[...], wdt_ref[...], preferred_element_type=f32)

    proj_ref[...] = jnp.dot(h_ref[...], w_ref[...], preferred_element_type=f32).astype(bf16)


def _inproj(x2, g, w_main, w_dt):
    m = x2.shape[0]
    return pl.pallas_call(
        _inproj_kernel,
        grid=(m // INPROJ_TM, PROJ_MAIN // INPROJ_TN),
        in_specs=[pl.BlockSpec((INPROJ_TM, D_MODEL), lambda i, j: (i, 0)),
                  pl.BlockSpec((1, D_MODEL), lambda i, j: (0, 0)),
                  pl.BlockSpec((D_MODEL, INPROJ_TN), lambda i, j: (0, j)),
                  pl.BlockSpec((D_MODEL, LANES), lambda i, j: (0, 0))],
        out_specs=[pl.BlockSpec((INPROJ_TM, INPROJ_TN), lambda i, j: (i, j)),
                   pl.BlockSpec((INPROJ_TM, LANES), lambda i, j: (i, 0))],
        out_shape=[jax.ShapeDtypeStruct((m, PROJ_MAIN), bf16),
                   jax.ShapeDtypeStruct((m, LANES), f32)],
        scratch_shapes=[pltpu.VMEM((INPROJ_TM, D_MODEL), bf16)],
        compiler_params=_params("arbitrary", "arbitrary"),
        name="inproj",
    )(x2, g, w_main, w_dt)


SCORE_SCALE = LOG2E / math.sqrt(HEAD_DIM)
CLASS_STEP = 4


def _attn_kernel(q_ref, k_ref, v_ref, bias_ref, o_ref,
                 tok_ref, cls4_ref, qd_ref, kd_ref, vd_ref, bias4_ref,
                 acc4_ref, m4_ref, l4_ref, acc1_ref, m1_ref, l1_ref,
                 s_buf, p_buf, mn_buf, rs_buf, *, seq):
    nblk = seq // QBLK
    n4 = seq // 4
    n16 = seq // 16
    srcs = (q_ref, k_ref, v_ref)
    dsts = ((qd_ref, 0), (kd_ref, RADIUS), (vd_ref, RADIUS))

    col = lax.broadcasted_iota(jnp.int32, (QBLK, KBLK), 1)
    before = col < RADIUS
    after = col >= QBLK + RADIUS
    for p in range(len(DILATIONS)):
        b = bias_ref[p, 0]
        bias4_ref[p, 0] = b
        bias4_ref[p, 1] = jnp.where(before, NEG, b)
        bias4_ref[p, 2] = jnp.where(after, NEG, b)
        bias4_ref[p, 3] = jnp.where(before | after, NEG, b)

    pad = jnp.zeros((RADIUS, HEAD_DIM), bf16)
    for ref in (kd_ref, vd_ref):
        ref[0:RADIUS, :] = pad
        ref[RADIUS + seq:2 * RADIUS + seq, :] = pad

    def stage(i, carry):
        r = pl.ds(pl.multiple_of(i * QBLK, QBLK), QBLK)
        for x, src in enumerate(srcs):
            tok_ref[x, r, :] = src[0, r, :].astype(f32)
        return carry

    lax.fori_loop(0, nblk, stage, 0)
    for x in range(3):
        for c4 in range(CLASS_STEP):
            cls4_ref[x, c4 * n4:(c4 + 1) * n4, :] = tok_ref[x, pl.ds(c4, n4, stride=CLASS_STEP), :]

    def tile_rows(t):
        return pl.ds(pl.multiple_of(t * QBLK, QBLK), QBLK)

    def scores(t, u, p, tiles_per_class):
        r0 = pl.multiple_of(t * QBLK, QBLK)
        q = qd_ref[pl.ds(r0, QBLK), :]
        k = kd_ref[pl.ds(r0, KBLK), :]
        jj = lax.rem(t, tiles_per_class)
        variant = (jj == 0).astype(jnp.int32) + 2 * (jj == tiles_per_class - 1).astype(jnp.int32)
        s = lax.dot_general(q, k, (((1,), (1,)), ((), ())), preferred_element_type=f32)
        s_buf[u] = s * SCORE_SCALE + bias4_ref[p, variant]

    def softmax(t, u, prev):
        s = s_buf[u]
        top = jnp.max(s, axis=-1, keepdims=True)
        if prev is None:
            mn = jnp.broadcast_to(top, (QBLK, HEAD_DIM))
        else:
            mn = jnp.maximum(prev[1][tile_rows(t), :], top)
        e_lo = jnp.exp2(s[:, :HEAD_DIM] - mn)
        e_hi = jnp.exp2(s[:, HEAD_DIM:] - mn)
        p_buf[u, :, :HEAD_DIM] = e_lo.astype(bf16)
        p_buf[u, :, HEAD_DIM:] = e_hi.astype(bf16)
        mn_buf[u] = mn
        rs_buf[u] = jnp.broadcast_to(jnp.sum(e_lo + e_hi, axis=-1, keepdims=True), (QBLK, HEAD_DIM))

    def update(t, u, prev, nxt, dst_rows):
        r = tile_rows(t)
        v = vd_ref[pl.ds(pl.multiple_of(t * QBLK, QBLK), KBLK), :]
        acc = jnp.dot(p_buf[u], v, preferred_element_type=f32)
        mn = mn_buf[u]
        l = rs_buf[u]
        if prev is not None:
            alpha = jnp.exp2(prev[1][r, :] - mn)
            l = alpha * prev[2][r, :] + l
            acc = alpha * prev[0][r, :] + acc
        if nxt is None:
            o_ref[0, r, :] = (acc / l).astype(o_ref.dtype)
        else:
            nxt[0][dst_rows, :] = acc
            nxt[1][dst_rows, :] = mn
            nxt[2][dst_rows, :] = l

    def run_pattern(p, prev, nxt, dst_rows_fn):
        tiles_per_class = seq // DILATIONS[p] // QBLK

        def group(i, carry):
            tiles = [i * TILE_GROUP + u for u in range(TILE_GROUP)]
            for u, t in enumerate(tiles):
                scores(t, u, p, tiles_per_class)
            for u, t in enumerate(tiles):
                softmax(t, u, prev)
            for u, t in enumerate(tiles):
                dst = None if nxt is None else dst_rows_fn(t // tiles_per_class, lax.rem(t, tiles_per_class))
                update(t, u, prev, nxt, dst)
            return carry

        lax.fori_loop(0, nblk // TILE_GROUP, group, 0)

    for c16 in range(16):
        rows = pl.ds((c16 % 4) * n4 + c16 // 4, n16, stride=CLASS_STEP)
        for x, (dst, off) in enumerate(dsts):
            dst[off + c16 * n16:off + (c16 + 1) * n16, :] = cls4_ref[x, rows, :].astype(bf16)
    state4 = (acc4_ref, m4_ref, l4_ref)
    state1 = (acc1_ref, m1_ref, l1_ref)
    run_pattern(2, None, state4,
                lambda c16, jj: pl.ds(lax.rem(c16, 4) * n4 + c16 // 4 + jj * (QBLK * CLASS_STEP), QBLK,
                                      stride=CLASS_STEP))

    def copy_cls4(i, carry):
        r0 = pl.multiple_of(i * QBLK, QBLK)
        for x, (dst, off) in enumerate(dsts):
            dst[pl.ds(r0 + off, QBLK), :] = cls4_ref[x, pl.ds(r0, QBLK), :].astype(bf16)
        return carry

    lax.fori_loop(0, nblk, copy_cls4, 0)
    run_pattern(1, state4, state1,
                lambda c4, jj: pl.ds(c4 + jj * (QBLK * CLASS_STEP), QBLK, stride=CLASS_STEP))

    def copy_tok(i, carry):
        r0 = pl.multiple_of(i * QBLK, QBLK)
        for src, (dst, off) in zip(srcs, dsts):
            dst[pl.ds(r0 + off, QBLK), :] = src[0, pl.ds(r0, QBLK), :]
        return carry

    lax.fori_loop(0, nblk, copy_tok, 0)
    run_pattern(0, state1, None, None)


def _attention(proj3, bias):
    b, seq, _ = proj3.shape
    np_ = len(DILATIONS)
    qkv = lambda off: pl.BlockSpec((1, seq, HEAD_DIM), lambda i, h: (i, 0, off // HEAD_DIM + h))
    return pl.pallas_call(
        functools.partial(_attn_kernel, seq=seq),
        grid=(b, ATTN_HEADS),
        in_specs=[qkv(COL_Q), qkv(COL_K), qkv(COL_V),
                  pl.BlockSpec((np_, 1, QBLK, KBLK), lambda i, h: (0, h, 0, 0))],
        out_specs=pl.BlockSpec((1, seq, HEAD_DIM), lambda i, h: (i, 0, h)),
        out_shape=jax.ShapeDtypeStruct((b, seq, ATTN_WIDTH), bf16),
        scratch_shapes=[pltpu.VMEM((3, seq, HEAD_DIM), f32)] * 2
                       + [pltpu.VMEM((seq, HEAD_DIM), bf16)]
                       + [pltpu.VMEM((seq + 2 * RADIUS, HEAD_DIM), bf16)] * 2
                       + [pltpu.VMEM((np_, 4, QBLK, KBLK), f32)]
                       + [pltpu.VMEM((seq, HEAD_DIM), f32)] * 6
                       + [pltpu.VMEM((TILE_GROUP, QBLK, KBLK), f32),
                          pltpu.VMEM((TILE_GROUP, QBLK, KBLK), bf16),
                          pltpu.VMEM((TILE_GROUP, QBLK, HEAD_DIM), f32),
                          pltpu.VMEM((TILE_GROUP, QBLK, HEAD_DIM), f32)],
        compiler_params=_params("arbitrary", "arbitrary"),
        name="attn",
    )(proj3, proj3, proj3, bias)


HALO = BF16_ROWS


def _silu(x):
    return x / (1.0 + jnp.exp(-x))


def _softplus(x):
    return jnp.maximum(x, 0.0) + jnp.log1p(jnp.exp(-jnp.abs(x)))


def _ssd_kernel(z_ref, x_ref, bm_ref, cm_ref, dt_ref, cwx_ref, cbx_ref, cwb_ref, cbb_ref, cwc_ref, cbc_ref,
                hp_ref, dskip_ref, ng_ref, out_ref,
                xs_ref, bs_ref, cs_ref, y_ref, hf_ref, hb_ref, tr_ref, *, seq):
    nc = seq // CHUNK
    g = pl.program_id(1)
    half = CONV_WIDTH // 2

    def conv_chunk(i, carry):
        r0 = pl.multiple_of(i * CHUNK, CHUNK)
        lo = pl.multiple_of(jnp.maximum(r0 - HALO, 0), HALO)
        hi = pl.multiple_of(jnp.minimum(r0 + CHUNK, seq - HALO), HALO)
        has_lo = jnp.where(i > 0, 1.0, 0.0)
        has_hi = jnp.where(i < nc - 1, 1.0, 0.0)
        for src, w_ref, b_ref, dst in ((x_ref, cwx_ref, cbx_ref, xs_ref),
                                       (bm_ref, cwb_ref, cbb_ref, bs_ref),
                                       (cm_ref, cwc_ref, cbc_ref, cs_ref)):
            ext = jnp.concatenate([
                src[0, pl.ds(lo, HALO), :].astype(f32) * has_lo,
                src[0, pl.ds(r0, CHUNK), :].astype(f32),
                src[0, pl.ds(hi, HALO), :].astype(f32) * has_hi], axis=0)
            acc = b_ref[...]
            for tap in range(CONV_WIDTH):
                start = HALO - half + tap
                acc = acc + w_ref[tap:tap + 1, :] * ext[start:start + CHUNK, :]
            dst[pl.ds(r0, CHUNK), :] = _silu(acc).astype(bf16)
        return carry

    lax.fori_loop(0, nc, conv_chunk, 0)

    lane = lax.broadcasted_iota(jnp.int32, (CHUNK, LANES), 1)
    li = lax.broadcasted_iota(jnp.int32, (CHUNK, CHUNK), 0)
    si = lax.broadcasted_iota(jnp.int32, (CHUNK, CHUNK), 1)
    lower = li >= si
    upper = si >= li
    tril = lower.astype(f32)
    dt_bias = hp_ref[0, 0:1, :]
    a_row = -jnp.exp(hp_ref[0, 1:2, :])
    shift = lax.rem(LANES - 2 * HEADS_PER_GROUP * g, LANES)

    def chunk_terms(c):
        r = pl.ds(pl.multiple_of(c * CHUNK, CHUNK), CHUNK)
        raw = pltpu.roll(dt_ref[0, r, :], shift, 1)
        dtv = _softplus(raw + dt_bias)
        adt = dtv * a_row
        cum = jnp.dot(tril, adt, preferred_element_type=f32, precision=lax.Precision.HIGHEST)
        total = cum[CHUNK - 1:CHUNK, :]
        suffix = total - cum + adt
        acs = jnp.where(lane < HEADS_PER_GROUP, cum, suffix)
        tr_ref[0] = acs.T
        tr_ref[1] = dtv.T
        return r, acs, total

    hf_ref[...] = jnp.zeros_like(hf_ref)
    hb_ref[...] = jnp.zeros_like(hb_ref)

    def forward(c, carry):
        r, acs, total = chunk_terms(c)
        cmat = cs_ref[r, :]
        bmat = bs_ref[r, :]
        cf = cmat.astype(f32)
        bt = bmat.astype(f32).T
        cb = lax.dot_general(cmat, bmat, (((1,), (1,)), ((), ())), preferred_element_type=f32)
        for j in range(HEADS_PER_GROUP):
            jb = HEADS_PER_GROUP + j
            a_f = jnp.broadcast_to(acs[:, j:j + 1], (CHUNK, CHUNK))
            a_b = jnp.broadcast_to(acs[:, jb:jb + 1], (CHUNK, CHUNK))
            r_f, r_b = tr_ref[0, j:j + 1, :], tr_ref[0, jb:jb + 1, :]
            dt_f, dt_b = tr_ref[1, j:j + 1, :], tr_ref[1, jb:jb + 1, :]
            w = (jnp.exp(jnp.where(lower, a_f - r_f, NEG)) * dt_f
                 + jnp.exp(jnp.where(upper, a_b - r_b, NEG)) * dt_b)
            xh = xs_ref[r, j * SSM_HEAD_DIM:(j + 1) * SSM_HEAD_DIM]
            y = jnp.dot((cb * w).astype(bf16), xh, preferred_element_type=f32)
            y = y + jnp.dot((cf * jnp.exp(a_f)).astype(bf16), hf_ref[j].astype(bf16),
                            preferred_element_type=f32)
            y_ref[r, j * SSM_HEAD_DIM:(j + 1) * SSM_HEAD_DIM] = y
            tot = total[:, j:j + 1]
            inject = (bt * (jnp.exp(tot - r_f) * dt_f)).astype(bf16)
            hf_ref[j] = hf_ref[j] * jnp.exp(tot) + jnp.dot(inject, xh, preferred_element_type=f32)
        y_ref[r, :] = y_ref[r, :] + dskip_ref[...] * xs_ref[r, :].astype(f32)
        return carry

    lax.fori_loop(0, nc, forward, 0)

    def backward(i, carry):
        c = nc - 1 - i
        r, acs, total = chunk_terms(c)
        cf = cs_ref[r, :].astype(f32)
        bt = bs_ref[r, :].astype(f32).T
        for j in range(HEADS_PER_GROUP):
            jb = HEADS_PER_GROUP + j
            a_b = jnp.broadcast_to(acs[:, jb:jb + 1], (CHUNK, CHUNK))
            r_b, dt_b = tr_ref[0, jb:jb + 1, :], tr_ref[1, jb:jb + 1, :]
            cols = slice(j * SSM_HEAD_DIM, (j + 1) * SSM_HEAD_DIM)
            y_ref[r, cols] = y_ref[r, cols] + jnp.dot((cf * jnp.exp(a_b)).astype(bf16), hb_ref[j].astype(bf16),
                                                      preferred_element_type=f32)
            tot = total[:, jb:jb + 1]
            inject = (bt * (jnp.exp(tot - r_b) * dt_b)).astype(bf16)
            hb_ref[j] = hb_ref[j] * jnp.exp(tot) + jnp.dot(inject, xs_ref[r, cols], preferred_element_type=f32)
        y = y_ref[r, :] * _silu(z_ref[0, r, :].astype(f32))
        y = y * lax.rsqrt(jnp.mean(y * y, axis=-1, keepdims=True) + EPS)
        out_ref[0, r, :] = (y * ng_ref[...]).astype(out_ref.dtype)
        return carry

    lax.fori_loop(0, nc, backward, 0)


def _ssd(proj3, dt3, conv_w, conv_b, head_params, dskip_cols, norm_g):
    b, seq, _ = proj3.shape
    gw, ns = GROUP_WIDTH, SSM_STATE
    wide = lambda off: pl.BlockSpec((1, seq, gw), lambda i, g: (i, 0, off // gw + g))
    narrow = lambda off: pl.BlockSpec((1, seq, ns), lambda i, g: (i, 0, off // ns + g))
    xoff, boff, coff = 0, SSM_WIDTH, SSM_WIDTH + SSM_GROUPS * ns
    cw = lambda off, w: pl.BlockSpec((CONV_WIDTH, w), lambda i, g: (0, off // w + g))
    cbias = lambda off, w: pl.BlockSpec((1, w), lambda i, g: (0, off // w + g))
    return pl.pallas_call(
        functools.partial(_ssd_kernel, seq=seq),
        grid=(b, SSM_GROUPS),
        in_specs=[wide(COL_ZS), wide(COL_X), narrow(COL_B), narrow(COL_C),
                  pl.BlockSpec((1, seq, LANES), lambda i, g: (i, 0, 0)),
                  cw(xoff, gw), cbias(xoff, gw), cw(boff, ns), cbias(boff, ns), cw(coff, ns), cbias(coff, ns),
                  pl.BlockSpec((1, 8, LANES), lambda i, g: (g, 0, 0)),
                  pl.BlockSpec((1, gw), lambda i, g: (0, g)),
                  pl.BlockSpec((1, gw), lambda i, g: (0, g))],
        out_specs=pl.BlockSpec((1, seq, gw), lambda i, g: (i, 0, g)),
        out_shape=jax.ShapeDtypeStruct((b, seq, SSM_WIDTH), bf16),
        scratch_shapes=[pltpu.VMEM((seq, gw), bf16),
                        pltpu.VMEM((seq, ns), bf16),
                        pltpu.VMEM((seq, ns), bf16),
                        pltpu.VMEM((seq, gw), f32),
                        pltpu.VMEM((HEADS_PER_GROUP, ns, SSM_HEAD_DIM), f32),
                        pltpu.VMEM((HEADS_PER_GROUP, ns, SSM_HEAD_DIM), f32),
                        pltpu.VMEM((2, LANES, CHUNK), f32)],
        compiler_params=_params("arbitrary", "arbitrary"),
        name="ssd",
    )(proj3, proj3, proj3, proj3, dt3, conv_w, conv_b, conv_w, conv_b, conv_w, conv_b,
      head_params, dskip_cols, norm_g)


OUTPROJ_TM = 256


def _outproj_kernel(o_ref, z_ref, s_ref, x_ref, w_ref, ga_ref, gp_ref, y_ref):
    o = o_ref[...].astype(f32)
    a = o * lax.rsqrt(jnp.mean(o * o, axis=-1, keepdims=True) + EPS) * ga_ref[...]
    a = a * _silu(z_ref[...].astype(f32))
    mix = jnp.dot(a.astype(bf16), w_ref[0:ATTN_WIDTH, :], preferred_element_type=f32)
    mix = mix + jnp.dot(s_ref[...], w_ref[ATTN_WIDTH:MIX_WIDTH, :], preferred_element_type=f32)
    mix = mix * lax.rsqrt(jnp.mean(mix * mix, axis=-1, keepdims=True) + EPS) * gp_ref[...]
    y_ref[...] = x_ref[...] + mix


def _outproj(o2, proj2, s2, x2, w_out, g_attn, g_post):
    m = x2.shape[0]
    tm = OUTPROJ_TM
    row = lambda width, blk: pl.BlockSpec((tm, width), lambda i: (i, blk))
    const = lambda shape: pl.BlockSpec(shape, lambda i: (0, 0))
    return pl.pallas_call(
        _outproj_kernel,
        grid=(m // tm,),
        in_specs=[row(ATTN_WIDTH, 0), row(ATTN_WIDTH, COL_ZA // ATTN_WIDTH), row(SSM_WIDTH, 0), row(D_MODEL, 0),
                  pl.BlockSpec((MIX_WIDTH, D_MODEL), lambda i: (0, 0), pipeline_mode=pl.Buffered(1)),
                  const((1, ATTN_WIDTH)), const((1, D_MODEL))],
        out_specs=row(D_MODEL, 0),
        out_shape=jax.ShapeDtypeStruct((m, D_MODEL), f32),
        compiler_params=_params("arbitrary"),
        name="outproj",
    )(o2, proj2, s2, x2, w_out, g_attn, g_post)


def _group_head_rows(fwd, bwd):
    rows = jnp.concatenate([fwd.reshape(SSM_GROUPS, HEADS_PER_GROUP), bwd.reshape(SSM_GROUPS, HEADS_PER_GROUP)],
                           axis=1)
    return jnp.pad(rows, ((0, 0), (0, LANES - 2 * HEADS_PER_GROUP)))


def _layer(x, bias, pre_g, w_main, w_dt, attn_g, conv_w, conv_b, head_params, dskip_cols, ssm_g, w_out, post_g):
    b, seq, _ = x.shape
    x2 = x.reshape(b * seq, D_MODEL)
    proj2, dt2 = _inproj(x2, pre_g, w_main, w_dt)
    proj3 = proj2.reshape(b, seq, PROJ_MAIN)
    o = _attention(proj3, bias)
    s = _ssd(proj3, dt2.reshape(b, seq, LANES), conv_w, conv_b, head_params, dskip_cols, ssm_g)
    y = _outproj(o.reshape(b * seq, ATTN_WIDTH), proj2, s.reshape(b * seq, SSM_WIDTH), x2, w_out, attn_g, post_g)
    return y.reshape(b, seq, D_MODEL)


def kernel(x_prompt, x_sample, pre_norm_g, w_in, rel_bias_table, attn_norm_g, conv_w, conv_b, dt_bias_fwd,
           dt_bias_bwd, a_log_fwd, a_log_bwd, d_skip, ssm_norm_g, w_out, post_norm_g):
    depth = w_in.shape[0]
    bias = _relbias(rel_bias_table)
    y_prompt, y_sample = x_prompt, x_sample
    for i in range(depth):
        w_main = w_in[i, :, :PROJ_MAIN].astype(bf16)
        w_dt = w_in[i, :, PROJ_MAIN:]
        w_dt = jnp.concatenate([w_dt[:, :SSM_HEADS].reshape(D_MODEL, SSM_GROUPS, HEADS_PER_GROUP),
                                w_dt[:, SSM_HEADS:].reshape(D_MODEL, SSM_GROUPS, HEADS_PER_GROUP)], axis=2)
        w_dt = jnp.pad(w_dt.reshape(D_MODEL, 2 * SSM_HEADS), ((0, 0), (0, LANES - 2 * SSM_HEADS))).astype(bf16)
        head_params = jnp.stack([_group_head_rows(dt_bias_fwd[i], dt_bias_bwd[i]),
                                 _group_head_rows(a_log_fwd[i], a_log_bwd[i])], axis=1)
        head_params = jnp.pad(head_params, ((0, 0), (0, 6), (0, 0)))
        args = (bias, pre_norm_g[i].reshape(1, D_MODEL), w_main, w_dt, attn_norm_g[i].reshape(1, ATTN_WIDTH),
                conv_w[i], conv_b[i].reshape(1, CONV_CH), head_params,
                jnp.repeat(d_skip[i], SSM_HEAD_DIM).reshape(1, SSM_WIDTH), ssm_norm_g[i].reshape(1, SSM_WIDTH),
                w_out[i].astype(bf16), post_norm_g[i].reshape(1, D_MODEL))
        y_prompt = _layer(y_prompt, *args)
        y_sample = _layer(y_sample, *args)
    return (y_prompt, y_sample)
```

```python
import functools
import math

import numpy as np
import jax
import jax.numpy as jnp
from jax import lax
from jax.experimental import pallas as pl
from jax.experimental.pallas import tpu as pltpu

D_MODEL = 2048
ATTN_HEADS = 16
HEAD_DIM = 128
ATTN_WIDTH = ATTN_HEADS * HEAD_DIM
DILATIONS = (1, 4, 16)
RADIUS = 64
N_BUCKETS = 32
BUCKET_MAX_DIST = 1024
SSM_HEADS = 32
SSM_HEAD_DIM = 64
SSM_WIDTH = SSM_HEADS * SSM_HEAD_DIM
SSM_GROUPS = 4
HEADS_PER_GROUP = SSM_HEADS // SSM_GROUPS
GROUP_WIDTH = SSM_WIDTH // SSM_GROUPS
SSM_STATE = 128
CONV_WIDTH = 5
CHUNK = 128
CONV_CH = SSM_WIDTH + 2 * SSM_GROUPS * SSM_STATE
MIX_WIDTH = ATTN_WIDTH + SSM_WIDTH
PROJ_MAIN = 4 * ATTN_WIDTH + SSM_WIDTH + CONV_CH
EPS = 1e-6

LANES = 128
BF16_ROWS = 16
QBLK = 128
KBLK = QBLK + 2 * RADIUS
TILE_GROUP = 16
LOG2E = math.log2(math.e)
NEG = -1e30
VMEM_LIMIT = 56 * 1024 * 1024

COL_Q, COL_K, COL_V, COL_ZA = 0, ATTN_WIDTH, 2 * ATTN_WIDTH, 3 * ATTN_WIDTH
COL_ZS = 4 * ATTN_WIDTH
COL_X = COL_ZS + SSM_WIDTH
COL_B = COL_X + SSM_WIDTH
COL_C = COL_B + SSM_GROUPS * SSM_STATE

f32 = jnp.float32
bf16 = jnp.bfloat16


def _params(*sem):
    return pltpu.CompilerParams(dimension_semantics=sem, vmem_limit_bytes=VMEM_LIMIT)


def _bucket_tiles():
    qi = np.arange(QBLK)[:, None]
    kj = np.arange(KBLK)[None, :]
    rel = kj - RADIUS - qi
    nb = N_BUCKETS // 2
    max_exact = nb // 2
    tiles = []
    for dil in DILATIONS:
        r = rel * dil
        n = np.abs(r)
        nf = np.maximum(n, 1).astype(np.float32)
        large = max_exact + (np.log(nf / np.float32(max_exact)) / np.float32(math.log(BUCKET_MAX_DIST / max_exact))
                             * np.float32(nb - max_exact)).astype(np.int32)
        large = np.minimum(large, nb - 1)
        bucket = np.where(r > 0, nb, 0) + np.where(n < max_exact, n, large)
        tiles.append(np.where(np.abs(rel) <= RADIUS, bucket, -1))
    return np.stack(tiles).astype(np.int32)


def _relbias_kernel(tab_ref, idx_ref, out_ref):
    h = pl.program_id(0)
    for p in range(len(DILATIONS)):
        idx = idx_ref[p]
        acc = jnp.full((QBLK, KBLK), NEG, f32)
        for b in range(N_BUCKETS):
            acc = jnp.where(idx == b, tab_ref[b, h] * LOG2E, acc)
        out_ref[p, 0] = acc


def _relbias(rel_table):
    idx = jnp.asarray(_bucket_tiles())
    np_ = len(DILATIONS)
    return pl.pallas_call(
        _relbias_kernel,
        grid=(ATTN_HEADS,),
        in_specs=[pl.BlockSpec(memory_space=pltpu.SMEM),
                  pl.BlockSpec((np_, QBLK, KBLK), lambda h: (0, 0, 0))],
        out_specs=pl.BlockSpec((np_, 1, QBLK, KBLK), lambda h: (0, h, 0, 0)),
        out_shape=jax.ShapeDtypeStruct((np_, ATTN_HEADS, QBLK, KBLK), f32),
        compiler_params=_params("arbitrary"),
        name="relbias",
    )(rel_table, idx)


INPROJ_TM = 1024
INPROJ_TN = 1024
NORM_ROWS = 128


def _inproj_kernel(x_ref, g_ref, w_ref, wdt_ref, proj_ref, dt_ref, h_ref):
    @pl.when(pl.program_id(1) == 0)
    def _():
        g = g_ref[...]

        def rows(i, carry):
            r = pl.ds(pl.multiple_of(i * NORM_ROWS, NORM_ROWS), NORM_ROWS)
            x = x_ref[r, :]
            y = x * lax.rsqrt(jnp.mean(x * x, axis=-1, keepdims=True) + EPS)
            h_ref[r, :] = (y * g).astype(bf16)
            return carry

        lax.fori_loop(0, INPROJ_TM // NORM_ROWS, rows, 0)
        dt_ref[...] = jnp.dot(h_ref[...], wdt_ref[...], preferred_element_type=f32)

    proj_ref[...] = jnp.dot(h_ref[...], w_ref[...], preferred_element_type=f32).astype(bf16)


def _inproj(x2, g, w_main, w_dt):
    m = x2.shape[0]
    return pl.pallas_call(
        _inproj_kernel,
        grid=(m // INPROJ_TM, PROJ_MAIN // INPROJ_TN),
        in_specs=[pl.BlockSpec((INPROJ_TM, D_MODEL), lambda i, j: (i, 0)),
                  pl.BlockSpec((1, D_MODEL), lambda i, j: (0, 0)),
                  pl.BlockSpec((D_MODEL, INPROJ_TN), lambda i, j: (0, j)),
                  pl.BlockSpec((D_MODEL, LANES), lambda i, j: (0, 0))],
        out_specs=[pl.BlockSpec((INPROJ_TM, INPROJ_TN), lambda i, j: (i, j)),
                   pl.BlockSpec((INPROJ_TM, LANES), lambda i, j: (i, 0))],
        out_shape=[jax.ShapeDtypeStruct((m, PROJ_MAIN), bf16),
                   jax.ShapeDtypeStruct((m, LANES), f32)],
        scratch_shapes=[pltpu.VMEM((INPROJ_TM, D_MODEL), bf16)],
        compiler_params=_params("arbitrary", "arbitrary"),
        name="inproj",
    )(x2, g, w_main, w_dt)


SCORE_SCALE = LOG2E / math.sqrt(HEAD_DIM)
CLASS_STEP = 4


def _attn_kernel(q_ref, k_ref, v_ref, bias_ref, o_ref,
                 tok_ref, cls4_ref, qd_ref, kd_ref, vd_ref, bias4_ref,
                 acc4_ref, m4_ref, l4_ref, acc1_ref, m1_ref, l1_ref,
                 s_buf, p_buf, mn_buf, rs_buf, *, seq):
    nblk = seq // QBLK
    n4 = seq // 4
    n16 = seq // 16
    srcs = (q_ref, k_ref, v_ref)
    dsts = ((qd_ref, 0), (kd_ref, RADIUS), (vd_ref, RADIUS))

    col = lax.broadcasted_iota(jnp.int32, (QBLK, KBLK), 1)
    before = col < RADIUS
    after = col >= QBLK + RADIUS
    for p in range(len(DILATIONS)):
        b = bias_ref[p, 0]
        bias4_ref[p, 0] = b
        bias4_ref[p, 1] = jnp.where(before, NEG, b)
        bias4_ref[p, 2] = jnp.where(after, NEG, b)
        bias4_ref[p, 3] = jnp.where(before | after, NEG, b)

    pad = jnp.zeros((RADIUS, HEAD_DIM), bf16)
    for ref in (kd_ref, vd_ref):
        ref[0:RADIUS, :] = pad
        ref[RADIUS + seq:2 * RADIUS + seq, :] = pad

    def stage(i, carry):
        r = pl.ds(pl.multiple_of(i * QBLK, QBLK), QBLK)
        for x, src in enumerate(srcs):
            tok_ref[x, r, :] = src[0, r, :].astype(f32)
        return carry

    lax.fori_loop(0, nblk, stage, 0)
    for x in range(3):
        for c4 in range(CLASS_STEP):
            cls4_ref[x, c4 * n4:(c4 + 1) * n4, :] = tok_ref[x, pl.ds(c4, n4, stride=CLASS_STEP), :]

    def tile_rows(t):
        return pl.ds(pl.multiple_of(t * QBLK, QBLK), QBLK)

    def scores(t, u, p, tiles_per_class):
        r0 = pl.multiple_of(t * QBLK, QBLK)
        q = qd_ref[pl.ds(r0, QBLK), :]
        k = kd_ref[pl.ds(r0, KBLK), :]
        jj = lax.rem(t, tiles_per_class)
        variant = (jj == 0).astype(jnp.int32) + 2 * (jj == tiles_per_class - 1).astype(jnp.int32)
        s = lax.dot_general(q, k, (((1,), (1,)), ((), ())), preferred_element_type=f32)
        s_buf[u] = s * SCORE_SCALE + bias4_ref[p, variant]

    def softmax(t, u, prev):
        s = s_buf[u]
        top = jnp.max(s, axis=-1, keepdims=True)
        if prev is None:
            mn = jnp.broadcast_to(top, (QBLK, HEAD_DIM))
        else:
            mn = jnp.maximum(prev[1][tile_rows(t), :], top)
        e_lo = jnp.exp2(s[:, :HEAD_DIM] - mn)
        e_hi = jnp.exp2(s[:, HEAD_DIM:] - mn)
        p_buf[u, :, :HEAD_DIM] = e_lo.astype(bf16)
        p_buf[u, :, HEAD_DIM:] = e_hi.astype(bf16)
        mn_buf[u] = mn
        rs_buf[u] = jnp.broadcast_to(jnp.sum(e_lo + e_hi, axis=-1, keepdims=True), (QBLK, HEAD_DIM))

    def update(t, u, prev, nxt, dst_rows):
        r = tile_rows(t)
        v = vd_ref[pl.ds(pl.multiple_of(t * QBLK, QBLK), KBLK), :]
        acc = jnp.dot(p_buf[u], v, preferred_element_type=f32)
        mn = mn_buf[u]
        l = rs_buf[u]
        if prev is not None:
            alpha = jnp.exp2(prev[1][r, :] - mn)
            l = alpha * prev[2][r, :] + l
            acc = alpha * prev[0][r, :] + acc
        if nxt is None:
            o_ref[0, r, :] = (acc / l).astype(o_ref.dtype)
        else:
            nxt[0][dst_rows, :] = acc
            nxt[1][dst_rows, :] = mn
            nxt[2][dst_rows, :] = l

    def run_pattern(p, prev, nxt, dst_rows_fn):
        tiles_per_class = seq // DILATIONS[p] // QBLK

        def group(i, carry):
            tiles = [i * TILE_GROUP + u for u in range(TILE_GROUP)]
            for u, t in enumerate(tiles):
                scores(t, u, p, tiles_per_class)
            for u, t in enumerate(tiles):
                softmax(t, u, prev)
            for u, t in enumerate(tiles):
                dst = None if nxt is None else dst_rows_fn(t // tiles_per_class, lax.rem(t, tiles_per_class))
                update(t, u, prev, nxt, dst)
            return carry

        lax.fori_loop(0, nblk // TILE_GROUP, group, 0)

    for c16 in range(16):
        rows = pl.ds((c16 % 4) * n4 + c16 // 4, n16, stride=CLASS_STEP)
        for x, (dst, off) in enumerate(dsts):
            dst[off + c16 * n16:off + (c16 + 1) * n16, :] = cls4_ref[x, rows, :].astype(bf16)
    state4 = (acc4_ref, m4_ref, l4_ref)
    state1 = (acc1_ref, m1_ref, l1_ref)
    run_pattern(2, None, state4,
                lambda c16, jj: pl.ds(lax.rem(c16, 4) * n4 + c16 // 4 + jj * (QBLK * CLASS_STEP), QBLK,
                                      stride=CLASS_STEP))

    def copy_cls4(i, carry):
        r0 = pl.multiple_of(i * QBLK, QBLK)
        for x, (dst, off) in enumerate(dsts):
            dst[pl.ds(r0 + off, QBLK), :] = cls4_ref[x, pl.ds(r0, QBLK), :].astype(bf16)
        return carry

    lax.fori_loop(0, nblk, copy_cls4, 0)
    run_pattern(1, state4, state1,
                lambda c4, jj: pl.ds(c4 + jj * (QBLK * CLASS_STEP), QBLK, stride=CLASS_STEP))

    def copy_tok(i, carry):
        r0 = pl.multiple_of(i * QBLK, QBLK)
        for src, (dst, off) in zip(srcs, dsts):
            dst[pl.ds(r0 + off, QBLK), :] = src[0, pl.ds(r0, QBLK), :]
        return carry

    lax.fori_loop(0, nblk, copy_tok, 0)
    run_pattern(0, state1, None, None)


def _attention(proj3, bias):
    b, seq, _ = proj3.shape
    np_ = len(DILATIONS)
    qkv = lambda off: pl.BlockSpec((1, seq, HEAD_DIM), lambda i, h: (i, 0, off // HEAD_DIM + h))
    return pl.pallas_call(
        functools.partial(_attn_kernel, seq=seq),
        grid=(b, ATTN_HEADS),
        in_specs=[qkv(COL_Q), qkv(COL_K), qkv(COL_V),
                  pl.BlockSpec((np_, 1, QBLK, KBLK), lambda i, h: (0, h, 0, 0))],
        out_specs=pl.BlockSpec((1, seq, HEAD_DIM), lambda i, h: (i, 0, h)),
        out_shape=jax.ShapeDtypeStruct((b, seq, ATTN_WIDTH), bf16),
        scratch_shapes=[pltpu.VMEM((3, seq, HEAD_DIM), f32)] * 2
                       + [pltpu.VMEM((seq, HEAD_DIM), bf16)]
                       + [pltpu.VMEM((seq + 2 * RADIUS, HEAD_DIM), bf16)] * 2
                       + [pltpu.VMEM((np_, 4, QBLK, KBLK), f32)]
                       + [pltpu.VMEM((seq, HEAD_DIM), f32)] * 6
                       + [pltpu.VMEM((TILE_GROUP, QBLK, KBLK), f32),
                          pltpu.VMEM((TILE_GROUP, QBLK, KBLK), bf16),
                          pltpu.VMEM((TILE_GROUP, QBLK, HEAD_DIM), f32),
                          pltpu.VMEM((TILE_GROUP, QBLK, HEAD_DIM), f32)],
        compiler_params=_params("arbitrary", "arbitrary"),
        name="attn",
    )(proj3, proj3, proj3, bias)


HALO = BF16_ROWS


def _silu(x):
    h = 0.5 * x
    return h + h * jnp.tanh(h)


def _softplus(x):
    return jnp.maximum(x, 0.0) + jnp.log1p(jnp.exp(-jnp.abs(x)))


def _ssd_kernel(z_ref, x_ref, bm_ref, cm_ref, dt_ref, cwx_ref, cbx_ref, cwb_ref, cbb_ref, cwc_ref, cbc_ref,
                hp_ref, dskip_ref, ng_ref, out_ref,
                xs_ref, bs_ref, cs_ref, bt_ref, y_ref, acs_ref, rows_ref, decx_ref, hf_ref, hb_ref,
                st_ref, ost_ref, *, seq):
    nc = seq // CHUNK
    g = pl.program_id(1)
    half = CONV_WIDTH // 2
    nh = HEADS_PER_GROUP

    lane = lax.broadcasted_iota(jnp.int32, (CHUNK, LANES), 1)
    left = lane < SSM_HEAD_DIM
    mask_lo = jnp.where(left, 1.0, 0.0).astype(bf16)
    mask_hi = jnp.where(left, 0.0, 1.0).astype(bf16)
    li = lax.broadcasted_iota(jnp.int32, (CHUNK, CHUNK), 0)
    si = lax.broadcasted_iota(jnp.int32, (CHUNK, CHUNK), 1)
    lower = li >= si
    upper = si >= li
    tril = lower.astype(f32)
    dt_bias = hp_ref[0, 0:1, :]
    a_row = -jnp.exp(hp_ref[0, 1:2, :]) * LOG2E
    shift = lax.rem(LANES - 2 * nh * g, LANES)
    head_of_col = lax.broadcasted_iota(jnp.int32, (LANES, GROUP_WIDTH), 1) // SSM_HEAD_DIM
    head_lane = lax.broadcasted_iota(jnp.int32, (LANES, GROUP_WIDTH), 0)
    spread_f = (head_lane == head_of_col).astype(bf16)
    spread_b = (head_lane == head_of_col + nh).astype(bf16)
    tril_bf = tril.astype(bf16)

    def split3(x):
        hi = x.astype(bf16)
        rest = x - hi.astype(f32)
        mid = rest.astype(bf16)
        return hi, mid, (rest - mid.astype(f32)).astype(bf16)

    def dot3(lhs, rhs):
        return sum(jnp.dot(a, b, preferred_element_type=f32) for a in lhs for b in rhs)

    half_rows = CHUNK // 2

    def prepare(i, carry):
        r0 = pl.multiple_of(i * CHUNK, CHUNK)
        r = pl.ds(r0, CHUNK)
        lo = pl.multiple_of(jnp.maximum(r0 - HALO, 0), HALO)
        hi = pl.multiple_of(jnp.minimum(r0 + CHUNK, seq - HALO), HALO)
        has_lo = jnp.where(i > 0, 1.0, 0.0)
        has_hi = jnp.where(i < nc - 1, 1.0, 0.0)
        slab = 0
        for src, w_ref, b_ref, dst in ((x_ref, cwx_ref, cbx_ref, xs_ref),
                                       (bm_ref, cwb_ref, cbb_ref, bs_ref),
                                       (cm_ref, cwc_ref, cbc_ref, cs_ref)):
            for blk in range(src.shape[-1] // LANES):
                cols = slice(blk * LANES, (blk + 1) * LANES)
                st_ref[slab, 0:HALO, :] = src[0, pl.ds(lo, HALO), cols].astype(f32) * has_lo
                st_ref[slab, HALO:HALO + CHUNK, :] = src[0, r, cols].astype(f32)
                st_ref[slab, HALO + CHUNK:2 * HALO + CHUNK, :] = src[0, pl.ds(hi, HALO), cols].astype(f32) * has_hi
                planes = [st_ref[slab, pl.ds(HALO - half + s, half_rows, stride=2), :]
                          for s in range(CONV_WIDTH + 1)]
                for parity in range(2):
                    acc = b_ref[:, cols]
                    for tap in range(CONV_WIDTH):
                        acc = acc + w_ref[tap:tap + 1, cols] * planes[tap + parity]
                    ost_ref[slab, pl.ds(parity, half_rows, stride=2), :] = _silu(acc)
                out = ost_ref[slab].astype(bf16)
                dst[r, cols] = out
                if dst is bs_ref:
                    bt_ref[r, :] = out.astype(f32).T.astype(bf16)
                slab += 1

        raw = pltpu.roll(dt_ref[0, r, :], shift, 1)
        dtv = _softplus(raw + dt_bias)
        adt = dtv * a_row
        cum = dot3((tril_bf,), split3(adt))
        total = cum[CHUNK - 1:CHUNK, :]
        acs = jnp.where(lane < nh, cum, total - cum + adt)
        acs_ref[r, :] = acs
        rows_ref[i, 0] = (acs - jnp.log2(dtv)).T[0:2 * nh, :]
        rows_ref[i, 1] = (jnp.exp2(total - acs) * dtv).T[0:2 * nh, :]
        decay = split3(jnp.broadcast_to(jnp.exp2(total), (8, LANES)))
        decx_ref[i, 0] = dot3(decay, (spread_f,))
        decx_ref[i, 1] = dot3(decay, (spread_b,))
        return carry

    lax.fori_loop(0, nc, prepare, 0, unroll=2)

    hf_ref[...] = jnp.zeros_like(hf_ref)
    hb_ref[...] = jnp.zeros_like(hb_ref)

    def column(acs, j):
        return jnp.broadcast_to(acs[:, j:j + 1], (CHUNK, CHUNK))

    def forward(c, carry):
        r = pl.ds(pl.multiple_of(c * CHUNK, CHUNK), CHUNK)
        acs = acs_ref[r, :]
        cmat = cs_ref[r, :]
        btf = bt_ref[r, :].astype(f32)
        cb = lax.dot_general(cmat, bs_ref[r, :], (((1,), (1,)), ((), ())), preferred_element_type=f32)
        y_off = jnp.dot(cmat, hf_ref[...].astype(bf16), preferred_element_type=f32)
        for k in range(nh // 2):
            cols = slice(k * LANES, (k + 1) * LANES)
            xp = xs_ref[r, cols]
            halves = (xp * mask_lo, xp * mask_hi)
            y_diag = snew = None
            e_f = []
            for side in range(2):
                j = 2 * k + side
                a_f, a_b = column(acs, j), column(acs, nh + j)
                w = (jnp.exp2(jnp.where(lower, a_f - rows_ref[c, 0, j:j + 1, :], NEG))
                     + jnp.exp2(jnp.where(upper, a_b - rows_ref[c, 0, nh + j:nh + j + 1, :], NEG)))
                d = jnp.dot((cb * w).astype(bf16), halves[side], preferred_element_type=f32)
                inject = (btf * rows_ref[c, 1, j:j + 1, :]).astype(bf16)
                s = jnp.dot(inject, halves[side], preferred_element_type=f32)
                y_diag = d if y_diag is None else y_diag + d
                snew = s if snew is None else snew + s
                e_f.append(jnp.exp2(a_f))
            y_ref[r, cols] = (y_diag + y_off[:, cols] * jnp.where(left, e_f[0], e_f[1])
                              + dskip_ref[:, cols] * xp.astype(f32))
            hf_ref[:, cols] = hf_ref[:, cols] * decx_ref[c, 0, 0:1, cols] + snew
        return carry

    lax.fori_loop(0, nc, forward, 0, unroll=2)

    def backward(i, carry):
        c = nc - 1 - i
        r = pl.ds(pl.multiple_of(c * CHUNK, CHUNK), CHUNK)
        acs = acs_ref[r, :]
        btf = bt_ref[r, :].astype(f32)
        y_off = jnp.dot(cs_ref[r, :], hb_ref[...].astype(bf16), preferred_element_type=f32)
        ys = []
        for k in range(nh // 2):
            cols = slice(k * LANES, (k + 1) * LANES)
            xp = xs_ref[r, cols]
            halves = (xp * mask_lo, xp * mask_hi)
            snew = None
            e_b = []
            for side in range(2):
                jb = nh + 2 * k + side
                inject = (btf * rows_ref[c, 1, jb:jb + 1, :]).astype(bf16)
                s = jnp.dot(inject, halves[side], preferred_element_type=f32)
                snew = s if snew is None else snew + s
                e_b.append(jnp.exp2(column(acs, jb)))
            ys.append(y_ref[r, cols] + y_off[:, cols] * jnp.where(left, e_b[0], e_b[1]))
            hb_ref[:, cols] = hb_ref[:, cols] * decx_ref[c, 1, 0:1, cols] + snew
        y = jnp.concatenate(ys, axis=1) * _silu(z_ref[0, r, :].astype(f32))
        y = y * lax.rsqrt(jnp.mean(y * y, axis=-1, keepdims=True) + EPS)
        out_ref[0, r, :] = (y * ng_ref[...]).astype(out_ref.dtype)
        return carry

    lax.fori_loop(0, nc, backward, 0, unroll=2)


def _ssd(proj3, dt3, conv_w, conv_b, head_params, dskip_cols, norm_g):
    b, seq, _ = proj3.shape
    gw, ns = GROUP_WIDTH, SSM_STATE
    wide = lambda off: pl.BlockSpec((1, seq, gw), lambda i, g: (i, 0, off // gw + g))
    narrow = lambda off: pl.BlockSpec((1, seq, ns), lambda i, g: (i, 0, off // ns + g))
    xoff, boff, coff = 0, SSM_WIDTH, SSM_WIDTH + SSM_GROUPS * ns
    cw = lambda off, w: pl.BlockSpec((CONV_WIDTH, w), lambda i, g: (0, off // w + g))
    cbias = lambda off, w: pl.BlockSpec((1, w), lambda i, g: (0, off // w + g))
    return pl.pallas_call(
        functools.partial(_ssd_kernel, seq=seq),
        grid=(b, SSM_GROUPS),
        in_specs=[wide(COL_ZS), wide(COL_X), narrow(COL_B), narrow(COL_C),
                  pl.BlockSpec((1, seq, LANES), lambda i, g: (i, 0, 0)),
                  cw(xoff, gw), cbias(xoff, gw), cw(boff, ns), cbias(boff, ns), cw(coff, ns), cbias(coff, ns),
                  pl.BlockSpec((1, 8, LANES), lambda i, g: (g, 0, 0)),
                  pl.BlockSpec((1, gw), lambda i, g: (0, g)),
                  pl.BlockSpec((1, gw), lambda i, g: (0, g))],
        out_specs=pl.BlockSpec((1, seq, gw), lambda i, g: (i, 0, g)),
        out_shape=jax.ShapeDtypeStruct((b, seq, SSM_WIDTH), bf16),
        scratch_shapes=[pltpu.VMEM((seq, gw), bf16),
                        pltpu.VMEM((seq, ns), bf16),
                        pltpu.VMEM((seq, ns), bf16),
                        pltpu.VMEM((seq, ns), bf16),
                        pltpu.VMEM((seq, gw), f32),
                        pltpu.VMEM((seq, LANES), f32),
                        pltpu.VMEM((seq // CHUNK, 2, 2 * HEADS_PER_GROUP, CHUNK), f32),
                        pltpu.VMEM((seq // CHUNK, 2, 8, gw), f32),
                        pltpu.VMEM((ns, gw), f32),
                        pltpu.VMEM((ns, gw), f32),
                        pltpu.VMEM(((gw + 2 * ns) // LANES, CHUNK + 2 * HALO, LANES), f32),
                        pltpu.VMEM(((gw + 2 * ns) // LANES, CHUNK, LANES), f32)],
        compiler_params=_params("arbitrary", "arbitrary"),
        name="ssd",
    )(proj3, proj3, proj3, proj3, dt3, conv_w, conv_b, conv_w, conv_b, conv_w, conv_b,
      head_params, dskip_cols, norm_g)


OUTPROJ_TM = 256


def _outproj_kernel(o_ref, z_ref, s_ref, x_ref, w_ref, ga_ref, gp_ref, y_ref):
    o = o_ref[...].astype(f32)
    a = o * lax.rsqrt(jnp.mean(o * o, axis=-1, keepdims=True) + EPS) * ga_ref[...]
    a = a * _silu(z_ref[...].astype(f32))
    mix = jnp.dot(a.astype(bf16), w_ref[0:ATTN_WIDTH, :], preferred_element_type=f32)
    mix = mix + jnp.dot(s_ref[...], w_ref[ATTN_WIDTH:MIX_WIDTH, :], preferred_element_type=f32)
    mix = mix * lax.rsqrt(jnp.mean(mix * mix, axis=-1, keepdims=True) + EPS) * gp_ref[...]
    y_ref[...] = x_ref[...] + mix


def _outproj(o2, proj2, s2, x2, w_out, g_attn, g_post):
    m = x2.shape[0]
    tm = OUTPROJ_TM
    row = lambda width, blk: pl.BlockSpec((tm, width), lambda i: (i, blk))
    const = lambda shape: pl.BlockSpec(shape, lambda i: (0, 0))
    return pl.pallas_call(
        _outproj_kernel,
        grid=(m // tm,),
        in_specs=[row(ATTN_WIDTH, 0), row(ATTN_WIDTH, COL_ZA // ATTN_WIDTH), row(SSM_WIDTH, 0), row(D_MODEL, 0),
                  pl.BlockSpec((MIX_WIDTH, D_MODEL), lambda i: (0, 0), pipeline_mode=pl.Buffered(1)),
                  const((1, ATTN_WIDTH)), const((1, D_MODEL))],
        out_specs=row(D_MODEL, 0),
        out_shape=jax.ShapeDtypeStruct((m, D_MODEL), f32),
        compiler_params=_params("arbitrary"),
        name="outproj",
    )(o2, proj2, s2, x2, w_out, g_attn, g_post)


def _group_head_rows(fwd, bwd):
    rows = jnp.concatenate([fwd.reshape(SSM_GROUPS, HEADS_PER_GROUP), bwd.reshape(SSM_GROUPS, HEADS_PER_GROUP)],
                           axis=1)
    return jnp.pad(rows, ((0, 0), (0, LANES - 2 * HEADS_PER_GROUP)))


def _layer(x, bias, pre_g, w_main, w_dt, attn_g, conv_w, conv_b, head_params, dskip_cols, ssm_g, w_out, post_g):
    b, seq, _ = x.shape
    x2 = x.reshape(b * seq, D_MODEL)
    proj2, dt2 = _inproj(x2, pre_g, w_main, w_dt)
    proj3 = proj2.reshape(b, seq, PROJ_MAIN)
    o = _attention(proj3, bias)
    s = _ssd(proj3, dt2.reshape(b, seq, LANES), conv_w, conv_b, head_params, dskip_cols, ssm_g)
    y = _outproj(o.reshape(b * seq, ATTN_WIDTH), proj2, s.reshape(b * seq, SSM_WIDTH), x2, w_out, attn_g, post_g)
    return y.reshape(b, seq, D_MODEL)


def kernel(x_prompt, x_sample, pre_norm_g, w_in, rel_bias_table, attn_norm_g, conv_w, conv_b, dt_bias_fwd,
           dt_bias_bwd, a_log_fwd, a_log_bwd, d_skip, ssm_norm_g, w_out, post_norm_g):
    depth = w_in.shape[0]
    bias = _relbias(rel_bias_table)
    y_prompt, y_sample = x_prompt, x_sample
    for i in range(depth):
        w_main = w_in[i, :, :PROJ_MAIN].astype(bf16)
        w_dt = w_in[i, :, PROJ_MAIN:]
        w_dt = jnp.concatenate([w_dt[:, :SSM_HEADS].reshape(D_MODEL, SSM_GROUPS, HEADS_PER_GROUP),
                                w_dt[:, SSM_HEADS:].reshape(D_MODEL, SSM_GROUPS, HEADS_PER_GROUP)], axis=2)
        w_dt = jnp.pad(w_dt.reshape(D_MODEL, 2 * SSM_HEADS), ((0, 0), (0, LANES - 2 * SSM_HEADS))).astype(bf16)
        head_params = jnp.stack([_group_head_rows(dt_bias_fwd[i], dt_bias_bwd[i]),
                                 _group_head_rows(a_log_fwd[i], a_log_bwd[i])], axis=1)
        head_params = jnp.pad(head_params, ((0, 0), (0, 6), (0, 0)))
        args = (bias, pre_norm_g[i].reshape(1, D_MODEL), w_main, w_dt, attn_norm_g[i].reshape(1, ATTN_WIDTH),
                conv_w[i], conv_b[i].reshape(1, CONV_CH), head_params,
                jnp.repeat(d_skip[i], SSM_HEAD_DIM).reshape(1, SSM_WIDTH), ssm_norm_g[i].reshape(1, SSM_WIDTH),
                w_out[i].astype(bf16), post_norm_g[i].reshape(1, D_MODEL))
        y_prompt = _layer(y_prompt, *args)
        y_sample = _layer(y_sample, *args)
    return (y_prompt, y_sample)
```

```python
import functools
import math

import numpy as np
import jax
import jax.numpy as jnp
from jax import lax
from jax.experimental import pallas as pl
from jax.experimental.pallas import tpu as pltpu

D_MODEL = 2048
ATTN_HEADS = 16
HEAD_DIM = 128
ATTN_WIDTH = ATTN_HEADS * HEAD_DIM
DILATIONS = (1, 4, 16)
RADIUS = 64
N_BUCKETS = 32
BUCKET_MAX_DIST = 1024
SSM_HEADS = 32
SSM_HEAD_DIM = 64
SSM_WIDTH = SSM_HEADS * SSM_HEAD_DIM
SSM_GROUPS = 4
HEADS_PER_GROUP = SSM_HEADS // SSM_GROUPS
GROUP_WIDTH = SSM_WIDTH // SSM_GROUPS
SSM_STATE = 128
CONV_WIDTH = 5
CHUNK = 128
CONV_CH = SSM_WIDTH + 2 * SSM_GROUPS * SSM_STATE
MIX_WIDTH = ATTN_WIDTH + SSM_WIDTH
PROJ_MAIN = 4 * ATTN_WIDTH + SSM_WIDTH + CONV_CH
EPS = 1e-6

LANES = 128
BF16_ROWS = 16
QBLK = 128
KBLK = QBLK + 2 * RADIUS
TILE_GROUP = 16
LOG2E = math.log2(math.e)
NEG = -1e30
VMEM_LIMIT = 56 * 1024 * 1024

QKV_WIDTH = 3 * ATTN_WIDTH
REST_WIDTH = PROJ_MAIN - QKV_WIDTH
COL_ZA = 0
COL_ZS = ATTN_WIDTH
COL_X = COL_ZS + SSM_WIDTH
COL_B = COL_X + SSM_WIDTH
COL_C = COL_B + SSM_GROUPS * SSM_STATE

f32 = jnp.float32
bf16 = jnp.bfloat16


def _params(*sem):
    return pltpu.CompilerParams(dimension_semantics=sem, vmem_limit_bytes=VMEM_LIMIT)


def _bucket_tiles():
    qi = np.arange(QBLK)[:, None]
    kj = np.arange(KBLK)[None, :]
    rel = kj - RADIUS - qi
    nb = N_BUCKETS // 2
    max_exact = nb // 2
    tiles = []
    for dil in DILATIONS:
        r = rel * dil
        n = np.abs(r)
        nf = np.maximum(n, 1).astype(np.float32)
        large = max_exact + (np.log(nf / np.float32(max_exact)) / np.float32(math.log(BUCKET_MAX_DIST / max_exact))
                             * np.float32(nb - max_exact)).astype(np.int32)
        large = np.minimum(large, nb - 1)
        bucket = np.where(r > 0, nb, 0) + np.where(n < max_exact, n, large)
        tiles.append(np.where(np.abs(rel) <= RADIUS, bucket, -1))
    return np.stack(tiles).astype(np.int32)


def _relbias_kernel(tab_ref, idx_ref, out_ref):
    h = pl.program_id(0)
    for p in range(len(DILATIONS)):
        idx = idx_ref[p]
        acc = jnp.full((QBLK, KBLK), NEG, f32)
        for b in range(N_BUCKETS):
            acc = jnp.where(idx == b, tab_ref[b, h] * LOG2E, acc)
        out_ref[p, 0] = acc


def _relbias(rel_table):
    idx = jnp.asarray(_bucket_tiles())
    np_ = len(DILATIONS)
    return pl.pallas_call(
        _relbias_kernel,
        grid=(ATTN_HEADS,),
        in_specs=[pl.BlockSpec(memory_space=pltpu.SMEM),
                  pl.BlockSpec((np_, QBLK, KBLK), lambda h: (0, 0, 0))],
        out_specs=pl.BlockSpec((np_, 1, QBLK, KBLK), lambda h: (0, h, 0, 0)),
        out_shape=jax.ShapeDtypeStruct((np_, ATTN_HEADS, QBLK, KBLK), f32),
        compiler_params=_params("arbitrary"),
        name="relbias",
    )(rel_table, idx)


INPROJ_TM = 1024
INPROJ_TN = 1024
NORM_ROWS = 128


QKV_TILES = QKV_WIDTH // INPROJ_TN
HEADS_PER_TILE = INPROJ_TN // HEAD_DIM


def _inproj_kernel(x_ref, g_ref, w_ref, wdt_ref, qkv_ref, rest_ref, dt_ref, h_ref):
    j = pl.program_id(1)

    @pl.when(j == 0)
    def _():
        g = g_ref[...]

        def rows(i, carry):
            r = pl.ds(pl.multiple_of(i * NORM_ROWS, NORM_ROWS), NORM_ROWS)
            x = x_ref[r, :]
            y = x * lax.rsqrt(jnp.mean(x * x, axis=-1, keepdims=True) + EPS)
            h_ref[r, :] = (y * g).astype(bf16)
            return carry

        lax.fori_loop(0, INPROJ_TM // NORM_ROWS, rows, 0)
        dt_ref[...] = jnp.dot(h_ref[...], wdt_ref[...], preferred_element_type=f32)

    @pl.when(j < QKV_TILES)
    def _():
        tile = jnp.dot(h_ref[...], w_ref[...], preferred_element_type=f32).astype(bf16)
        for hh in range(HEADS_PER_TILE):
            qkv_ref[hh] = tile[:, hh * HEAD_DIM:(hh + 1) * HEAD_DIM]

    @pl.when(j >= QKV_TILES)
    def _():
        rest_ref[...] = jnp.dot(h_ref[...], w_ref[...], preferred_element_type=f32).astype(bf16)


def _inproj(x2, g, w_main, w_dt):
    m = x2.shape[0]
    return pl.pallas_call(
        _inproj_kernel,
        grid=(m // INPROJ_TM, PROJ_MAIN // INPROJ_TN),
        in_specs=[pl.BlockSpec((INPROJ_TM, D_MODEL), lambda i, j: (i, 0)),
                  pl.BlockSpec((1, D_MODEL), lambda i, j: (0, 0)),
                  pl.BlockSpec((D_MODEL, INPROJ_TN), lambda i, j: (0, j)),
                  pl.BlockSpec((D_MODEL, LANES), lambda i, j: (0, 0))],
        out_specs=[pl.BlockSpec((HEADS_PER_TILE, INPROJ_TM, HEAD_DIM),
                                lambda i, j: (jnp.minimum(j, QKV_TILES - 1), i, 0)),
                   pl.BlockSpec((INPROJ_TM, INPROJ_TN), lambda i, j: (i, jnp.maximum(j - QKV_TILES, 0))),
                   pl.BlockSpec((INPROJ_TM, LANES), lambda i, j: (i, 0))],
        out_shape=[jax.ShapeDtypeStruct((3 * ATTN_HEADS, m, HEAD_DIM), bf16),
                   jax.ShapeDtypeStruct((m, REST_WIDTH), bf16),
                   jax.ShapeDtypeStruct((m, LANES), f32)],
        scratch_shapes=[pltpu.VMEM((INPROJ_TM, D_MODEL), bf16)],
        compiler_params=_params("arbitrary", "arbitrary"),
        name="inproj",
    )(x2, g, w_main, w_dt)


SCORE_SCALE = LOG2E / math.sqrt(HEAD_DIM)
CLASS_STEP = 4


def _attn_kernel(q_ref, k_ref, v_ref, bias_ref, o_ref,
                 tok_ref, cls4_ref, qd_ref, kd_ref, vd_ref, bias4_ref,
                 acc4_ref, m4_ref, l4_ref, acc1_ref, m1_ref, l1_ref,
                 s_buf, p_buf, mn_buf, rs_buf, *, seq):
    nblk = seq // QBLK
    n4 = seq // 4
    n16 = seq // 16
    srcs = (q_ref, k_ref, v_ref)
    dsts = ((qd_ref, 0), (kd_ref, RADIUS), (vd_ref, RADIUS))

    @pl.when(pl.program_id(1) == 0)
    def _():
        col = lax.broadcasted_iota(jnp.int32, (QBLK, KBLK), 1)
        before = col < RADIUS
        after = col >= QBLK + RADIUS
        for p in range(len(DILATIONS)):
            b = bias_ref[p, 0]
            bias4_ref[p, 0] = b
            bias4_ref[p, 1] = jnp.where(before, NEG, b)
            bias4_ref[p, 2] = jnp.where(after, NEG, b)
            bias4_ref[p, 3] = jnp.where(before | after, NEG, b)
        pad = jnp.zeros((RADIUS, HEAD_DIM), bf16)
        for ref in (kd_ref, vd_ref):
            ref[0:RADIUS, :] = pad
            ref[RADIUS + seq:2 * RADIUS + seq, :] = pad

    def stage(i, carry):
        r = pl.ds(pl.multiple_of(i * QBLK, QBLK), QBLK)
        for x, src in enumerate(srcs):
            tok_ref[x, r, :] = src[r, :].astype(f32)
        return carry

    lax.fori_loop(0, nblk, stage, 0)
    for x in range(3):
        for c4 in range(CLASS_STEP):
            cls4_ref[x, c4 * n4:(c4 + 1) * n4, :] = tok_ref[x, pl.ds(c4, n4, stride=CLASS_STEP), :]

    def tile_rows(t):
        return pl.ds(pl.multiple_of(t * QBLK, QBLK), QBLK)

    def scores(t, u, p, tiles_per_class):
        r0 = pl.multiple_of(t * QBLK, QBLK)
        q = qd_ref[pl.ds(r0, QBLK), :]
        k = kd_ref[pl.ds(r0, KBLK), :]
        jj = lax.rem(t, tiles_per_class)
        variant = (jj == 0).astype(jnp.int32) + 2 * (jj == tiles_per_class - 1).astype(jnp.int32)
        s = lax.dot_general(q, k, (((1,), (1,)), ((), ())), preferred_element_type=f32)
        s_buf[u] = s * SCORE_SCALE + bias4_ref[p, variant]

    def softmax(t, u, prev):
        s = s_buf[u]
        top = jnp.max(s, axis=-1, keepdims=True)
        if prev is None:
            mn = jnp.broadcast_to(top, (QBLK, HEAD_DIM))
        else:
            mn = jnp.maximum(prev[1][tile_rows(t), :], top)
        e_lo = jnp.exp2(s[:, :HEAD_DIM] - mn)
        e_hi = jnp.exp2(s[:, HEAD_DIM:] - mn)
        p_buf[u, :, :HEAD_DIM] = e_lo.astype(bf16)
        p_buf[u, :, HEAD_DIM:] = e_hi.astype(bf16)
        mn_buf[u] = mn
        rs_buf[u] = jnp.broadcast_to(jnp.sum(e_lo + e_hi, axis=-1, keepdims=True), (QBLK, HEAD_DIM))

    def update(t, u, prev, nxt, dst_rows):
        r = tile_rows(t)
        v = vd_ref[pl.ds(pl.multiple_of(t * QBLK, QBLK), KBLK), :]
        acc = jnp.dot(p_buf[u], v, preferred_element_type=f32)
        mn = mn_buf[u]
        l = rs_buf[u]
        if prev is not None:
            alpha = jnp.exp2(prev[1][r, :] - mn)
            l = alpha * prev[2][r, :] + l
            acc = alpha * prev[0][r, :] + acc
        if nxt is None:
            o_ref[r, :] = (acc / l).astype(o_ref.dtype)
        else:
            nxt[0][dst_rows, :] = acc
            nxt[1][dst_rows, :] = mn
            nxt[2][dst_rows, :] = l

    def run_pattern(p, prev, nxt, dst_rows_fn):
        tiles_per_class = seq // DILATIONS[p] // QBLK

        def group(i, carry):
            tiles = [i * TILE_GROUP + u for u in range(TILE_GROUP)]
            for u, t in enumerate(tiles):
                scores(t, u, p, tiles_per_class)
            for u, t in enumerate(tiles):
                softmax(t, u, prev)
            for u, t in enumerate(tiles):
                dst = None if nxt is None else dst_rows_fn(t // tiles_per_class, lax.rem(t, tiles_per_class))
                update(t, u, prev, nxt, dst)
            return carry

        lax.fori_loop(0, nblk // TILE_GROUP, group, 0)

    for c16 in range(16):
        rows = pl.ds((c16 % 4) * n4 + c16 // 4, n16, stride=CLASS_STEP)
        for x, (dst, off) in enumerate(dsts):
            dst[off + c16 * n16:off + (c16 + 1) * n16, :] = cls4_ref[x, rows, :].astype(bf16)
    state4 = (acc4_ref, m4_ref, l4_ref)
    state1 = (acc1_ref, m1_ref, l1_ref)
    run_pattern(2, None, state4,
                lambda c16, jj: pl.ds(lax.rem(c16, 4) * n4 + c16 // 4 + jj * (QBLK * CLASS_STEP), QBLK,
                                      stride=CLASS_STEP))

    def copy_cls4(i, carry):
        r0 = pl.multiple_of(i * QBLK, QBLK)
        for x, (dst, off) in enumerate(dsts):
            dst[pl.ds(r0 + off, QBLK), :] = cls4_ref[x, pl.ds(r0, QBLK), :].astype(bf16)
        return carry

    lax.fori_loop(0, nblk, copy_cls4, 0)
    run_pattern(1, state4, state1,
                lambda c4, jj: pl.ds(c4 + jj * (QBLK * CLASS_STEP), QBLK, stride=CLASS_STEP))

    def copy_tok(i, carry):
        r0 = pl.multiple_of(i * QBLK, QBLK)
        for src, (dst, off) in zip(srcs, dsts):
            dst[pl.ds(r0 + off, QBLK), :] = src[pl.ds(r0, QBLK), :]
        return carry

    lax.fori_loop(0, nblk, copy_tok, 0)
    run_pattern(0, state1, None, None)


def _attention(qkv4, bias):
    _, b, seq, _ = qkv4.shape
    np_ = len(DILATIONS)
    head = lambda first: pl.BlockSpec((None, None, seq, HEAD_DIM), lambda h, i: (first + h, i, 0, 0))
    return pl.pallas_call(
        functools.partial(_attn_kernel, seq=seq),
        grid=(ATTN_HEADS, b),
        in_specs=[head(0), head(ATTN_HEADS), head(2 * ATTN_HEADS),
                  pl.BlockSpec((np_, 1, QBLK, KBLK), lambda h, i: (0, h, 0, 0))],
        out_specs=head(0),
        out_shape=jax.ShapeDtypeStruct((ATTN_HEADS, b, seq, HEAD_DIM), bf16),
        scratch_shapes=[pltpu.VMEM((3, seq, HEAD_DIM), f32)] * 2
                       + [pltpu.VMEM((seq, HEAD_DIM), bf16)]
                       + [pltpu.VMEM((seq + 2 * RADIUS, HEAD_DIM), bf16)] * 2
                       + [pltpu.VMEM((np_, 4, QBLK, KBLK), f32)]
                       + [pltpu.VMEM((seq, HEAD_DIM), f32)] * 6
                       + [pltpu.VMEM((TILE_GROUP, QBLK, KBLK), f32),
                          pltpu.VMEM((TILE_GROUP, QBLK, KBLK), bf16),
                          pltpu.VMEM((TILE_GROUP, QBLK, HEAD_DIM), f32),
                          pltpu.VMEM((TILE_GROUP, QBLK, HEAD_DIM), f32)],
        compiler_params=_params("arbitrary", "arbitrary"),
        name="attn",
    )(qkv4, qkv4, qkv4, bias)


HALO = BF16_ROWS


def _silu(x):
    h = 0.5 * x
    return h + h * jnp.tanh(h)


def _softplus(x):
    return jnp.maximum(x, 0.0) + jnp.log1p(jnp.exp(-jnp.abs(x)))


def _ssd_kernel(z_ref, x_ref, bm_ref, cm_ref, dt_ref, cwx_ref, cbx_ref, cwb_ref, cbb_ref, cwc_ref, cbc_ref,
                hp_ref, dskip_ref, ng_ref, out_ref,
                xs_ref, bs_ref, cs_ref, bt_ref, y_ref, acs_ref, rows_ref, decx_ref, hf_ref, hb_ref,
                st_ref, ost_ref, *, seq):
    nc = seq // CHUNK
    g = pl.program_id(1)
    half = CONV_WIDTH // 2
    nh = HEADS_PER_GROUP

    lane = lax.broadcasted_iota(jnp.int32, (CHUNK, LANES), 1)
    left = lane < SSM_HEAD_DIM
    mask_lo = jnp.where(left, 1.0, 0.0).astype(bf16)
    mask_hi = jnp.where(left, 0.0, 1.0).astype(bf16)
    li = lax.broadcasted_iota(jnp.int32, (CHUNK, CHUNK), 0)
    si = lax.broadcasted_iota(jnp.int32, (CHUNK, CHUNK), 1)
    lower = li >= si
    upper = si >= li
    tril = lower.astype(f32)
    dt_bias = hp_ref[0, 0:1, :]
    a_row = -jnp.exp(hp_ref[0, 1:2, :]) * LOG2E
    shift = lax.rem(LANES - 2 * nh * g, LANES)
    head_of_col = lax.broadcasted_iota(jnp.int32, (LANES, GROUP_WIDTH), 1) // SSM_HEAD_DIM
    head_lane = lax.broadcasted_iota(jnp.int32, (LANES, GROUP_WIDTH), 0)
    spread_f = (head_lane == head_of_col).astype(bf16)
    spread_b = (head_lane == head_of_col + nh).astype(bf16)
    tril_bf = tril.astype(bf16)

    def split3(x):
        hi = x.astype(bf16)
        rest = x - hi.astype(f32)
        mid = rest.astype(bf16)
        return hi, mid, (rest - mid.astype(f32)).astype(bf16)

    def dot3(lhs, rhs):
        return sum(jnp.dot(a, b, preferred_element_type=f32) for a in lhs for b in rhs)

    half_rows = CHUNK // 2

    def prepare(i, carry):
        r0 = pl.multiple_of(i * CHUNK, CHUNK)
        r = pl.ds(r0, CHUNK)
        lo = pl.multiple_of(jnp.maximum(r0 - HALO, 0), HALO)
        hi = pl.multiple_of(jnp.minimum(r0 + CHUNK, seq - HALO), HALO)
        has_lo = jnp.where(i > 0, 1.0, 0.0)
        has_hi = jnp.where(i < nc - 1, 1.0, 0.0)
        slab = 0
        for src, w_ref, b_ref, dst in ((x_ref, cwx_ref, cbx_ref, xs_ref),
                                       (bm_ref, cwb_ref, cbb_ref, bs_ref),
                                       (cm_ref, cwc_ref, cbc_ref, cs_ref)):
            for blk in range(src.shape[-1] // LANES):
                cols = slice(blk * LANES, (blk + 1) * LANES)
                st_ref[slab, 0:HALO, :] = src[0, pl.ds(lo, HALO), cols].astype(f32) * has_lo
                st_ref[slab, HALO:HALO + CHUNK, :] = src[0, r, cols].astype(f32)
                st_ref[slab, HALO + CHUNK:2 * HALO + CHUNK, :] = src[0, pl.ds(hi, HALO), cols].astype(f32) * has_hi
                planes = [st_ref[slab, pl.ds(HALO - half + s, half_rows, stride=2), :]
                          for s in range(CONV_WIDTH + 1)]
                for parity in range(2):
                    acc = b_ref[:, cols]
                    for tap in range(CONV_WIDTH):
                        acc = acc + w_ref[tap:tap + 1, cols] * planes[tap + parity]
                    ost_ref[slab, pl.ds(parity, half_rows, stride=2), :] = _silu(acc)
                out = ost_ref[slab].astype(bf16)
                dst[r, cols] = out
                if dst is bs_ref:
                    bt_ref[r, :] = out.astype(f32).T.astype(bf16)
                slab += 1

        raw = pltpu.roll(dt_ref[0, r, :], shift, 1)
        dtv = _softplus(raw + dt_bias)
        adt = dtv * a_row
        cum = dot3((tril_bf,), split3(adt))
        total = cum[CHUNK - 1:CHUNK, :]
        acs = jnp.where(lane < nh, cum, total - cum + adt)
        acs_ref[r, :] = acs
        rows_ref[i, 0] = (acs - jnp.log2(dtv)).T[0:2 * nh, :]
        rows_ref[i, 1] = (jnp.exp2(total - acs) * dtv).T[0:2 * nh, :]
        decay = split3(jnp.broadcast_to(jnp.exp2(total), (8, LANES)))
        decx_ref[i, 0] = dot3(decay, (spread_f,))
        decx_ref[i, 1] = dot3(decay, (spread_b,))
        return carry

    lax.fori_loop(0, nc, prepare, 0, unroll=2)

    hf_ref[...] = jnp.zeros_like(hf_ref)
    hb_ref[...] = jnp.zeros_like(hb_ref)

    def column(acs, j):
        return jnp.broadcast_to(acs[:, j:j + 1], (CHUNK, CHUNK))

    def forward(c, carry):
        r = pl.ds(pl.multiple_of(c * CHUNK, CHUNK), CHUNK)
        acs = acs_ref[r, :]
        cmat = cs_ref[r, :]
        btf = bt_ref[r, :].astype(f32)
        cb = lax.dot_general(cmat, bs_ref[r, :], (((1,), (1,)), ((), ())), preferred_element_type=f32)
        y_off = jnp.dot(cmat, hf_ref[...].astype(bf16), preferred_element_type=f32)
        for k in range(nh // 2):
            cols = slice(k * LANES, (k + 1) * LANES)
            xp = xs_ref[r, cols]
            halves = (xp * mask_lo, xp * mask_hi)
            y_diag = snew = None
            e_f = []
            for side in range(2):
                j = 2 * k + side
                a_f, a_b = column(acs, j), column(acs, nh + j)
                w = (jnp.exp2(jnp.where(lower, a_f - rows_ref[c, 0, j:j + 1, :], NEG))
                     + jnp.exp2(jnp.where(upper, a_b - rows_ref[c, 0, nh + j:nh + j + 1, :], NEG)))
                d = jnp.dot((cb * w).astype(bf16), halves[side], preferred_element_type=f32)
                inject = (btf * rows_ref[c, 1, j:j + 1, :]).astype(bf16)
                s = jnp.dot(inject, halves[side], preferred_element_type=f32)
                y_diag = d if y_diag is None else y_diag + d
                snew = s if snew is None else snew + s
                e_f.append(jnp.exp2(a_f))
            y_ref[r, cols] = (y_diag + y_off[:, cols] * jnp.where(left, e_f[0], e_f[1])
                              + dskip_ref[:, cols] * xp.astype(f32))
            hf_ref[:, cols] = hf_ref[:, cols] * decx_ref[c, 0, 0:1, cols] + snew
        return carry

    lax.fori_loop(0, nc, forward, 0, unroll=2)

    def backward(i, carry):
        c = nc - 1 - i
        r = pl.ds(pl.multiple_of(c * CHUNK, CHUNK), CHUNK)
        acs = acs_ref[r, :]
        btf = bt_ref[r, :].astype(f32)
        y_off = jnp.dot(cs_ref[r, :], hb_ref[...].astype(bf16), preferred_element_type=f32)
        ys = []
        for k in range(nh // 2):
            cols = slice(k * LANES, (k + 1) * LANES)
            xp = xs_ref[r, cols]
            halves = (xp * mask_lo, xp * mask_hi)
            snew = None
            e_b = []
            for side in range(2):
                jb = nh + 2 * k + side
                inject = (btf * rows_ref[c, 1, jb:jb + 1, :]).astype(bf16)
                s = jnp.dot(inject, halves[side], preferred_element_type=f32)
                snew = s if snew is None else snew + s
                e_b.append(jnp.exp2(column(acs, jb)))
            ys.append(y_ref[r, cols] + y_off[:, cols] * jnp.where(left, e_b[0], e_b[1]))
            hb_ref[:, cols] = hb_ref[:, cols] * decx_ref[c, 1, 0:1, cols] + snew
        y = jnp.concatenate(ys, axis=1) * _silu(z_ref[0, r, :].astype(f32))
        y = y * lax.rsqrt(jnp.mean(y * y, axis=-1, keepdims=True) + EPS)
        out_ref[0, r, :] = (y * ng_ref[...]).astype(out_ref.dtype)
        return carry

    lax.fori_loop(0, nc, backward, 0, unroll=2)


def _ssd(proj3, dt3, conv_w, conv_b, head_params, dskip_cols, norm_g):
    b, seq, _ = proj3.shape
    gw, ns = GROUP_WIDTH, SSM_STATE
    wide = lambda off: pl.BlockSpec((1, seq, gw), lambda i, g: (i, 0, off // gw + g))
    narrow = lambda off: pl.BlockSpec((1, seq, ns), lambda i, g: (i, 0, off // ns + g))
    xoff, boff, coff = 0, SSM_WIDTH, SSM_WIDTH + SSM_GROUPS * ns
    cw = lambda off, w: pl.BlockSpec((CONV_WIDTH, w), lambda i, g: (0, off // w + g))
    cbias = lambda off, w: pl.BlockSpec((1, w), lambda i, g: (0, off // w + g))
    return pl.pallas_call(
        functools.partial(_ssd_kernel, seq=seq),
        grid=(b, SSM_GROUPS),
        in_specs=[wide(COL_ZS), wide(COL_X), narrow(COL_B), narrow(COL_C),
                  pl.BlockSpec((1, seq, LANES), lambda i, g: (i, 0, 0)),
                  cw(xoff, gw), cbias(xoff, gw), cw(boff, ns), cbias(boff, ns), cw(coff, ns), cbias(coff, ns),
                  pl.BlockSpec((1, 8, LANES), lambda i, g: (g, 0, 0)),
                  pl.BlockSpec((1, gw), lambda i, g: (0, g)),
                  pl.BlockSpec((1, gw), lambda i, g: (0, g))],
        out_specs=pl.BlockSpec((1, seq, gw), lambda i, g: (i, 0, g)),
        out_shape=jax.ShapeDtypeStruct((b, seq, SSM_WIDTH), bf16),
        scratch_shapes=[pltpu.VMEM((seq, gw), bf16),
                        pltpu.VMEM((seq, ns), bf16),
                        pltpu.VMEM((seq, ns), bf16),
                        pltpu.VMEM((seq, ns), bf16),
                        pltpu.VMEM((seq, gw), f32),
                        pltpu.VMEM((seq, LANES), f32),
                        pltpu.VMEM((seq // CHUNK, 2, 2 * HEADS_PER_GROUP, CHUNK), f32),
                        pltpu.VMEM((seq // CHUNK, 2, 8, gw), f32),
                        pltpu.VMEM((ns, gw), f32),
                        pltpu.VMEM((ns, gw), f32),
                        pltpu.VMEM(((gw + 2 * ns) // LANES, CHUNK + 2 * HALO, LANES), f32),
                        pltpu.VMEM(((gw + 2 * ns) // LANES, CHUNK, LANES), f32)],
        compiler_params=_params("arbitrary", "arbitrary"),
        name="ssd",
    )(proj3, proj3, proj3, proj3, dt3, conv_w, conv_b, conv_w, conv_b, conv_w, conv_b,
      head_params, dskip_cols, norm_g)


OUTPROJ_TM = 512


def _outproj_kernel(o_ref, z_ref, s_ref, x_ref, w_ref, ga_ref, gp_ref, y_ref):
    o = jnp.concatenate([o_ref[h] for h in range(ATTN_HEADS)], axis=1).astype(f32)
    a = o * lax.rsqrt(jnp.mean(o * o, axis=-1, keepdims=True) + EPS) * ga_ref[...]
    a = a * _silu(z_ref[...].astype(f32))
    mix = jnp.dot(a.astype(bf16), w_ref[0:ATTN_WIDTH, :], preferred_element_type=f32)
    mix = mix + jnp.dot(s_ref[...], w_ref[ATTN_WIDTH:MIX_WIDTH, :], preferred_element_type=f32)
    mix = mix * lax.rsqrt(jnp.mean(mix * mix, axis=-1, keepdims=True) + EPS) * gp_ref[...]
    y_ref[...] = x_ref[...] + mix


def _outproj(o3, rest2, s2, x2, w_out, g_attn, g_post):
    m = x2.shape[0]
    tm = OUTPROJ_TM
    row = lambda width, blk: pl.BlockSpec((tm, width), lambda i: (i, blk))
    const = lambda shape: pl.BlockSpec(shape, lambda i: (0, 0))
    return pl.pallas_call(
        _outproj_kernel,
        grid=(m // tm,),
        in_specs=[pl.BlockSpec((ATTN_HEADS, tm, HEAD_DIM), lambda i: (0, i, 0)),
                  row(ATTN_WIDTH, COL_ZA // ATTN_WIDTH), row(SSM_WIDTH, 0), row(D_MODEL, 0),
                  pl.BlockSpec((MIX_WIDTH, D_MODEL), lambda i: (0, 0), pipeline_mode=pl.Buffered(1)),
                  const((1, ATTN_WIDTH)), const((1, D_MODEL))],
        out_specs=row(D_MODEL, 0),
        out_shape=jax.ShapeDtypeStruct((m, D_MODEL), f32),
        compiler_params=_params("arbitrary"),
        name="outproj",
    )(o3, rest2, s2, x2, w_out, g_attn, g_post)


def _group_head_rows(fwd, bwd):
    rows = jnp.concatenate([fwd.reshape(SSM_GROUPS, HEADS_PER_GROUP), bwd.reshape(SSM_GROUPS, HEADS_PER_GROUP)],
                           axis=1)
    return jnp.pad(rows, ((0, 0), (0, LANES - 2 * HEADS_PER_GROUP)))


def _layer(x, bias, pre_g, w_main, w_dt, attn_g, conv_w, conv_b, head_params, dskip_cols, ssm_g, w_out, post_g):
    b, seq, _ = x.shape
    x2 = x.reshape(b * seq, D_MODEL)
    qkv, rest2, dt2 = _inproj(x2, pre_g, w_main, w_dt)
    o = _attention(qkv.reshape(3 * ATTN_HEADS, b, seq, HEAD_DIM), bias)
    s = _ssd(rest2.reshape(b, seq, REST_WIDTH), dt2.reshape(b, seq, LANES), conv_w, conv_b, head_params,
             dskip_cols, ssm_g)
    y = _outproj(o.reshape(ATTN_HEADS, b * seq, HEAD_DIM), rest2, s.reshape(b * seq, SSM_WIDTH), x2, w_out,
                 attn_g, post_g)
    return y.reshape(b, seq, D_MODEL)


def kernel(x_prompt, x_sample, pre_norm_g, w_in, rel_bias_table, attn_norm_g, conv_w, conv_b, dt_bias_fwd,
           dt_bias_bwd, a_log_fwd, a_log_bwd, d_skip, ssm_norm_g, w_out, post_norm_g):
    depth = w_in.shape[0]
    bias = _relbias(rel_bias_table)
    y_prompt, y_sample = x_prompt, x_sample
    for i in range(depth):
        w_main = w_in[i, :, :PROJ_MAIN].astype(bf16)
        w_dt = w_in[i, :, PROJ_MAIN:]
        w_dt = jnp.concatenate([w_dt[:, :SSM_HEADS].reshape(D_MODEL, SSM_GROUPS, HEADS_PER_GROUP),
                                w_dt[:, SSM_HEADS:].reshape(D_MODEL, SSM_GROUPS, HEADS_PER_GROUP)], axis=2)
        w_dt = jnp.pad(w_dt.reshape(D_MODEL, 2 * SSM_HEADS), ((0, 0), (0, LANES - 2 * SSM_HEADS))).astype(bf16)
        head_params = jnp.stack([_group_head_rows(dt_bias_fwd[i], dt_bias_bwd[i]),
                                 _group_head_rows(a_log_fwd[i], a_log_bwd[i])], axis=1)
        head_params = jnp.pad(head_params, ((0, 0), (0, 6), (0, 0)))
        args = (bias, pre_norm_g[i].reshape(1, D_MODEL), w_main, w_dt, attn_norm_g[i].reshape(1, ATTN_WIDTH),
                conv_w[i], conv_b[i].reshape(1, CONV_CH), head_params,
                jnp.repeat(d_skip[i], SSM_HEAD_DIM).reshape(1, SSM_WIDTH), ssm_norm_g[i].reshape(1, SSM_WIDTH),
                w_out[i].astype(bf16), post_norm_g[i].reshape(1, D_MODEL))
        y_prompt = _layer(y_prompt, *args)
        y_sample = _layer(y_sample, *args)
    return (y_prompt, y_sample)
```

```python
import functools
import math

import numpy as np
import jax
import jax.numpy as jnp
from jax import lax
from jax.experimental import pallas as pl
from jax.experimental.pallas import tpu as pltpu

D_MODEL = 2048
ATTN_HEADS = 16
HEAD_DIM = 128
ATTN_WIDTH = ATTN_HEADS * HEAD_DIM
DILATIONS = (1, 4, 16)
RADIUS = 64
N_BUCKETS = 32
BUCKET_MAX_DIST = 1024
SSM_HEADS = 32
SSM_HEAD_DIM = 64
SSM_WIDTH = SSM_HEADS * SSM_HEAD_DIM
SSM_GROUPS = 4
HEADS_PER_GROUP = SSM_HEADS // SSM_GROUPS
GROUP_WIDTH = SSM_WIDTH // SSM_GROUPS
SSM_STATE = 128
CONV_WIDTH = 5
CHUNK = 128
CONV_CH = SSM_WIDTH + 2 * SSM_GROUPS * SSM_STATE
MIX_WIDTH = ATTN_WIDTH + SSM_WIDTH
PROJ_MAIN = 4 * ATTN_WIDTH + SSM_WIDTH + CONV_CH
EPS = 1e-6

LANES = 128
BF16_ROWS = 16
QBLK = 128
KBLK = QBLK + 2 * RADIUS
TILE_GROUP = 16
LOG2E = math.log2(math.e)
NEG = -1e30
VMEM_LIMIT = 56 * 1024 * 1024

QKV_WIDTH = 3 * ATTN_WIDTH
REST_WIDTH = PROJ_MAIN - QKV_WIDTH
COL_ZA = 0
COL_ZS = ATTN_WIDTH
COL_X = COL_ZS + SSM_WIDTH
COL_B = COL_X + SSM_WIDTH
COL_C = COL_B + SSM_GROUPS * SSM_STATE

f32 = jnp.float32
bf16 = jnp.bfloat16


def _params(*sem):
    return pltpu.CompilerParams(dimension_semantics=sem, vmem_limit_bytes=VMEM_LIMIT)


def _bucket_tiles():
    qi = np.arange(QBLK)[:, None]
    kj = np.arange(KBLK)[None, :]
    rel = kj - RADIUS - qi
    nb = N_BUCKETS // 2
    max_exact = nb // 2
    tiles = []
    for dil in DILATIONS:
        r = rel * dil
        n = np.abs(r)
        nf = np.maximum(n, 1).astype(np.float32)
        large = max_exact + (np.log(nf / np.float32(max_exact)) / np.float32(math.log(BUCKET_MAX_DIST / max_exact))
                             * np.float32(nb - max_exact)).astype(np.int32)
        large = np.minimum(large, nb - 1)
        bucket = np.where(r > 0, nb, 0) + np.where(n < max_exact, n, large)
        tiles.append(np.where(np.abs(rel) <= RADIUS, bucket, -1))
    return np.stack(tiles).astype(np.int32)


def _relbias_kernel(tab_ref, idx_ref, out_ref):
    h = pl.program_id(0)
    for p in range(len(DILATIONS)):
        idx = idx_ref[p]
        acc = jnp.full((QBLK, KBLK), NEG, f32)
        for b in range(N_BUCKETS):
            acc = jnp.where(idx == b, tab_ref[b, h] * LOG2E, acc)
        out_ref[p, 0] = acc


def _relbias(rel_table):
    idx = jnp.asarray(_bucket_tiles())
    np_ = len(DILATIONS)
    return pl.pallas_call(
        _relbias_kernel,
        grid=(ATTN_HEADS,),
        in_specs=[pl.BlockSpec(memory_space=pltpu.SMEM),
                  pl.BlockSpec((np_, QBLK, KBLK), lambda h: (0, 0, 0))],
        out_specs=pl.BlockSpec((np_, 1, QBLK, KBLK), lambda h: (0, h, 0, 0)),
        out_shape=jax.ShapeDtypeStruct((np_, ATTN_HEADS, QBLK, KBLK), f32),
        compiler_params=_params("arbitrary"),
        name="relbias",
    )(rel_table, idx)


INPROJ_TM = 1024
INPROJ_TN = 1024
NORM_ROWS = 128


QKV_TILES = QKV_WIDTH // INPROJ_TN
HEADS_PER_TILE = INPROJ_TN // HEAD_DIM


def _inproj_kernel(x_ref, g_ref, w_ref, wdt_ref, qkv_ref, rest_ref, dt_ref, h_ref):
    j = pl.program_id(1)

    @pl.when(j == 0)
    def _():
        g = g_ref[...]

        def rows(i, carry):
            r = pl.ds(pl.multiple_of(i * NORM_ROWS, NORM_ROWS), NORM_ROWS)
            x = x_ref[r, :]
            y = x * lax.rsqrt(jnp.mean(x * x, axis=-1, keepdims=True) + EPS)
            h_ref[r, :] = (y * g).astype(bf16)
            return carry

        lax.fori_loop(0, INPROJ_TM // NORM_ROWS, rows, 0)
        dt_ref[...] = jnp.dot(h_ref[...], wdt_ref[...], preferred_element_type=f32)

    @pl.when(j < QKV_TILES)
    def _():
        tile = jnp.dot(h_ref[...], w_ref[...], preferred_element_type=f32).astype(bf16)
        for hh in range(HEADS_PER_TILE):
            qkv_ref[hh] = tile[:, hh * HEAD_DIM:(hh + 1) * HEAD_DIM]

    @pl.when(j >= QKV_TILES)
    def _():
        rest_ref[...] = jnp.dot(h_ref[...], w_ref[...], preferred_element_type=f32).astype(bf16)


def _inproj(x2, g, w_main, w_dt):
    m = x2.shape[0]
    return pl.pallas_call(
        _inproj_kernel,
        grid=(m // INPROJ_TM, PROJ_MAIN // INPROJ_TN),
        in_specs=[pl.BlockSpec((INPROJ_TM, D_MODEL), lambda i, j: (i, 0)),
                  pl.BlockSpec((1, D_MODEL), lambda i, j: (0, 0)),
                  pl.BlockSpec((D_MODEL, INPROJ_TN), lambda i, j: (0, j)),
                  pl.BlockSpec((D_MODEL, LANES), lambda i, j: (0, 0))],
        out_specs=[pl.BlockSpec((HEADS_PER_TILE, INPROJ_TM, HEAD_DIM),
                                lambda i, j: (jnp.minimum(j, QKV_TILES - 1), i, 0)),
                   pl.BlockSpec((INPROJ_TM, INPROJ_TN), lambda i, j: (i, jnp.maximum(j - QKV_TILES, 0))),
                   pl.BlockSpec((INPROJ_TM, LANES), lambda i, j: (i, 0))],
        out_shape=[jax.ShapeDtypeStruct((3 * ATTN_HEADS, m, HEAD_DIM), bf16),
                   jax.ShapeDtypeStruct((m, REST_WIDTH), bf16),
                   jax.ShapeDtypeStruct((m, LANES), f32)],
        scratch_shapes=[pltpu.VMEM((INPROJ_TM, D_MODEL), bf16)],
        compiler_params=_params("arbitrary", "arbitrary"),
        name="inproj",
    )(x2, g, w_main, w_dt)


SCORE_SCALE = LOG2E / math.sqrt(HEAD_DIM)
CLASS_STEP = 4


def _attn_kernel(q_ref, k_ref, v_ref, bias_ref, o_ref,
                 tok_ref, cls4_ref, qd_ref, kd_ref, vd_ref, bias4_ref,
                 acc4_ref, m4_ref, l4_ref, acc1_ref, m1_ref, l1_ref,
                 s_buf, p_buf, mn_buf, rs_buf, *, seq):
    nblk = seq // QBLK
    n4 = seq // 4
    n16 = seq // 16
    srcs = (q_ref, k_ref, v_ref)
    dsts = ((qd_ref, 0), (kd_ref, RADIUS), (vd_ref, RADIUS))

    @pl.when(pl.program_id(1) == 0)
    def _():
        col = lax.broadcasted_iota(jnp.int32, (QBLK, KBLK), 1)
        before = col < RADIUS
        after = col >= QBLK + RADIUS
        for p in range(len(DILATIONS)):
            b = bias_ref[p, 0]
            bias4_ref[p, 0] = b
            bias4_ref[p, 1] = jnp.where(before, NEG, b)
            bias4_ref[p, 2] = jnp.where(after, NEG, b)
            bias4_ref[p, 3] = jnp.where(before | after, NEG, b)
        pad = jnp.zeros((RADIUS, HEAD_DIM), bf16)
        for ref in (kd_ref, vd_ref):
            ref[0:RADIUS, :] = pad
            ref[RADIUS + seq:2 * RADIUS + seq, :] = pad

    def stage(i, carry):
        r = pl.ds(pl.multiple_of(i * QBLK, QBLK), QBLK)
        for x, src in enumerate(srcs):
            tok_ref[x, r, :] = src[r, :].astype(f32)
        return carry

    lax.fori_loop(0, nblk, stage, 0)
    for x in range(3):
        for c4 in range(CLASS_STEP):
            cls4_ref[x, c4 * n4:(c4 + 1) * n4, :] = tok_ref[x, pl.ds(c4, n4, stride=CLASS_STEP), :]

    def tile_rows(t):
        return pl.ds(pl.multiple_of(t * QBLK, QBLK), QBLK)

    def scores(t, u, p, tiles_per_class):
        r0 = pl.multiple_of(t * QBLK, QBLK)
        q = qd_ref[pl.ds(r0, QBLK), :]
        k = kd_ref[pl.ds(r0, KBLK), :]
        jj = lax.rem(t, tiles_per_class)
        variant = (jj == 0).astype(jnp.int32) + 2 * (jj == tiles_per_class - 1).astype(jnp.int32)
        s = lax.dot_general(q, k, (((1,), (1,)), ((), ())), preferred_element_type=f32)
        s_buf[u] = s * SCORE_SCALE + bias4_ref[p, variant]

    def softmax(t, u, prev):
        s = s_buf[u]
        top = jnp.max(s, axis=-1, keepdims=True)
        if prev is None:
            mn = jnp.broadcast_to(top, (QBLK, HEAD_DIM))
        else:
            mn = jnp.maximum(prev[1][tile_rows(t), :], top)
        e_lo = jnp.exp2(s[:, :HEAD_DIM] - mn)
        e_hi = jnp.exp2(s[:, HEAD_DIM:] - mn)
        p_buf[u, :, :HEAD_DIM] = e_lo.astype(bf16)
        p_buf[u, :, HEAD_DIM:] = e_hi.astype(bf16)
        mn_buf[u] = mn
        rs_buf[u] = jnp.broadcast_to(jnp.sum(e_lo + e_hi, axis=-1, keepdims=True), (QBLK, HEAD_DIM))

    def update(t, u, prev, nxt, dst_rows):
        r = tile_rows(t)
        v = vd_ref[pl.ds(pl.multiple_of(t * QBLK, QBLK), KBLK), :]
        acc = jnp.dot(p_buf[u], v, preferred_element_type=f32)
        mn = mn_buf[u]
        l = rs_buf[u]
        if prev is not None:
            alpha = jnp.exp2(prev[1][r, :] - mn)
            l = alpha * prev[2][r, :] + l
            acc = alpha * prev[0][r, :] + acc
        if nxt is None:
            o_ref[r, :] = (acc / l).astype(o_ref.dtype)
        else:
            nxt[0][dst_rows, :] = acc
            nxt[1][dst_rows, :] = mn
            nxt[2][dst_rows, :] = l

    def run_pattern(p, prev, nxt, dst_rows_fn):
        tiles_per_class = seq // DILATIONS[p] // QBLK

        def group(i, carry):
            tiles = [i * TILE_GROUP + u for u in range(TILE_GROUP)]
            for u, t in enumerate(tiles):
                scores(t, u, p, tiles_per_class)
            for u, t in enumerate(tiles):
                softmax(t, u, prev)
            for u, t in enumerate(tiles):
                dst = None if nxt is None else dst_rows_fn(t // tiles_per_class, lax.rem(t, tiles_per_class))
                update(t, u, prev, nxt, dst)
            return carry

        lax.fori_loop(0, nblk // TILE_GROUP, group, 0)

    for c16 in range(16):
        rows = pl.ds((c16 % 4) * n4 + c16 // 4, n16, stride=CLASS_STEP)
        for x, (dst, off) in enumerate(dsts):
            dst[off + c16 * n16:off + (c16 + 1) * n16, :] = cls4_ref[x, rows, :].astype(bf16)
    state4 = (acc4_ref, m4_ref, l4_ref)
    state1 = (acc1_ref, m1_ref, l1_ref)
    run_pattern(2, None, state4,
                lambda c16, jj: pl.ds(lax.rem(c16, 4) * n4 + c16 // 4 + jj * (QBLK * CLASS_STEP), QBLK,
                                      stride=CLASS_STEP))

    def copy_cls4(i, carry):
        r0 = pl.multiple_of(i * QBLK, QBLK)
        for x, (dst, off) in enumerate(dsts):
            dst[pl.ds(r0 + off, QBLK), :] = cls4_ref[x, pl.ds(r0, QBLK), :].astype(bf16)
        return carry

    lax.fori_loop(0, nblk, copy_cls4, 0)
    run_pattern(1, state4, state1,
                lambda c4, jj: pl.ds(c4 + jj * (QBLK * CLASS_STEP), QBLK, stride=CLASS_STEP))

    def copy_tok(i, carry):
        r0 = pl.multiple_of(i * QBLK, QBLK)
        for src, (dst, off) in zip(srcs, dsts):
            dst[pl.ds(r0 + off, QBLK), :] = src[pl.ds(r0, QBLK), :]
        return carry

    lax.fori_loop(0, nblk, copy_tok, 0)
    run_pattern(0, state1, None, None)


def _attention(qkv4, bias):
    _, b, seq, _ = qkv4.shape
    np_ = len(DILATIONS)
    head = lambda first: pl.BlockSpec((None, None, seq, HEAD_DIM), lambda h, i: (first + h, i, 0, 0))
    return pl.pallas_call(
        functools.partial(_attn_kernel, seq=seq),
        grid=(ATTN_HEADS, b),
        in_specs=[head(0), head(ATTN_HEADS), head(2 * ATTN_HEADS),
                  pl.BlockSpec((np_, 1, QBLK, KBLK), lambda h, i: (0, h, 0, 0))],
        out_specs=head(0),
        out_shape=jax.ShapeDtypeStruct((ATTN_HEADS, b, seq, HEAD_DIM), bf16),
        scratch_shapes=[pltpu.VMEM((3, seq, HEAD_DIM), f32)] * 2
                       + [pltpu.VMEM((seq, HEAD_DIM), bf16)]
                       + [pltpu.VMEM((seq + 2 * RADIUS, HEAD_DIM), bf16)] * 2
                       + [pltpu.VMEM((np_, 4, QBLK, KBLK), f32)]
                       + [pltpu.VMEM((seq, HEAD_DIM), f32)] * 6
                       + [pltpu.VMEM((TILE_GROUP, QBLK, KBLK), f32),
                          pltpu.VMEM((TILE_GROUP, QBLK, KBLK), bf16),
                          pltpu.VMEM((TILE_GROUP, QBLK, HEAD_DIM), f32),
                          pltpu.VMEM((TILE_GROUP, QBLK, HEAD_DIM), f32)],
        compiler_params=_params("arbitrary", "arbitrary"),
        name="attn",
    )(qkv4, qkv4, qkv4, bias)


HALO = BF16_ROWS


def _silu(x):
    h = 0.5 * x
    return h + h * jnp.tanh(h)


def _softplus(x):
    return jnp.maximum(x, 0.0) + jnp.log1p(jnp.exp(-jnp.abs(x)))


def _ssd_kernel(z_ref, x_ref, bm_ref, cm_ref, dt_ref, cwx_ref, cbx_ref, cwb_ref, cbb_ref, cwc_ref, cbc_ref,
                hp_ref, dskip_ref, ng_ref, out_ref,
                xs_ref, bs_ref, cs_ref, bt_ref, y_ref, acs_ref, rows_ref, decx_ref, hf_ref, hb_ref,
                *stage_refs, seq):
    nc = seq // CHUNK
    g = pl.program_id(1)
    half = CONV_WIDTH // 2
    nh = HEADS_PER_GROUP
    n_slabs = len(stage_refs) // 2

    lane = lax.broadcasted_iota(jnp.int32, (CHUNK, LANES), 1)
    left = lane < SSM_HEAD_DIM
    mask_lo = jnp.where(left, 1.0, 0.0).astype(bf16)
    mask_hi = jnp.where(left, 0.0, 1.0).astype(bf16)
    li = lax.broadcasted_iota(jnp.int32, (CHUNK, CHUNK), 0)
    si = lax.broadcasted_iota(jnp.int32, (CHUNK, CHUNK), 1)
    lower = li >= si
    upper = si >= li
    tril = lower.astype(f32)
    dt_bias = hp_ref[0, 0:1, :]
    a_row = -jnp.exp(hp_ref[0, 1:2, :]) * LOG2E
    shift = lax.rem(LANES - 2 * nh * g, LANES)
    head_of_col = lax.broadcasted_iota(jnp.int32, (LANES, GROUP_WIDTH), 1) // SSM_HEAD_DIM
    head_lane = lax.broadcasted_iota(jnp.int32, (LANES, GROUP_WIDTH), 0)
    spread_f = (head_lane == head_of_col).astype(bf16)
    spread_b = (head_lane == head_of_col + nh).astype(bf16)
    tril_bf = tril.astype(bf16)

    def split3(x):
        hi = x.astype(bf16)
        rest = x - hi.astype(f32)
        mid = rest.astype(bf16)
        return hi, mid, (rest - mid.astype(f32)).astype(bf16)

    def dot3(lhs, rhs):
        return sum(jnp.dot(a, b, preferred_element_type=f32) for a in lhs for b in rhs)

    half_rows = CHUNK // 2

    def prepare(i, carry):
        r0 = pl.multiple_of(i * CHUNK, CHUNK)
        r = pl.ds(r0, CHUNK)
        lo = pl.multiple_of(jnp.maximum(r0 - HALO, 0), HALO)
        hi = pl.multiple_of(jnp.minimum(r0 + CHUNK, seq - HALO), HALO)
        has_lo = jnp.where(i > 0, 1.0, 0.0)
        has_hi = jnp.where(i < nc - 1, 1.0, 0.0)
        slab = 0
        for src, w_ref, b_ref, dst in ((x_ref, cwx_ref, cbx_ref, xs_ref),
                                       (bm_ref, cwb_ref, cbb_ref, bs_ref),
                                       (cm_ref, cwc_ref, cbc_ref, cs_ref)):
            for blk in range(src.shape[-1] // LANES):
                cols = slice(blk * LANES, (blk + 1) * LANES)
                st, ost = stage_refs[slab], stage_refs[n_slabs + slab]
                st[0:HALO, :] = src[0, pl.ds(lo, HALO), cols].astype(f32) * has_lo
                st[HALO:HALO + CHUNK, :] = src[0, r, cols].astype(f32)
                st[HALO + CHUNK:2 * HALO + CHUNK, :] = src[0, pl.ds(hi, HALO), cols].astype(f32) * has_hi
                planes = [st[pl.ds(HALO - half + s, half_rows, stride=2), :]
                          for s in range(CONV_WIDTH + 1)]
                for parity in range(2):
                    acc = b_ref[:, cols]
                    for tap in range(CONV_WIDTH):
                        acc = acc + w_ref[tap:tap + 1, cols] * planes[tap + parity]
                    ost[pl.ds(parity, half_rows, stride=2), :] = _silu(acc)
                out = ost[...].astype(bf16)
                dst[r, cols] = out
                if dst is bs_ref:
                    bt_ref[r, :] = out.astype(f32).T.astype(bf16)
                slab += 1

        raw = pltpu.roll(dt_ref[0, r, :], shift, 1)
        dtv = _softplus(raw + dt_bias)
        adt = dtv * a_row
        cum = dot3((tril_bf,), split3(adt))
        total = cum[CHUNK - 1:CHUNK, :]
        acs = jnp.where(lane < nh, cum, total - cum + adt)
        acs_ref[r, :] = acs
        rows_ref[i, 0] = (acs - jnp.log2(dtv)).T[0:2 * nh, :]
        rows_ref[i, 1] = (jnp.exp2(total - acs) * dtv).T[0:2 * nh, :]
        decay = split3(jnp.broadcast_to(jnp.exp2(total), (8, LANES)))
        decx_ref[i, 0] = dot3(decay, (spread_f,))
        decx_ref[i, 1] = dot3(decay, (spread_b,))
        return carry

    prepare(jnp.int32(0), 0)

    hf_ref[...] = jnp.zeros_like(hf_ref)
    hb_ref[...] = jnp.zeros_like(hb_ref)

    def column(acs, j):
        return jnp.broadcast_to(acs[:, j:j + 1], (CHUNK, CHUNK))

    def forward(c, carry):
        r = pl.ds(pl.multiple_of(c * CHUNK, CHUNK), CHUNK)
        acs = acs_ref[r, :]
        cmat = cs_ref[r, :]
        btf = bt_ref[r, :].astype(f32)
        cb = lax.dot_general(cmat, bs_ref[r, :], (((1,), (1,)), ((), ())), preferred_element_type=f32)
        y_off = jnp.dot(cmat, hf_ref[...].astype(bf16), preferred_element_type=f32)
        for k in range(nh // 2):
            cols = slice(k * LANES, (k + 1) * LANES)
            xp = xs_ref[r, cols]
            halves = (xp * mask_lo, xp * mask_hi)
            y_diag = snew = None
            e_f = []
            for side in range(2):
                j = 2 * k + side
                a_f, a_b = column(acs, j), column(acs, nh + j)
                w = (jnp.exp2(jnp.where(lower, a_f - rows_ref[c, 0, j:j + 1, :], NEG))
                     + jnp.exp2(jnp.where(upper, a_b - rows_ref[c, 0, nh + j:nh + j + 1, :], NEG)))
                d = jnp.dot((cb * w).astype(bf16), halves[side], preferred_element_type=f32)
                inject = (btf * rows_ref[c, 1, j:j + 1, :]).astype(bf16)
                s = jnp.dot(inject, halves[side], preferred_element_type=f32)
                y_diag = d if y_diag is None else y_diag + d
                snew = s if snew is None else snew + s
                e_f.append(jnp.exp2(a_f))
            y_ref[r, cols] = (y_diag + y_off[:, cols] * jnp.where(left, e_f[0], e_f[1])
                              + dskip_ref[:, cols] * xp.astype(f32))
            hf_ref[:, cols] = hf_ref[:, cols] * decx_ref[c, 0, 0:1, cols] + snew
        return carry

    def forward_and_prepare(c, carry):
        forward(c, carry)
        return prepare(jnp.minimum(c + 1, nc - 1), carry)

    lax.fori_loop(0, nc, forward_and_prepare, 0)

    def backward(i, carry):
        c = nc - 1 - i
        r = pl.ds(pl.multiple_of(c * CHUNK, CHUNK), CHUNK)
        acs = acs_ref[r, :]
        btf = bt_ref[r, :].astype(f32)
        y_off = jnp.dot(cs_ref[r, :], hb_ref[...].astype(bf16), preferred_element_type=f32)
        ys = []
        for k in range(nh // 2):
            cols = slice(k * LANES, (k + 1) * LANES)
            xp = xs_ref[r, cols]
            halves = (xp * mask_lo, xp * mask_hi)
            snew = None
            e_b = []
            for side in range(2):
                jb = nh + 2 * k + side
                inject = (btf * rows_ref[c, 1, jb:jb + 1, :]).astype(bf16)
                s = jnp.dot(inject, halves[side], preferred_element_type=f32)
                snew = s if snew is None else snew + s
                e_b.append(jnp.exp2(column(acs, jb)))
            ys.append(y_ref[r, cols] + y_off[:, cols] * jnp.where(left, e_b[0], e_b[1]))
            hb_ref[:, cols] = hb_ref[:, cols] * decx_ref[c, 1, 0:1, cols] + snew
        y = jnp.concatenate(ys, axis=1) * _silu(z_ref[0, r, :].astype(f32))
        y = y * lax.rsqrt(jnp.mean(y * y, axis=-1, keepdims=True) + EPS)
        out_ref[0, r, :] = (y * ng_ref[...]).astype(out_ref.dtype)
        return carry

    lax.fori_loop(0, nc, backward, 0, unroll=2)


def _ssd(proj3, dt3, conv_w, conv_b, head_params, dskip_cols, norm_g):
    b, seq, _ = proj3.shape
    gw, ns = GROUP_WIDTH, SSM_STATE
    wide = lambda off: pl.BlockSpec((1, seq, gw), lambda i, g: (i, 0, off // gw + g))
    narrow = lambda off: pl.BlockSpec((1, seq, ns), lambda i, g: (i, 0, off // ns + g))
    xoff, boff, coff = 0, SSM_WIDTH, SSM_WIDTH + SSM_GROUPS * ns
    cw = lambda off, w: pl.BlockSpec((CONV_WIDTH, w), lambda i, g: (0, off // w + g))
    cbias = lambda off, w: pl.BlockSpec((1, w), lambda i, g: (0, off // w + g))
    return pl.pallas_call(
        functools.partial(_ssd_kernel, seq=seq),
        grid=(b, SSM_GROUPS),
        in_specs=[wide(COL_ZS), wide(COL_X), narrow(COL_B), narrow(COL_C),
                  pl.BlockSpec((1, seq, LANES), lambda i, g: (i, 0, 0)),
                  cw(xoff, gw), cbias(xoff, gw), cw(boff, ns), cbias(boff, ns), cw(coff, ns), cbias(coff, ns),
                  pl.BlockSpec((1, 8, LANES), lambda i, g: (g, 0, 0)),
                  pl.BlockSpec((1, gw), lambda i, g: (0, g)),
                  pl.BlockSpec((1, gw), lambda i, g: (0, g))],
        out_specs=pl.BlockSpec((1, seq, gw), lambda i, g: (i, 0, g)),
        out_shape=jax.ShapeDtypeStruct((b, seq, SSM_WIDTH), bf16),
        scratch_shapes=[pltpu.VMEM((seq, gw), bf16),
                        pltpu.VMEM((seq, ns), bf16),
                        pltpu.VMEM((seq, ns), bf16),
                        pltpu.VMEM((seq, ns), bf16),
                        pltpu.VMEM((seq, gw), f32),
                        pltpu.VMEM((seq, LANES), f32),
                        pltpu.VMEM((seq // CHUNK, 2, 2 * HEADS_PER_GROUP, CHUNK), f32),
                        pltpu.VMEM((seq // CHUNK, 2, 8, gw), f32),
                        pltpu.VMEM((ns, gw), f32),
                        pltpu.VMEM((ns, gw), f32),
                        ]
                       + [pltpu.VMEM((CHUNK + 2 * HALO, LANES), f32)] * ((gw + 2 * ns) // LANES)
                       + [pltpu.VMEM((CHUNK, LANES), f32)] * ((gw + 2 * ns) // LANES),
        compiler_params=_params("arbitrary", "arbitrary"),
        name="ssd",
    )(proj3, proj3, proj3, proj3, dt3, conv_w, conv_b, conv_w, conv_b, conv_w, conv_b,
      head_params, dskip_cols, norm_g)


OUTPROJ_TM = 512


def _outproj_kernel(o_ref, z_ref, s_ref, x_ref, w_ref, ga_ref, gp_ref, y_ref):
    o = jnp.concatenate([o_ref[h] for h in range(ATTN_HEADS)], axis=1).astype(f32)
    a = o * lax.rsqrt(jnp.mean(o * o, axis=-1, keepdims=True) + EPS) * ga_ref[...]
    a = a * _silu(z_ref[...].astype(f32))
    mix = jnp.dot(a.astype(bf16), w_ref[0:ATTN_WIDTH, :], preferred_element_type=f32)
    mix = mix + jnp.dot(s_ref[...], w_ref[ATTN_WIDTH:MIX_WIDTH, :], preferred_element_type=f32)
    mix = mix * lax.rsqrt(jnp.mean(mix * mix, axis=-1, keepdims=True) + EPS) * gp_ref[...]
    y_ref[...] = x_ref[...] + mix


def _outproj(o3, rest2, s2, x2, w_out, g_attn, g_post):
    m = x2.shape[0]
    tm = OUTPROJ_TM
    row = lambda width, blk: pl.BlockSpec((tm, width), lambda i: (i, blk))
    const = lambda shape: pl.BlockSpec(shape, lambda i: (0, 0))
    return pl.pallas_call(
        _outproj_kernel,
        grid=(m // tm,),
        in_specs=[pl.BlockSpec((ATTN_HEADS, tm, HEAD_DIM), lambda i: (0, i, 0)),
                  row(ATTN_WIDTH, COL_ZA // ATTN_WIDTH), row(SSM_WIDTH, 0), row(D_MODEL, 0),
                  pl.BlockSpec((MIX_WIDTH, D_MODEL), lambda i: (0, 0), pipeline_mode=pl.Buffered(1)),
                  const((1, ATTN_WIDTH)), const((1, D_MODEL))],
        out_specs=row(D_MODEL, 0),
        out_shape=jax.ShapeDtypeStruct((m, D_MODEL), f32),
        compiler_params=_params("arbitrary"),
        name="outproj",
    )(o3, rest2, s2, x2, w_out, g_attn, g_post)


def _group_head_rows(fwd, bwd):
    rows = jnp.concatenate([fwd.reshape(SSM_GROUPS, HEADS_PER_GROUP), bwd.reshape(SSM_GROUPS, HEADS_PER_GROUP)],
                           axis=1)
    return jnp.pad(rows, ((0, 0), (0, LANES - 2 * HEADS_PER_GROUP)))


def _layer(x, bias, pre_g, w_main, w_dt, attn_g, conv_w, conv_b, head_params, dskip_cols, ssm_g, w_out, post_g):
    b, seq, _ = x.shape
    x2 = x.reshape(b * seq, D_MODEL)
    qkv, rest2, dt2 = _inproj(x2, pre_g, w_main, w_dt)
    o = _attention(qkv.reshape(3 * ATTN_HEADS, b, seq, HEAD_DIM), bias)
    s = _ssd(rest2.reshape(b, seq, REST_WIDTH), dt2.reshape(b, seq, LANES), conv_w, conv_b, head_params,
             dskip_cols, ssm_g)
    y = _outproj(o.reshape(ATTN_HEADS, b * seq, HEAD_DIM), rest2, s.reshape(b * seq, SSM_WIDTH), x2, w_out,
                 attn_g, post_g)
    return y.reshape(b, seq, D_MODEL)


def kernel(x_prompt, x_sample, pre_norm_g, w_in, rel_bias_table, attn_norm_g, conv_w, conv_b, dt_bias_fwd,
           dt_bias_bwd, a_log_fwd, a_log_bwd, d_skip, ssm_norm_g, w_out, post_norm_g):
    depth = w_in.shape[0]
    bias = _relbias(rel_bias_table)
    y_prompt, y_sample = x_prompt, x_sample
    for i in range(depth):
        w_main = w_in[i, :, :PROJ_MAIN].astype(bf16)
        w_dt = w_in[i, :, PROJ_MAIN:]
        w_dt = jnp.concatenate([w_dt[:, :SSM_HEADS].reshape(D_MODEL, SSM_GROUPS, HEADS_PER_GROUP),
                                w_dt[:, SSM_HEADS:].reshape(D_MODEL, SSM_GROUPS, HEADS_PER_GROUP)], axis=2)
        w_dt = jnp.pad(w_dt.reshape(D_MODEL, 2 * SSM_HEADS), ((0, 0), (0, LANES - 2 * SSM_HEADS))).astype(bf16)
        head_params = jnp.stack([_group_head_rows(dt_bias_fwd[i], dt_bias_bwd[i]),
                                 _group_head_rows(a_log_fwd[i], a_log_bwd[i])], axis=1)
        head_params = jnp.pad(head_params, ((0, 0), (0, 6), (0, 0)))
        args = (bias, pre_norm_g[i].reshape(1, D_MODEL), w_main, w_dt, attn_norm_g[i].reshape(1, ATTN_WIDTH),
                conv_w[i], conv_b[i].reshape(1, CONV_CH), head_params,
                jnp.repeat(d_skip[i], SSM_HEAD_DIM).reshape(1, SSM_WIDTH), ssm_norm_g[i].reshape(1, SSM_WIDTH),
                w_out[i].astype(bf16), post_norm_g[i].reshape(1, D_MODEL))
        y_prompt = _layer(y_prompt, *args)
        y_sample = _layer(y_sample, *args)
    return (y_prompt, y_sample)
```

```python
import functools
import math

import numpy as np
import jax
import jax.numpy as jnp
from jax import lax
from jax.experimental import pallas as pl
from jax.experimental.pallas import tpu as pltpu

D_MODEL = 2048
ATTN_HEADS = 16
HEAD_DIM = 128
ATTN_WIDTH = ATTN_HEADS * HEAD_DIM
DILATIONS = (1, 4, 16)
RADIUS = 64
N_BUCKETS = 32
BUCKET_MAX_DIST = 1024
SSM_HEADS = 32
SSM_HEAD_DIM = 64
SSM_WIDTH = SSM_HEADS * SSM_HEAD_DIM
SSM_GROUPS = 4
HEADS_PER_GROUP = SSM_HEADS // SSM_GROUPS
GROUP_WIDTH = SSM_WIDTH // SSM_GROUPS
SSM_STATE = 128
CONV_WIDTH = 5
CHUNK = 128
CONV_CH = SSM_WIDTH + 2 * SSM_GROUPS * SSM_STATE
MIX_WIDTH = ATTN_WIDTH + SSM_WIDTH
PROJ_MAIN = 4 * ATTN_WIDTH + SSM_WIDTH + CONV_CH
EPS = 1e-6

LANES = 128
BF16_ROWS = 16
QBLK = 128
KBLK = QBLK + 2 * RADIUS
TILE_GROUP = 16
LOG2E = math.log2(math.e)
NEG = -1e30
VMEM_LIMIT = 56 * 1024 * 1024

QKV_WIDTH = 3 * ATTN_WIDTH
REST_WIDTH = PROJ_MAIN - QKV_WIDTH
COL_ZA = 0
COL_ZS = ATTN_WIDTH
COL_X = COL_ZS + SSM_WIDTH
COL_B = COL_X + SSM_WIDTH
COL_C = COL_B + SSM_GROUPS * SSM_STATE

f32 = jnp.float32
bf16 = jnp.bfloat16


def _params(*sem):
    return pltpu.CompilerParams(dimension_semantics=sem, vmem_limit_bytes=VMEM_LIMIT)


def _bucket_tiles():
    qi = np.arange(QBLK)[:, None]
    kj = np.arange(KBLK)[None, :]
    rel = kj - RADIUS - qi
    nb = N_BUCKETS // 2
    max_exact = nb // 2
    tiles = []
    for dil in DILATIONS:
        r = rel * dil
        n = np.abs(r)
        nf = np.maximum(n, 1).astype(np.float32)
        large = max_exact + (np.log(nf / np.float32(max_exact)) / np.float32(math.log(BUCKET_MAX_DIST / max_exact))
                             * np.float32(nb - max_exact)).astype(np.int32)
        large = np.minimum(large, nb - 1)
        bucket = np.where(r > 0, nb, 0) + np.where(n < max_exact, n, large)
        tiles.append(np.where(np.abs(rel) <= RADIUS, bucket, -1))
    return np.stack(tiles).astype(np.int32)


def _relbias_kernel(tab_ref, idx_ref, out_ref):
    h = pl.program_id(0)
    for p in range(len(DILATIONS)):
        idx = idx_ref[p]
        acc = jnp.full((QBLK, KBLK), NEG, f32)
        for b in range(N_BUCKETS):
            acc = jnp.where(idx == b, tab_ref[b, h] * LOG2E, acc)
        out_ref[p, 0] = acc


def _relbias(rel_table):
    idx = jnp.asarray(_bucket_tiles())
    np_ = len(DILATIONS)
    return pl.pallas_call(
        _relbias_kernel,
        grid=(ATTN_HEADS,),
        in_specs=[pl.BlockSpec(memory_space=pltpu.SMEM),
                  pl.BlockSpec((np_, QBLK, KBLK), lambda h: (0, 0, 0))],
        out_specs=pl.BlockSpec((np_, 1, QBLK, KBLK), lambda h: (0, h, 0, 0)),
        out_shape=jax.ShapeDtypeStruct((np_, ATTN_HEADS, QBLK, KBLK), f32),
        compiler_params=_params("arbitrary"),
        name="relbias",
    )(rel_table, idx)


INPROJ_TM = 1024
INPROJ_TN = 1024
NORM_ROWS = 128


QKV_TILES = QKV_WIDTH // INPROJ_TN
HEADS_PER_TILE = INPROJ_TN // HEAD_DIM


CLASS_STEP = 4
ROWS4 = INPROJ_TM // 4
ROWS16 = INPROJ_TM // 16


def _inproj_kernel(x_ref, g_ref, w_ref, wdt_ref, qkv_ref, qkv4_ref, qkv16_ref, rest_ref, dt_ref,
                   h_ref, tok_ref, cls_ref):
    j = pl.program_id(1)

    @pl.when(j == 0)
    def _():
        g = g_ref[...]

        def rows(i, carry):
            r = pl.ds(pl.multiple_of(i * NORM_ROWS, NORM_ROWS), NORM_ROWS)
            x = x_ref[r, :]
            y = x * lax.rsqrt(jnp.mean(x * x, axis=-1, keepdims=True) + EPS)
            h_ref[r, :] = (y * g).astype(bf16)
            return carry

        lax.fori_loop(0, INPROJ_TM // NORM_ROWS, rows, 0)
        dt_ref[...] = jnp.dot(h_ref[...], wdt_ref[...], preferred_element_type=f32)

    @pl.when(j < QKV_TILES)
    def _():
        tile = jnp.dot(h_ref[...], w_ref[...], preferred_element_type=f32)
        for hh in range(HEADS_PER_TILE):
            slab = tile[:, hh * HEAD_DIM:(hh + 1) * HEAD_DIM]
            slot = hh % 2
            qkv_ref[hh] = slab.astype(bf16)
            tok_ref[slot] = slab
            for c4 in range(CLASS_STEP):
                cls = tok_ref[slot, pl.ds(c4, ROWS4, stride=CLASS_STEP), :]
                qkv4_ref[hh, c4] = cls.astype(bf16)
                cls_ref[slot, c4] = cls
            for c16 in range(CLASS_STEP * CLASS_STEP):
                qkv16_ref[hh, c16] = cls_ref[slot, c16 % CLASS_STEP,
                                             pl.ds(c16 // CLASS_STEP, ROWS16, stride=CLASS_STEP), :].astype(bf16)

    @pl.when(j >= QKV_TILES)
    def _():
        rest_ref[...] = jnp.dot(h_ref[...], w_ref[...], preferred_element_type=f32).astype(bf16)


def _inproj(x2, g, w_main, w_dt, seq):
    m = x2.shape[0]
    tiles = seq // INPROJ_TM
    batch = m // seq
    qkv_tile = lambda i, j: jnp.minimum(j, QKV_TILES - 1)
    split = lambda classes, rows: pl.BlockSpec(
        (HEADS_PER_TILE, None, classes, None, rows, HEAD_DIM),
        lambda i, j: (qkv_tile(i, j), i // tiles, 0, lax.rem(i, tiles), 0, 0))
    return pl.pallas_call(
        _inproj_kernel,
        grid=(m // INPROJ_TM, PROJ_MAIN // INPROJ_TN),
        in_specs=[pl.BlockSpec((INPROJ_TM, D_MODEL), lambda i, j: (i, 0)),
                  pl.BlockSpec((1, D_MODEL), lambda i, j: (0, 0)),
                  pl.BlockSpec((D_MODEL, INPROJ_TN), lambda i, j: (0, j)),
                  pl.BlockSpec((D_MODEL, LANES), lambda i, j: (0, 0))],
        out_specs=[pl.BlockSpec((HEADS_PER_TILE, INPROJ_TM, HEAD_DIM), lambda i, j: (qkv_tile(i, j), i, 0)),
                   split(CLASS_STEP, ROWS4), split(CLASS_STEP * CLASS_STEP, ROWS16),
                   pl.BlockSpec((INPROJ_TM, INPROJ_TN), lambda i, j: (i, jnp.maximum(j - QKV_TILES, 0))),
                   pl.BlockSpec((INPROJ_TM, LANES), lambda i, j: (i, 0))],
        out_shape=[jax.ShapeDtypeStruct((3 * ATTN_HEADS, m, HEAD_DIM), bf16),
                   jax.ShapeDtypeStruct((3 * ATTN_HEADS, batch, CLASS_STEP, tiles, ROWS4, HEAD_DIM), bf16),
                   jax.ShapeDtypeStruct((3 * ATTN_HEADS, batch, CLASS_STEP * CLASS_STEP, tiles, ROWS16, HEAD_DIM),
                                        bf16),
                   jax.ShapeDtypeStruct((m, REST_WIDTH), bf16),
                   jax.ShapeDtypeStruct((m, LANES), f32)],
        scratch_shapes=[pltpu.VMEM((INPROJ_TM, D_MODEL), bf16),
                        pltpu.VMEM((2, INPROJ_TM, HEAD_DIM), f32),
                        pltpu.VMEM((2, CLASS_STEP, ROWS4, HEAD_DIM), f32)],
        compiler_params=_params("arbitrary", "arbitrary"),
        name="inproj",
    )(x2, g, w_main, w_dt)


SCORE_SCALE = LOG2E / math.sqrt(HEAD_DIM)


def _attn_kernel(q_ref, k_ref, v_ref, q4_ref, k4_ref, v4_ref, q16_ref, k16_ref, v16_ref, bias_ref, o_ref,
                 qd_ref, kd_ref, vd_ref, bias4_ref,
                 acc4_ref, m4_ref, l4_ref, acc1_ref, m1_ref, l1_ref,
                 s_buf, p_buf, mn_buf, rs_buf, *, seq):
    nblk = seq // QBLK
    n4 = seq // 4
    srcs = (q_ref, k_ref, v_ref)
    dsts = ((qd_ref, 0), (kd_ref, RADIUS), (vd_ref, RADIUS))

    @pl.when(pl.program_id(1) == 0)
    def _():
        col = lax.broadcasted_iota(jnp.int32, (QBLK, KBLK), 1)
        before = col < RADIUS
        after = col >= QBLK + RADIUS
        for p in range(len(DILATIONS)):
            b = bias_ref[p, 0]
            bias4_ref[p, 0] = b
            bias4_ref[p, 1] = jnp.where(before, NEG, b)
            bias4_ref[p, 2] = jnp.where(after, NEG, b)
            bias4_ref[p, 3] = jnp.where(before | after, NEG, b)
        pad = jnp.zeros((RADIUS, HEAD_DIM), bf16)
        for ref in (kd_ref, vd_ref):
            ref[0:RADIUS, :] = pad
            ref[RADIUS + seq:2 * RADIUS + seq, :] = pad

    def load_classes(split_refs):
        for src, (dst, off) in zip(split_refs, dsts):
            classes, pieces, rows, _ = src.shape
            for c in range(classes):
                for piece in range(pieces):
                    start = off + (c * pieces + piece) * rows
                    dst[start:start + rows, :] = src[c, piece]

    def tile_rows(t):
        return pl.ds(pl.multiple_of(t * QBLK, QBLK), QBLK)

    def scores(t, u, p, tiles_per_class):
        r0 = pl.multiple_of(t * QBLK, QBLK)
        q = qd_ref[pl.ds(r0, QBLK), :]
        k = kd_ref[pl.ds(r0, KBLK), :]
        jj = lax.rem(t, tiles_per_class)
        variant = (jj == 0).astype(jnp.int32) + 2 * (jj == tiles_per_class - 1).astype(jnp.int32)
        s = lax.dot_general(q, k, (((1,), (1,)), ((), ())), preferred_element_type=f32)
        s_buf[u] = s * SCORE_SCALE + bias4_ref[p, variant]

    def softmax(t, u, prev):
        s = s_buf[u]
        top = jnp.max(s, axis=-1, keepdims=True)
        if prev is None:
            mn = jnp.broadcast_to(top, (QBLK, HEAD_DIM))
        else:
            mn = jnp.maximum(prev[1][tile_rows(t), :], top)
        e_lo = jnp.exp2(s[:, :HEAD_DIM] - mn)
        e_hi = jnp.exp2(s[:, HEAD_DIM:] - mn)
        p_buf[u, :, :HEAD_DIM] = e_lo.astype(bf16)
        p_buf[u, :, HEAD_DIM:] = e_hi.astype(bf16)
        mn_buf[u] = mn
        rs_buf[u] = jnp.broadcast_to(jnp.sum(e_lo + e_hi, axis=-1, keepdims=True), (QBLK, HEAD_DIM))

    def update(t, u, prev, nxt, dst_rows):
        r = tile_rows(t)
        v = vd_ref[pl.ds(pl.multiple_of(t * QBLK, QBLK), KBLK), :]
        acc = jnp.dot(p_buf[u], v, preferred_element_type=f32)
        mn = mn_buf[u]
        l = rs_buf[u]
        if prev is not None:
            alpha = jnp.exp2(prev[1][r, :] - mn)
            l = alpha * prev[2][r, :] + l
            acc = alpha * prev[0][r, :] + acc
        if nxt is None:
            o_ref[r, :] = (acc / l).astype(o_ref.dtype)
        else:
            nxt[0][dst_rows, :] = acc
            nxt[1][dst_rows, :] = mn
            nxt[2][dst_rows, :] = l

    def run_pattern(p, prev, nxt, dst_rows_fn):
        tiles_per_class = seq // DILATIONS[p] // QBLK

        def group(i, carry):
            tiles = [i * TILE_GROUP + u for u in range(TILE_GROUP)]
            for u, t in enumerate(tiles):
                scores(t, u, p, tiles_per_class)
            for u, t in enumerate(tiles):
                softmax(t, u, prev)
            for u, t in enumerate(tiles):
                dst = None if nxt is None else dst_rows_fn(t // tiles_per_class, lax.rem(t, tiles_per_class))
                update(t, u, prev, nxt, dst)
            return carry

        lax.fori_loop(0, nblk // TILE_GROUP, group, 0)

    load_classes((q16_ref, k16_ref, v16_ref))
    state4 = (acc4_ref, m4_ref, l4_ref)
    state1 = (acc1_ref, m1_ref, l1_ref)
    run_pattern(2, None, state4,
                lambda c16, jj: pl.ds(lax.rem(c16, 4) * n4 + c16 // 4 + jj * (QBLK * CLASS_STEP), QBLK,
                                      stride=CLASS_STEP))

    load_classes((q4_ref, k4_ref, v4_ref))
    run_pattern(1, state4, state1,
                lambda c4, jj: pl.ds(c4 + jj * (QBLK * CLASS_STEP), QBLK, stride=CLASS_STEP))

    def copy_tok(i, carry):
        r0 = pl.multiple_of(i * QBLK, QBLK)
        for src, (dst, off) in zip(srcs, dsts):
            dst[pl.ds(r0 + off, QBLK), :] = src[pl.ds(r0, QBLK), :]
        return carry

    lax.fori_loop(0, nblk, copy_tok, 0)
    run_pattern(0, state1, None, None)


def _attention(qkv, qkv_mod4, qkv_mod16, bias):
    _, b, seq, _ = qkv.shape
    np_ = len(DILATIONS)
    head = lambda first: pl.BlockSpec((None, None, seq, HEAD_DIM), lambda h, i: (first + h, i, 0, 0))
    split = lambda arr, first: pl.BlockSpec((None, None) + arr.shape[2:], lambda h, i: (first + h, i, 0, 0, 0, 0))
    three = lambda spec, *a: [spec(*a, first) for first in (0, ATTN_HEADS, 2 * ATTN_HEADS)]
    return pl.pallas_call(
        functools.partial(_attn_kernel, seq=seq),
        grid=(ATTN_HEADS, b),
        in_specs=three(head) + three(split, qkv_mod4) + three(split, qkv_mod16)
                 + [pl.BlockSpec((np_, 1, QBLK, KBLK), lambda h, i: (0, h, 0, 0))],
        out_specs=head(0),
        out_shape=jax.ShapeDtypeStruct((ATTN_HEADS, b, seq, HEAD_DIM), bf16),
        scratch_shapes=[pltpu.VMEM((seq, HEAD_DIM), bf16)]
                       + [pltpu.VMEM((seq + 2 * RADIUS, HEAD_DIM), bf16)] * 2
                       + [pltpu.VMEM((np_, 4, QBLK, KBLK), f32)]
                       + [pltpu.VMEM((seq, HEAD_DIM), f32)] * 6
                       + [pltpu.VMEM((TILE_GROUP, QBLK, KBLK), f32),
                          pltpu.VMEM((TILE_GROUP, QBLK, KBLK), bf16),
                          pltpu.VMEM((TILE_GROUP, QBLK, HEAD_DIM), f32),
                          pltpu.VMEM((TILE_GROUP, QBLK, HEAD_DIM), f32)],
        compiler_params=_params("arbitrary", "arbitrary"),
        name="attn",
    )(qkv, qkv, qkv, qkv_mod4, qkv_mod4, qkv_mod4, qkv_mod16, qkv_mod16, qkv_mod16, bias)


HALO = BF16_ROWS


def _silu(x):
    h = 0.5 * x
    return h + h * jnp.tanh(h)


def _softplus(x):
    return jnp.maximum(x, 0.0) + jnp.log1p(jnp.exp(-jnp.abs(x)))


def _ssd_kernel(z_ref, x_ref, bm_ref, cm_ref, dt_ref, cwx_ref, cbx_ref, cwb_ref, cbb_ref, cwc_ref, cbc_ref,
                hp_ref, dskip_ref, ng_ref, out_ref,
                xs_ref, bs_ref, cs_ref, bt_ref, y_ref, acs_ref, rows_ref, decx_ref, hf_ref, hb_ref,
                *stage_refs, seq):
    nc = seq // CHUNK
    g = pl.program_id(1)
    half = CONV_WIDTH // 2
    nh = HEADS_PER_GROUP
    n_slabs = len(stage_refs) // 2

    lane = lax.broadcasted_iota(jnp.int32, (CHUNK, LANES), 1)
    left = lane < SSM_HEAD_DIM
    mask_lo = jnp.where(left, 1.0, 0.0).astype(bf16)
    mask_hi = jnp.where(left, 0.0, 1.0).astype(bf16)
    li = lax.broadcasted_iota(jnp.int32, (CHUNK, CHUNK), 0)
    si = lax.broadcasted_iota(jnp.int32, (CHUNK, CHUNK), 1)
    lower = li >= si
    upper = si >= li
    tril = lower.astype(f32)
    dt_bias = hp_ref[0, 0:1, :]
    a_row = -jnp.exp(hp_ref[0, 1:2, :]) * LOG2E
    shift = lax.rem(LANES - 2 * nh * g, LANES)
    head_of_col = lax.broadcasted_iota(jnp.int32, (LANES, GROUP_WIDTH), 1) // SSM_HEAD_DIM
    head_lane = lax.broadcasted_iota(jnp.int32, (LANES, GROUP_WIDTH), 0)
    spread_f = (head_lane == head_of_col).astype(bf16)
    spread_b = (head_lane == head_of_col + nh).astype(bf16)
    tril_bf = tril.astype(bf16)

    def split3(x):
        hi = x.astype(bf16)
        rest = x - hi.astype(f32)
        mid = rest.astype(bf16)
        return hi, mid, (rest - mid.astype(f32)).astype(bf16)

    def dot3(lhs, rhs):
        return sum(jnp.dot(a, b, preferred_element_type=f32) for a in lhs for b in rhs)

    half_rows = CHUNK // 2

    def prepare(i, carry):
        r0 = pl.multiple_of(i * CHUNK, CHUNK)
        r = pl.ds(r0, CHUNK)
        lo = pl.multiple_of(jnp.maximum(r0 - HALO, 0), HALO)
        hi = pl.multiple_of(jnp.minimum(r0 + CHUNK, seq - HALO), HALO)
        has_lo = jnp.where(i > 0, 1.0, 0.0)
        has_hi = jnp.where(i < nc - 1, 1.0, 0.0)
        slab = 0
        for src, w_ref, b_ref, dst in ((x_ref, cwx_ref, cbx_ref, xs_ref),
                                       (bm_ref, cwb_ref, cbb_ref, bs_ref),
                                       (cm_ref, cwc_ref, cbc_ref, cs_ref)):
            for blk in range(src.shape[-1] // LANES):
                cols = slice(blk * LANES, (blk + 1) * LANES)
                st, ost = stage_refs[slab], stage_refs[n_slabs + slab]
                st[0:HALO, :] = src[0, pl.ds(lo, HALO), cols].astype(f32) * has_lo
                st[HALO:HALO + CHUNK, :] = src[0, r, cols].astype(f32)
                st[HALO + CHUNK:2 * HALO + CHUNK, :] = src[0, pl.ds(hi, HALO), cols].astype(f32) * has_hi
                planes = [st[pl.ds(HALO - half + s, half_rows, stride=2), :]
                          for s in range(CONV_WIDTH + 1)]
                for parity in range(2):
                    acc = b_ref[:, cols]
                    for tap in range(CONV_WIDTH):
                        acc = acc + w_ref[tap:tap + 1, cols] * planes[tap + parity]
                    ost[pl.ds(parity, half_rows, stride=2), :] = _silu(acc)
                out = ost[...].astype(bf16)
                dst[r, cols] = out
                if dst is bs_ref:
                    bt_ref[r, :] = out.astype(f32).T.astype(bf16)
                slab += 1

        raw = pltpu.roll(dt_ref[0, r, :], shift, 1)
        dtv = _softplus(raw + dt_bias)
        adt = dtv * a_row
        cum = dot3((tril_bf,), split3(adt))
        total = cum[CHUNK - 1:CHUNK, :]
        acs = jnp.where(lane < nh, cum, total - cum + adt)
        acs_ref[r, :] = acs
        rows_ref[i, 0] = (acs - jnp.log2(dtv)).T[0:2 * nh, :]
        rows_ref[i, 1] = (jnp.exp2(total - acs) * dtv).T[0:2 * nh, :]
        decay = split3(jnp.broadcast_to(jnp.exp2(total), (8, LANES)))
        decx_ref[i, 0] = dot3(decay, (spread_f,))
        decx_ref[i, 1] = dot3(decay, (spread_b,))
        return carry

    prepare(jnp.int32(0), 0)

    hf_ref[...] = jnp.zeros_like(hf_ref)
    hb_ref[...] = jnp.zeros_like(hb_ref)

    def column(acs, j):
        return jnp.broadcast_to(acs[:, j:j + 1], (CHUNK, CHUNK))

    def forward(c, carry):
        r = pl.ds(pl.multiple_of(c * CHUNK, CHUNK), CHUNK)
        acs = acs_ref[r, :]
        cmat = cs_ref[r, :]
        btf = bt_ref[r, :].astype(f32)
        cb = lax.dot_general(cmat, bs_ref[r, :], (((1,), (1,)), ((), ())), preferred_element_type=f32)
        y_off = jnp.dot(cmat, hf_ref[...].astype(bf16), preferred_element_type=f32)
        for k in range(nh // 2):
            cols = slice(k * LANES, (k + 1) * LANES)
            xp = xs_ref[r, cols]
            halves = (xp * mask_lo, xp * mask_hi)
            y_diag = snew = None
            e_f = []
            for side in range(2):
                j = 2 * k + side
                a_f, a_b = column(acs, j), column(acs, nh + j)
                w = (jnp.exp2(jnp.where(lower, a_f - rows_ref[c, 0, j:j + 1, :], NEG))
                     + jnp.exp2(jnp.where(upper, a_b - rows_ref[c, 0, nh + j:nh + j + 1, :], NEG)))
                d = jnp.dot((cb * w).astype(bf16), halves[side], preferred_element_type=f32)
                inject = (btf * rows_ref[c, 1, j:j + 1, :]).astype(bf16)
                s = jnp.dot(inject, halves[side], preferred_element_type=f32)
                y_diag = d if y_diag is None else y_diag + d
                snew = s if snew is None else snew + s
                e_f.append(jnp.exp2(a_f))
            y_ref[r, cols] = (y_diag + y_off[:, cols] * jnp.where(left, e_f[0], e_f[1])
                              + dskip_ref[:, cols] * xp.astype(f32))
            hf_ref[:, cols] = hf_ref[:, cols] * decx_ref[c, 0, 0:1, cols] + snew
        return carry

    def forward_and_prepare(c, carry):
        forward(c, carry)
        return prepare(jnp.minimum(c + 1, nc - 1), carry)

    lax.fori_loop(0, nc, forward_and_prepare, 0)

    def backward(i, carry):
        c = nc - 1 - i
        r = pl.ds(pl.multiple_of(c * CHUNK, CHUNK), CHUNK)
        acs = acs_ref[r, :]
        btf = bt_ref[r, :].astype(f32)
        y_off = jnp.dot(cs_ref[r, :], hb_ref[...].astype(bf16), preferred_element_type=f32)
        ys = []
        for k in range(nh // 2):
            cols = slice(k * LANES, (k + 1) * LANES)
            xp = xs_ref[r, cols]
            halves = (xp * mask_lo, xp * mask_hi)
            snew = None
            e_b = []
            for side in range(2):
                jb = nh + 2 * k + side
                inject = (btf * rows_ref[c, 1, jb:jb + 1, :]).astype(bf16)
                s = jnp.dot(inject, halves[side], preferred_element_type=f32)
                snew = s if snew is None else snew + s
                e_b.append(jnp.exp2(column(acs, jb)))
            ys.append(y_ref[r, cols] + y_off[:, cols] * jnp.where(left, e_b[0], e_b[1]))
            hb_ref[:, cols] = hb_ref[:, cols] * decx_ref[c, 1, 0:1, cols] + snew
        y = jnp.concatenate(ys, axis=1) * _silu(z_ref[0, r, :].astype(f32))
        y = y * lax.rsqrt(jnp.mean(y * y, axis=-1, keepdims=True) + EPS)
        out_ref[0, r, :] = (y * ng_ref[...]).astype(out_ref.dtype)
        return carry

    lax.fori_loop(0, nc, backward, 0, unroll=2)


def _ssd(proj3, dt3, conv_w, conv_b, head_params, dskip_cols, norm_g):
    b, seq, _ = proj3.shape
    gw, ns = GROUP_WIDTH, SSM_STATE
    wide = lambda off: pl.BlockSpec((1, seq, gw), lambda i, g: (i, 0, off // gw + g))
    narrow = lambda off: pl.BlockSpec((1, seq, ns), lambda i, g: (i, 0, off // ns + g))
    xoff, boff, coff = 0, SSM_WIDTH, SSM_WIDTH + SSM_GROUPS * ns
    cw = lambda off, w: pl.BlockSpec((CONV_WIDTH, w), lambda i, g: (0, off // w + g))
    cbias = lambda off, w: pl.BlockSpec((1, w), lambda i, g: (0, off // w + g))
    return pl.pallas_call(
        functools.partial(_ssd_kernel, seq=seq),
        grid=(b, SSM_GROUPS),
        in_specs=[wide(COL_ZS), wide(COL_X), narrow(COL_B), narrow(COL_C),
                  pl.BlockSpec((1, seq, LANES), lambda i, g: (i, 0, 0)),
                  cw(xoff, gw), cbias(xoff, gw), cw(boff, ns), cbias(boff, ns), cw(coff, ns), cbias(coff, ns),
                  pl.BlockSpec((1, 8, LANES), lambda i, g: (g, 0, 0)),
                  pl.BlockSpec((1, gw), lambda i, g: (0, g)),
                  pl.BlockSpec((1, gw), lambda i, g: (0, g))],
        out_specs=pl.BlockSpec((1, seq, gw), lambda i, g: (i, 0, g)),
        out_shape=jax.ShapeDtypeStruct((b, seq, SSM_WIDTH), bf16),
        scratch_shapes=[pltpu.VMEM((seq, gw), bf16),
                        pltpu.VMEM((seq, ns), bf16),
                        pltpu.VMEM((seq, ns), bf16),
                        pltpu.VMEM((seq, ns), bf16),
                        pltpu.VMEM((seq, gw), f32),
                        pltpu.VMEM((seq, LANES), f32),
                        pltpu.VMEM((seq // CHUNK, 2, 2 * HEADS_PER_GROUP, CHUNK), f32),
                        pltpu.VMEM((seq // CHUNK, 2, 8, gw), f32),
                        pltpu.VMEM((ns, gw), f32),
                        pltpu.VMEM((ns, gw), f32),
                        ]
                       + [pltpu.VMEM((CHUNK + 2 * HALO, LANES), f32)] * ((gw + 2 * ns) // LANES)
                       + [pltpu.VMEM((CHUNK, LANES), f32)] * ((gw + 2 * ns) // LANES),
        compiler_params=_params("arbitrary", "arbitrary"),
        name="ssd",
    )(proj3, proj3, proj3, proj3, dt3, conv_w, conv_b, conv_w, conv_b, conv_w, conv_b,
      head_params, dskip_cols, norm_g)


OUTPROJ_TM = 512


def _outproj_kernel(o_ref, z_ref, s_ref, x_ref, w_ref, ga_ref, gp_ref, y_ref):
    o = jnp.concatenate([o_ref[h] for h in range(ATTN_HEADS)], axis=1).astype(f32)
    a = o * lax.rsqrt(jnp.mean(o * o, axis=-1, keepdims=True) + EPS) * ga_ref[...]
    a = a * _silu(z_ref[...].astype(f32))
    mix = jnp.dot(a.astype(bf16), w_ref[0:ATTN_WIDTH, :], preferred_element_type=f32)
    mix = mix + jnp.dot(s_ref[...], w_ref[ATTN_WIDTH:MIX_WIDTH, :], preferred_element_type=f32)
    mix = mix * lax.rsqrt(jnp.mean(mix * mix, axis=-1, keepdims=True) + EPS) * gp_ref[...]
    y_ref[...] = x_ref[...] + mix


def _outproj(o3, rest2, s2, x2, w_out, g_attn, g_post):
    m = x2.shape[0]
    tm = OUTPROJ_TM
    row = lambda width, blk: pl.BlockSpec((tm, width), lambda i: (i, blk))
    const = lambda shape: pl.BlockSpec(shape, lambda i: (0, 0))
    return pl.pallas_call(
        _outproj_kernel,
        grid=(m // tm,),
        in_specs=[pl.BlockSpec((ATTN_HEADS, tm, HEAD_DIM), lambda i: (0, i, 0)),
                  row(ATTN_WIDTH, COL_ZA // ATTN_WIDTH), row(SSM_WIDTH, 0), row(D_MODEL, 0),
                  pl.BlockSpec((MIX_WIDTH, D_MODEL), lambda i: (0, 0), pipeline_mode=pl.Buffered(1)),
                  const((1, ATTN_WIDTH)), const((1, D_MODEL))],
        out_specs=row(D_MODEL, 0),
        out_shape=jax.ShapeDtypeStruct((m, D_MODEL), f32),
        compiler_params=_params("arbitrary"),
        name="outproj",
    )(o3, rest2, s2, x2, w_out, g_attn, g_post)


def _group_head_rows(fwd, bwd):
    rows = jnp.concatenate([fwd.reshape(SSM_GROUPS, HEADS_PER_GROUP), bwd.reshape(SSM_GROUPS, HEADS_PER_GROUP)],
                           axis=1)
    return jnp.pad(rows, ((0, 0), (0, LANES - 2 * HEADS_PER_GROUP)))


def _layer(x, bias, pre_g, w_main, w_dt, attn_g, conv_w, conv_b, head_params, dskip_cols, ssm_g, w_out, post_g):
    b, seq, _ = x.shape
    x2 = x.reshape(b * seq, D_MODEL)
    qkv, qkv_mod4, qkv_mod16, rest2, dt2 = _inproj(x2, pre_g, w_main, w_dt, seq)
    o = _attention(qkv.reshape(3 * ATTN_HEADS, b, seq, HEAD_DIM), qkv_mod4, qkv_mod16, bias)
    s = _ssd(rest2.reshape(b, seq, REST_WIDTH), dt2.reshape(b, seq, LANES), conv_w, conv_b, head_params,
             dskip_cols, ssm_g)
    y = _outproj(o.reshape(ATTN_HEADS, b * seq, HEAD_DIM), rest2, s.reshape(b * seq, SSM_WIDTH), x2, w_out,
                 attn_g, post_g)
    return y.reshape(b, seq, D_MODEL)


def kernel(x_prompt, x_sample, pre_norm_g, w_in, rel_bias_table, attn_norm_g, conv_w, conv_b, dt_bias_fwd,
           dt_bias_bwd, a_log_fwd, a_log_bwd, d_skip, ssm_norm_g, w_out, post_norm_g):
    depth = w_in.shape[0]
    bias = _relbias(rel_bias_table)
    y_prompt, y_sample = x_prompt, x_sample
    for i in range(depth):
        w_main = w_in[i, :, :PROJ_MAIN].astype(bf16)
        w_dt = w_in[i, :, PROJ_MAIN:]
        w_dt = jnp.concatenate([w_dt[:, :SSM_HEADS].reshape(D_MODEL, SSM_GROUPS, HEADS_PER_GROUP),
                                w_dt[:, SSM_HEADS:].reshape(D_MODEL, SSM_GROUPS, HEADS_PER_GROUP)], axis=2)
        w_dt = jnp.pad(w_dt.reshape(D_MODEL, 2 * SSM_HEADS), ((0, 0), (0, LANES - 2 * SSM_HEADS))).astype(bf16)
        head_params = jnp.stack([_group_head_rows(dt_bias_fwd[i], dt_bias_bwd[i]),
                                 _group_head_rows(a_log_fwd[i], a_log_bwd[i])], axis=1)
        head_params = jnp.pad(head_params, ((0, 0), (0, 6), (0, 0)))
        args = (bias, pre_norm_g[i].reshape(1, D_MODEL), w_main, w_dt, attn_norm_g[i].reshape(1, ATTN_WIDTH),
                conv_w[i], conv_b[i].reshape(1, CONV_CH), head_params,
                jnp.repeat(d_skip[i], SSM_HEAD_DIM).reshape(1, SSM_WIDTH), ssm_norm_g[i].reshape(1, SSM_WIDTH),
                w_out[i].astype(bf16), post_norm_g[i].reshape(1, D_MODEL))
        y_prompt = _layer(y_prompt, *args)
        y_sample = _layer(y_sample, *args)
    return (y_prompt, y_sample)
```

```python
import functools
import math

import numpy as np
import jax
import jax.numpy as jnp
from jax import lax
from jax.experimental import pallas as pl
from jax.experimental.pallas import tpu as pltpu

D_MODEL = 2048
ATTN_HEADS = 16
HEAD_DIM = 128
ATTN_WIDTH = ATTN_HEADS * HEAD_DIM
DILATIONS = (1, 4, 16)
RADIUS = 64
N_BUCKETS = 32
BUCKET_MAX_DIST = 1024
SSM_HEADS = 32
SSM_HEAD_DIM = 64
SSM_WIDTH = SSM_HEADS * SSM_HEAD_DIM
SSM_GROUPS = 4
HEADS_PER_GROUP = SSM_HEADS // SSM_GROUPS
GROUP_WIDTH = SSM_WIDTH // SSM_GROUPS
SSM_STATE = 128
CONV_WIDTH = 5
CHUNK = 128
CONV_CH = SSM_WIDTH + 2 * SSM_GROUPS * SSM_STATE
MIX_WIDTH = ATTN_WIDTH + SSM_WIDTH
PROJ_MAIN = 4 * ATTN_WIDTH + SSM_WIDTH + CONV_CH
EPS = 1e-6

LANES = 128
BF16_ROWS = 16
QBLK = 128
KBLK = QBLK + 2 * RADIUS
TILE_GROUP = 16
LOG2E = math.log2(math.e)
NEG = -1e30
VMEM_LIMIT = 56 * 1024 * 1024

QKV_WIDTH = 3 * ATTN_WIDTH
REST_WIDTH = PROJ_MAIN - QKV_WIDTH
COL_ZA = 0
COL_ZS = ATTN_WIDTH
COL_X = COL_ZS + SSM_WIDTH
COL_B = COL_X + SSM_WIDTH
COL_C = COL_B + SSM_GROUPS * SSM_STATE

f32 = jnp.float32
bf16 = jnp.bfloat16


def _params(*sem):
    return pltpu.CompilerParams(dimension_semantics=sem, vmem_limit_bytes=VMEM_LIMIT)


def _bucket_tiles():
    qi = np.arange(QBLK)[:, None]
    kj = np.arange(KBLK)[None, :]
    rel = kj - RADIUS - qi
    nb = N_BUCKETS // 2
    max_exact = nb // 2
    tiles = []
    for dil in DILATIONS:
        r = rel * dil
        n = np.abs(r)
        nf = np.maximum(n, 1).astype(np.float32)
        large = max_exact + (np.log(nf / np.float32(max_exact)) / np.float32(math.log(BUCKET_MAX_DIST / max_exact))
                             * np.float32(nb - max_exact)).astype(np.int32)
        large = np.minimum(large, nb - 1)
        bucket = np.where(r > 0, nb, 0) + np.where(n < max_exact, n, large)
        tiles.append(np.where(np.abs(rel) <= RADIUS, bucket, -1))
    return np.stack(tiles).astype(np.int32)


def _relbias_kernel(tab_ref, idx_ref, out_ref):
    h = pl.program_id(0)
    for p in range(len(DILATIONS)):
        idx = idx_ref[p]
        acc = jnp.full((QBLK, KBLK), NEG, f32)
        for b in range(N_BUCKETS):
            acc = jnp.where(idx == b, tab_ref[b, h] * LOG2E, acc)
        out_ref[p, 0] = acc


def _relbias(rel_table):
    idx = jnp.asarray(_bucket_tiles())
    np_ = len(DILATIONS)
    return pl.pallas_call(
        _relbias_kernel,
        grid=(ATTN_HEADS,),
        in_specs=[pl.BlockSpec(memory_space=pltpu.SMEM),
                  pl.BlockSpec((np_, QBLK, KBLK), lambda h: (0, 0, 0))],
        out_specs=pl.BlockSpec((np_, 1, QBLK, KBLK), lambda h: (0, h, 0, 0)),
        out_shape=jax.ShapeDtypeStruct((np_, ATTN_HEADS, QBLK, KBLK), f32),
        compiler_params=_params("arbitrary"),
        name="relbias",
    )(rel_table, idx)


INPROJ_TM = 1024
INPROJ_TN = 1024
NORM_ROWS = 128


QKV_TILES = QKV_WIDTH // INPROJ_TN
HEADS_PER_TILE = INPROJ_TN // HEAD_DIM


NORM_STEPS = INPROJ_TM // NORM_ROWS


def _inproj_kernel(x_ref, g_ref, w_ref, wdt_ref, qkv_ref, rest_ref, dt_ref, h_ref):
    i = pl.program_id(0)
    j = pl.program_id(1)
    cur = lax.rem(i, 2)

    def normalise(rows, slot):
        x = x_ref[rows, :]
        y = x * lax.rsqrt(jnp.mean(x * x, axis=-1, keepdims=True) + EPS)
        h_ref[slot, rows, :] = (y * g_ref[...]).astype(bf16)

    def matmul(with_norm):
        tile = jnp.dot(h_ref[cur], w_ref[...], preferred_element_type=f32).astype(bf16)
        if with_norm:
            normalise(pl.ds(pl.multiple_of((j - 1) * NORM_ROWS, NORM_ROWS), NORM_ROWS), 1 - cur)
        return tile

    def store_qkv(tile):
        for hh in range(HEADS_PER_TILE):
            qkv_ref[hh] = tile[:, hh * HEAD_DIM:(hh + 1) * HEAD_DIM]

    @pl.when((i == 0) & (j == 0))
    def _():
        def rows(c, carry):
            normalise(pl.ds(pl.multiple_of(c * NORM_ROWS, NORM_ROWS), NORM_ROWS), 0)
            return carry

        lax.fori_loop(0, NORM_STEPS, rows, 0)

    @pl.when(j == 0)
    def _():
        dt_ref[...] = jnp.dot(h_ref[cur], wdt_ref[...], preferred_element_type=f32)
        store_qkv(matmul(False))

    @pl.when((j >= 1) & (j < QKV_TILES))
    def _():
        store_qkv(matmul(True))

    @pl.when((j >= QKV_TILES) & (j <= NORM_STEPS))
    def _():
        rest_ref[...] = matmul(True)

    @pl.when(j > NORM_STEPS)
    def _():
        rest_ref[...] = matmul(False)


def _inproj(x2, g, w_main, w_dt):
    m = x2.shape[0]
    last = m // INPROJ_TM - 1
    return pl.pallas_call(
        _inproj_kernel,
        grid=(m // INPROJ_TM, PROJ_MAIN // INPROJ_TN),
        in_specs=[pl.BlockSpec((INPROJ_TM, D_MODEL),
                               lambda i, j: (jnp.where((i == 0) & (j == 0), 0, jnp.minimum(i + 1, last)), 0)),
                  pl.BlockSpec((1, D_MODEL), lambda i, j: (0, 0)),
                  pl.BlockSpec((D_MODEL, INPROJ_TN), lambda i, j: (0, j)),
                  pl.BlockSpec((D_MODEL, LANES), lambda i, j: (0, 0))],
        out_specs=[pl.BlockSpec((HEADS_PER_TILE, INPROJ_TM, HEAD_DIM),
                                lambda i, j: (jnp.minimum(j, QKV_TILES - 1), i, 0)),
                   pl.BlockSpec((INPROJ_TM, INPROJ_TN), lambda i, j: (i, jnp.maximum(j - QKV_TILES, 0))),
                   pl.BlockSpec((INPROJ_TM, LANES), lambda i, j: (i, 0))],
        out_shape=[jax.ShapeDtypeStruct((3 * ATTN_HEADS, m, HEAD_DIM), bf16),
                   jax.ShapeDtypeStruct((m, REST_WIDTH), bf16),
                   jax.ShapeDtypeStruct((m, LANES), f32)],
        scratch_shapes=[pltpu.VMEM((2, INPROJ_TM, D_MODEL), bf16)],
        compiler_params=_params("arbitrary", "arbitrary"),
        name="inproj",
    )(x2, g, w_main, w_dt)


SCORE_SCALE = LOG2E / math.sqrt(HEAD_DIM)
CLASS_STEP = 4


def _attn_kernel(q_ref, k_ref, v_ref, bias_ref, o_ref,
                 tok_ref, cls4_ref, qd_ref, kd_ref, vd_ref, bias4_ref,
                 acc4_ref, m4_ref, l4_ref, acc1_ref, m1_ref, l1_ref,
                 s_buf, p_buf, mn_buf, rs_buf, *, seq):
    nblk = seq // QBLK
    n4 = seq // 4
    n16 = seq // 16
    srcs = (q_ref, k_ref, v_ref)
    dsts = ((qd_ref, 0), (kd_ref, RADIUS), (vd_ref, RADIUS))

    @pl.when(pl.program_id(1) == 0)
    def _():
        col = lax.broadcasted_iota(jnp.int32, (QBLK, KBLK), 1)
        before = col < RADIUS
        after = col >= QBLK + RADIUS
        for p in range(len(DILATIONS)):
            b = bias_ref[p, 0]
            bias4_ref[p, 0] = b
            bias4_ref[p, 1] = jnp.where(before, NEG, b)
            bias4_ref[p, 2] = jnp.where(after, NEG, b)
            bias4_ref[p, 3] = jnp.where(before | after, NEG, b)
        pad = jnp.zeros((RADIUS, HEAD_DIM), bf16)
        for ref in (kd_ref, vd_ref):
            ref[0:RADIUS, :] = pad
            ref[RADIUS + seq:2 * RADIUS + seq, :] = pad

    def stage(i, carry):
        r = pl.ds(pl.multiple_of(i * QBLK, QBLK), QBLK)
        for x, src in enumerate(srcs):
            tok_ref[x, r, :] = src[r, :].astype(f32)
        return carry

    lax.fori_loop(0, nblk, stage, 0)
    for x in range(3):
        for c4 in range(CLASS_STEP):
            cls4_ref[x, c4 * n4:(c4 + 1) * n4, :] = tok_ref[x, pl.ds(c4, n4, stride=CLASS_STEP), :]

    def tile_rows(t):
        return pl.ds(pl.multiple_of(t * QBLK, QBLK), QBLK)

    def scores(t, u, p, tiles_per_class):
        r0 = pl.multiple_of(t * QBLK, QBLK)
        q = qd_ref[pl.ds(r0, QBLK), :]
        k = kd_ref[pl.ds(r0, KBLK), :]
        jj = lax.rem(t, tiles_per_class)
        variant = (jj == 0).astype(jnp.int32) + 2 * (jj == tiles_per_class - 1).astype(jnp.int32)
        s = lax.dot_general(q, k, (((1,), (1,)), ((), ())), preferred_element_type=f32)
        s_buf[u] = s * SCORE_SCALE + bias4_ref[p, variant]

    def softmax(t, u, prev):
        s = s_buf[u]
        top = jnp.max(s, axis=-1, keepdims=True)
        if prev is None:
            mn = jnp.broadcast_to(top, (QBLK, HEAD_DIM))
        else:
            mn = jnp.maximum(prev[1][tile_rows(t), :], top)
        e_lo = jnp.exp2(s[:, :HEAD_DIM] - mn)
        e_hi = jnp.exp2(s[:, HEAD_DIM:] - mn)
        p_buf[u, :, :HEAD_DIM] = e_lo.astype(bf16)
        p_buf[u, :, HEAD_DIM:] = e_hi.astype(bf16)
        mn_buf[u] = mn
        rs_buf[u] = jnp.broadcast_to(jnp.sum(e_lo + e_hi, axis=-1, keepdims=True), (QBLK, HEAD_DIM))

    def update(t, u, prev, nxt, dst_rows):
        r = tile_rows(t)
        v = vd_ref[pl.ds(pl.multiple_of(t * QBLK, QBLK), KBLK), :]
        acc = jnp.dot(p_buf[u], v, preferred_element_type=f32)
        mn = mn_buf[u]
        l = rs_buf[u]
        if prev is not None:
            alpha = jnp.exp2(prev[1][r, :] - mn)
            l = alpha * prev[2][r, :] + l
            acc = alpha * prev[0][r, :] + acc
        if nxt is None:
            o_ref[r, :] = (acc / l).astype(o_ref.dtype)
        else:
            nxt[0][dst_rows, :] = acc
            nxt[1][dst_rows, :] = mn
            nxt[2][dst_rows, :] = l

    def run_pattern(p, prev, nxt, dst_rows_fn):
        tiles_per_class = seq // DILATIONS[p] // QBLK

        def group(i, carry):
            tiles = [i * TILE_GROUP + u for u in range(TILE_GROUP)]
            for u, t in enumerate(tiles):
                scores(t, u, p, tiles_per_class)
            for u, t in enumerate(tiles):
                softmax(t, u, prev)
            for u, t in enumerate(tiles):
                dst = None if nxt is None else dst_rows_fn(t // tiles_per_class, lax.rem(t, tiles_per_class))
                update(t, u, prev, nxt, dst)
            return carry

        lax.fori_loop(0, nblk // TILE_GROUP, group, 0)

    for c16 in range(16):
        rows = pl.ds((c16 % 4) * n4 + c16 // 4, n16, stride=CLASS_STEP)
        for x, (dst, off) in enumerate(dsts):
            dst[off + c16 * n16:off + (c16 + 1) * n16, :] = cls4_ref[x, rows, :].astype(bf16)
    state4 = (acc4_ref, m4_ref, l4_ref)
    state1 = (acc1_ref, m1_ref, l1_ref)
    run_pattern(2, None, state4,
                lambda c16, jj: pl.ds(lax.rem(c16, 4) * n4 + c16 // 4 + jj * (QBLK * CLASS_STEP), QBLK,
                                      stride=CLASS_STEP))

    def copy_cls4(i, carry):
        r0 = pl.multiple_of(i * QBLK, QBLK)
        for x, (dst, off) in enumerate(dsts):
            dst[pl.ds(r0 + off, QBLK), :] = cls4_ref[x, pl.ds(r0, QBLK), :].astype(bf16)
        return carry

    lax.fori_loop(0, nblk, copy_cls4, 0)
    run_pattern(1, state4, state1,
                lambda c4, jj: pl.ds(c4 + jj * (QBLK * CLASS_STEP), QBLK, stride=CLASS_STEP))

    def copy_tok(i, carry):
        r0 = pl.multiple_of(i * QBLK, QBLK)
        for src, (dst, off) in zip(srcs, dsts):
            dst[pl.ds(r0 + off, QBLK), :] = src[pl.ds(r0, QBLK), :]
        return carry

    lax.fori_loop(0, nblk, copy_tok, 0)
    run_pattern(0, state1, None, None)


def _attention(qkv4, bias):
    _, b, seq, _ = qkv4.shape
    np_ = len(DILATIONS)
    head = lambda first: pl.BlockSpec((None, None, seq, HEAD_DIM), lambda h, i: (first + h, i, 0, 0))
    return pl.pallas_call(
        functools.partial(_attn_kernel, seq=seq),
        grid=(ATTN_HEADS, b),
        in_specs=[head(0), head(ATTN_HEADS), head(2 * ATTN_HEADS),
                  pl.BlockSpec((np_, 1, QBLK, KBLK), lambda h, i: (0, h, 0, 0))],
        out_specs=head(0),
        out_shape=jax.ShapeDtypeStruct((ATTN_HEADS, b, seq, HEAD_DIM), bf16),
        scratch_shapes=[pltpu.VMEM((3, seq, HEAD_DIM), f32)] * 2
                       + [pltpu.VMEM((seq, HEAD_DIM), bf16)]
                       + [pltpu.VMEM((seq + 2 * RADIUS, HEAD_DIM), bf16)] * 2
                       + [pltpu.VMEM((np_, 4, QBLK, KBLK), f32)]
                       + [pltpu.VMEM((seq, HEAD_DIM), f32)] * 6
                       + [pltpu.VMEM((TILE_GROUP, QBLK, KBLK), f32),
                          pltpu.VMEM((TILE_GROUP, QBLK, KBLK), bf16),
                          pltpu.VMEM((TILE_GROUP, QBLK, HEAD_DIM), f32),
                          pltpu.VMEM((TILE_GROUP, QBLK, HEAD_DIM), f32)],
        compiler_params=_params("arbitrary", "arbitrary"),
        name="attn",
    )(qkv4, qkv4, qkv4, bias)


HALO = BF16_ROWS


def _silu(x):
    h = 0.5 * x
    return h + h * jnp.tanh(h)


def _softplus(x):
    return jnp.maximum(x, 0.0) + jnp.log1p(jnp.exp(-jnp.abs(x)))


def _ssd_kernel(z_ref, x_ref, bm_ref, cm_ref, dt_ref, cwx_ref, cbx_ref, cwb_ref, cbb_ref, cwc_ref, cbc_ref,
                hp_ref, dskip_ref, ng_ref, out_ref,
                xs_ref, bs_ref, cs_ref, bt_ref, y_ref, acs_ref, rows_ref, decx_ref, hf_ref, hb_ref,
                *stage_refs, seq):
    nc = seq // CHUNK
    g = pl.program_id(1)
    half = CONV_WIDTH // 2
    nh = HEADS_PER_GROUP
    n_slabs = len(stage_refs) // 2

    lane = lax.broadcasted_iota(jnp.int32, (CHUNK, LANES), 1)
    left = lane < SSM_HEAD_DIM
    mask_lo = jnp.where(left, 1.0, 0.0).astype(bf16)
    mask_hi = jnp.where(left, 0.0, 1.0).astype(bf16)
    li = lax.broadcasted_iota(jnp.int32, (CHUNK, CHUNK), 0)
    si = lax.broadcasted_iota(jnp.int32, (CHUNK, CHUNK), 1)
    lower = li >= si
    upper = si >= li
    tril = lower.astype(f32)
    dt_bias = hp_ref[0, 0:1, :]
    a_row = -jnp.exp(hp_ref[0, 1:2, :]) * LOG2E
    shift = lax.rem(LANES - 2 * nh * g, LANES)
    head_of_col = lax.broadcasted_iota(jnp.int32, (LANES, GROUP_WIDTH), 1) // SSM_HEAD_DIM
    head_lane = lax.broadcasted_iota(jnp.int32, (LANES, GROUP_WIDTH), 0)
    spread_f = (head_lane == head_of_col).astype(bf16)
    spread_b = (head_lane == head_of_col + nh).astype(bf16)
    tril_bf = tril.astype(bf16)

    def split3(x):
        hi = x.astype(bf16)
        rest = x - hi.astype(f32)
        mid = rest.astype(bf16)
        return hi, mid, (rest - mid.astype(f32)).astype(bf16)

    def dot3(lhs, rhs):
        return sum(jnp.dot(a, b, preferred_element_type=f32) for a in lhs for b in rhs)

    half_rows = CHUNK // 2

    def prepare(i, carry):
        r0 = pl.multiple_of(i * CHUNK, CHUNK)
        r = pl.ds(r0, CHUNK)
        lo = pl.multiple_of(jnp.maximum(r0 - HALO, 0), HALO)
        hi = pl.multiple_of(jnp.minimum(r0 + CHUNK, seq - HALO), HALO)
        has_lo = jnp.where(i > 0, 1.0, 0.0)
        has_hi = jnp.where(i < nc - 1, 1.0, 0.0)
        slab = 0
        for src, w_ref, b_ref, dst in ((x_ref, cwx_ref, cbx_ref, xs_ref),
                                       (bm_ref, cwb_ref, cbb_ref, bs_ref),
                                       (cm_ref, cwc_ref, cbc_ref, cs_ref)):
            for blk in range(src.shape[-1] // LANES):
                cols = slice(blk * LANES, (blk + 1) * LANES)
                st, ost = stage_refs[slab], stage_refs[n_slabs + slab]
                st[0:HALO, :] = src[0, pl.ds(lo, HALO), cols].astype(f32) * has_lo
                st[HALO:HALO + CHUNK, :] = src[0, r, cols].astype(f32)
                st[HALO + CHUNK:2 * HALO + CHUNK, :] = src[0, pl.ds(hi, HALO), cols].astype(f32) * has_hi
                planes = [st[pl.ds(HALO - half + s, half_rows, stride=2), :]
                          for s in range(CONV_WIDTH + 1)]
                for parity in range(2):
                    acc = b_ref[:, cols]
                    for tap in range(CONV_WIDTH):
                        acc = acc + w_ref[tap:tap + 1, cols] * planes[tap + parity]
                    ost[pl.ds(parity, half_rows, stride=2), :] = _silu(acc)
                out = ost[...].astype(bf16)
                dst[r, cols] = out
                if dst is bs_ref:
                    bt_ref[r, :] = out.astype(f32).T.astype(bf16)
                slab += 1

        raw = pltpu.roll(dt_ref[0, r, :], shift, 1)
        dtv = _softplus(raw + dt_bias)
        adt = dtv * a_row
        cum = dot3((tril_bf,), split3(adt))
        total = cum[CHUNK - 1:CHUNK, :]
        acs = jnp.where(lane < nh, cum, total - cum + adt)
        acs_ref[r, :] = acs
        rows_ref[i, 0] = (acs - jnp.log2(dtv)).T[0:2 * nh, :]
        rows_ref[i, 1] = (jnp.exp2(total - acs) * dtv).T[0:2 * nh, :]
        decay = split3(jnp.broadcast_to(jnp.exp2(total), (8, LANES)))
        decx_ref[i, 0] = dot3(decay, (spread_f,))
        decx_ref[i, 1] = dot3(decay, (spread_b,))
        return carry

    prepare(jnp.int32(0), 0)

    hf_ref[...] = jnp.zeros_like(hf_ref)
    hb_ref[...] = jnp.zeros_like(hb_ref)

    def column(acs, j):
        return jnp.broadcast_to(acs[:, j:j + 1], (CHUNK, CHUNK))

    def forward(c, carry):
        r = pl.ds(pl.multiple_of(c * CHUNK, CHUNK), CHUNK)
        acs = acs_ref[r, :]
        cmat = cs_ref[r, :]
        btf = bt_ref[r, :].astype(f32)
        cb = lax.dot_general(cmat, bs_ref[r, :], (((1,), (1,)), ((), ())), preferred_element_type=f32)
        y_off = jnp.dot(cmat, hf_ref[...].astype(bf16), preferred_element_type=f32)
        for k in range(nh // 2):
            cols = slice(k * LANES, (k + 1) * LANES)
            xp = xs_ref[r, cols]
            halves = (xp * mask_lo, xp * mask_hi)
            y_diag = snew = None
            e_f = []
            for side in range(2):
                j = 2 * k + side
                a_f, a_b = column(acs, j), column(acs, nh + j)
                w = (jnp.exp2(jnp.where(lower, a_f - rows_ref[c, 0, j:j + 1, :], NEG))
                     + jnp.exp2(jnp.where(upper, a_b - rows_ref[c, 0, nh + j:nh + j + 1, :], NEG)))
                d = jnp.dot((cb * w).astype(bf16), halves[side], preferred_element_type=f32)
                inject = (btf * rows_ref[c, 1, j:j + 1, :]).astype(bf16)
                s = jnp.dot(inject, halves[side], preferred_element_type=f32)
                y_diag = d if y_diag is None else y_diag + d
                snew = s if snew is None else snew + s
                e_f.append(jnp.exp2(a_f))
            y_ref[r, cols] = (y_diag + y_off[:, cols] * jnp.where(left, e_f[0], e_f[1])
                              + dskip_ref[:, cols] * xp.astype(f32))
            hf_ref[:, cols] = hf_ref[:, cols] * decx_ref[c, 0, 0:1, cols] + snew
        return carry

    def forward_and_prepare(c, carry):
        forward(c, carry)
        return prepare(jnp.minimum(c + 1, nc - 1), carry)

    lax.fori_loop(0, nc, forward_and_prepare, 0)

    def backward(i, carry):
        c = nc - 1 - i
        r = pl.ds(pl.multiple_of(c * CHUNK, CHUNK), CHUNK)
        acs = acs_ref[r, :]
        btf = bt_ref[r, :].astype(f32)
        y_off = jnp.dot(cs_ref[r, :], hb_ref[...].astype(bf16), preferred_element_type=f32)
        ys = []
        for k in range(nh // 2):
            cols = slice(k * LANES, (k + 1) * LANES)
            xp = xs_ref[r, cols]
            halves = (xp * mask_lo, xp * mask_hi)
            snew = None
            e_b = []
            for side in range(2):
                jb = nh + 2 * k + side
                inject = (btf * rows_ref[c, 1, jb:jb + 1, :]).astype(bf16)
                s = jnp.dot(inject, halves[side], preferred_element_type=f32)
                snew = s if snew is None else snew + s
                e_b.append(jnp.exp2(column(acs, jb)))
            ys.append(y_ref[r, cols] + y_off[:, cols] * jnp.where(left, e_b[0], e_b[1]))
            hb_ref[:, cols] = hb_ref[:, cols] * decx_ref[c, 1, 0:1, cols] + snew
        y = jnp.concatenate(ys, axis=1) * _silu(z_ref[0, r, :].astype(f32))
        y = y * lax.rsqrt(jnp.mean(y * y, axis=-1, keepdims=True) + EPS)
        out_ref[0, r, :] = (y * ng_ref[...]).astype(out_ref.dtype)
        return carry

    lax.fori_loop(0, nc, backward, 0, unroll=2)


def _ssd(proj3, dt3, conv_w, conv_b, head_params, dskip_cols, norm_g):
    b, seq, _ = proj3.shape
    gw, ns = GROUP_WIDTH, SSM_STATE
    wide = lambda off: pl.BlockSpec((1, seq, gw), lambda i, g: (i, 0, off // gw + g))
    narrow = lambda off: pl.BlockSpec((1, seq, ns), lambda i, g: (i, 0, off // ns + g))
    xoff, boff, coff = 0, SSM_WIDTH, SSM_WIDTH + SSM_GROUPS * ns
    cw = lambda off, w: pl.BlockSpec((CONV_WIDTH, w), lambda i, g: (0, off // w + g))
    cbias = lambda off, w: pl.BlockSpec((1, w), lambda i, g: (0, off // w + g))
    return pl.pallas_call(
        functools.partial(_ssd_kernel, seq=seq),
        grid=(b, SSM_GROUPS),
        in_specs=[wide(COL_ZS), wide(COL_X), narrow(COL_B), narrow(COL_C),
                  pl.BlockSpec((1, seq, LANES), lambda i, g: (i, 0, 0)),
                  cw(xoff, gw), cbias(xoff, gw), cw(boff, ns), cbias(boff, ns), cw(coff, ns), cbias(coff, ns),
                  pl.BlockSpec((1, 8, LANES), lambda i, g: (g, 0, 0)),
                  pl.BlockSpec((1, gw), lambda i, g: (0, g)),
                  pl.BlockSpec((1, gw), lambda i, g: (0, g))],
        out_specs=pl.BlockSpec((1, seq, gw), lambda i, g: (i, 0, g)),
        out_shape=jax.ShapeDtypeStruct((b, seq, SSM_WIDTH), bf16),
        scratch_shapes=[pltpu.VMEM((seq, gw), bf16),
                        pltpu.VMEM((seq, ns), bf16),
                        pltpu.VMEM((seq, ns), bf16),
                        pltpu.VMEM((seq, ns), bf16),
                        pltpu.VMEM((seq, gw), f32),
                        pltpu.VMEM((seq, LANES), f32),
                        pltpu.VMEM((seq // CHUNK, 2, 2 * HEADS_PER_GROUP, CHUNK), f32),
                        pltpu.VMEM((seq // CHUNK, 2, 8, gw), f32),
                        pltpu.VMEM((ns, gw), f32),
                        pltpu.VMEM((ns, gw), f32),
                        ]
                       + [pltpu.VMEM((CHUNK + 2 * HALO, LANES), f32)] * ((gw + 2 * ns) // LANES)
                       + [pltpu.VMEM((CHUNK, LANES), f32)] * ((gw + 2 * ns) // LANES),
        compiler_params=_params("arbitrary", "arbitrary"),
        name="ssd",
    )(proj3, proj3, proj3, proj3, dt3, conv_w, conv_b, conv_w, conv_b, conv_w, conv_b,
      head_params, dskip_cols, norm_g)


OUTPROJ_TM = 512


def _outproj_kernel(o_ref, z_ref, s_ref, x_ref, w_ref, ga_ref, gp_ref, y_ref):
    o = jnp.concatenate([o_ref[h] for h in range(ATTN_HEADS)], axis=1).astype(f32)
    a = o * lax.rsqrt(jnp.mean(o * o, axis=-1, keepdims=True) + EPS) * ga_ref[...]
    a = a * _silu(z_ref[...].astype(f32))
    mix = jnp.dot(a.astype(bf16), w_ref[0:ATTN_WIDTH, :], preferred_element_type=f32)
    mix = mix + jnp.dot(s_ref[...], w_ref[ATTN_WIDTH:MIX_WIDTH, :], preferred_element_type=f32)
    mix = mix * lax.rsqrt(jnp.mean(mix * mix, axis=-1, keepdims=True) + EPS) * gp_ref[...]
    y_ref[...] = x_ref[...] + mix


def _outproj(o3, rest2, s2, x2, w_out, g_attn, g_post):
    m = x2.shape[0]
    tm = OUTPROJ_TM
    row = lambda width, blk: pl.BlockSpec((tm, width), lambda i: (i, blk))
    const = lambda shape: pl.BlockSpec(shape, lambda i: (0, 0))
    return pl.pallas_call(
        _outproj_kernel,
        grid=(m // tm,),
        in_specs=[pl.BlockSpec((ATTN_HEADS, tm, HEAD_DIM), lambda i: (0, i, 0)),
                  row(ATTN_WIDTH, COL_ZA // ATTN_WIDTH), row(SSM_WIDTH, 0), row(D_MODEL, 0),
                  pl.BlockSpec((MIX_WIDTH, D_MODEL), lambda i: (0, 0), pipeline_mode=pl.Buffered(1)),
                  const((1, ATTN_WIDTH)), const((1, D_MODEL))],
        out_specs=row(D_MODEL, 0),
        out_shape=jax.ShapeDtypeStruct((m, D_MODEL), f32),
        compiler_params=_params("arbitrary"),
        name="outproj",
    )(o3, rest2, s2, x2, w_out, g_attn, g_post)


def _group_head_rows(fwd, bwd):
    rows = jnp.concatenate([fwd.reshape(SSM_GROUPS, HEADS_PER_GROUP), bwd.reshape(SSM_GROUPS, HEADS_PER_GROUP)],
                           axis=1)
    return jnp.pad(rows, ((0, 0), (0, LANES - 2 * HEADS_PER_GROUP)))


def _layer(x, bias, pre_g, w_main, w_dt, attn_g, conv_w, conv_b, head_params, dskip_cols, ssm_g, w_out, post_g):
    b, seq, _ = x.shape
    x2 = x.reshape(b * seq, D_MODEL)
    qkv, rest2, dt2 = _inproj(x2, pre_g, w_main, w_dt)
    o = _attention(qkv.reshape(3 * ATTN_HEADS, b, seq, HEAD_DIM), bias)
    s = _ssd(rest2.reshape(b, seq, REST_WIDTH), dt2.reshape(b, seq, LANES), conv_w, conv_b, head_params,
             dskip_cols, ssm_g)
    y = _outproj(o.reshape(ATTN_HEADS, b * seq, HEAD_DIM), rest2, s.reshape(b * seq, SSM_WIDTH), x2, w_out,
                 attn_g, post_g)
    return y.reshape(b, seq, D_MODEL)


def kernel(x_prompt, x_sample, pre_norm_g, w_in, rel_bias_table, attn_norm_g, conv_w, conv_b, dt_bias_fwd,
           dt_bias_bwd, a_log_fwd, a_log_bwd, d_skip, ssm_norm_g, w_out, post_norm_g):
    depth = w_in.shape[0]
    bias = _relbias(rel_bias_table)
    y_prompt, y_sample = x_prompt, x_sample
    for i in range(depth):
        w_main = w_in[i, :, :PROJ_MAIN].astype(bf16)
        w_dt = w_in[i, :, PROJ_MAIN:]
        w_dt = jnp.concatenate([w_dt[:, :SSM_HEADS].reshape(D_MODEL, SSM_GROUPS, HEADS_PER_GROUP),
                                w_dt[:, SSM_HEADS:].reshape(D_MODEL, SSM_GROUPS, HEADS_PER_GROUP)], axis=2)
        w_dt = jnp.pad(w_dt.reshape(D_MODEL, 2 * SSM_HEADS), ((0, 0), (0, LANES - 2 * SSM_HEADS))).astype(bf16)
        head_params = jnp.stack([_group_head_rows(dt_bias_fwd[i], dt_bias_bwd[i]),
                                 _group_head_rows(a_log_fwd[i], a_log_bwd[i])], axis=1)
        head_params = jnp.pad(head_params, ((0, 0), (0, 6), (0, 0)))
        args = (bias, pre_norm_g[i].reshape(1, D_MODEL), w_main, w_dt, attn_norm_g[i].reshape(1, ATTN_WIDTH),
                conv_w[i], conv_b[i].reshape(1, CONV_CH), head_params,
                jnp.repeat(d_skip[i], SSM_HEAD_DIM).reshape(1, SSM_WIDTH), ssm_norm_g[i].reshape(1, SSM_WIDTH),
                w_out[i].astype(bf16), post_norm_g[i].reshape(1, D_MODEL))
        y_prompt = _layer(y_prompt, *args)
        y_sample = _layer(y_sample, *args)
    return (y_prompt, y_sample)
```

```python
import functools
import math

import numpy as np
import jax
import jax.numpy as jnp
from jax import lax
from jax.experimental import pallas as pl
from jax.experimental.pallas import tpu as pltpu

D_MODEL = 2048
ATTN_HEADS = 16
HEAD_DIM = 128
ATTN_WIDTH = ATTN_HEADS * HEAD_DIM
DILATIONS = (1, 4, 16)
RADIUS = 64
N_BUCKETS = 32
BUCKET_MAX_DIST = 1024
SSM_HEADS = 32
SSM_HEAD_DIM = 64
SSM_WIDTH = SSM_HEADS * SSM_HEAD_DIM
SSM_GROUPS = 4
HEADS_PER_GROUP = SSM_HEADS // SSM_GROUPS
GROUP_WIDTH = SSM_WIDTH // SSM_GROUPS
SSM_STATE = 128
CONV_WIDTH = 5
CHUNK = 128
CONV_CH = SSM_WIDTH + 2 * SSM_GROUPS * SSM_STATE
MIX_WIDTH = ATTN_WIDTH + SSM_WIDTH
PROJ_MAIN = 4 * ATTN_WIDTH + SSM_WIDTH + CONV_CH
EPS = 1e-6

LANES = 128
BF16_ROWS = 16
QBLK = 128
KBLK = QBLK + 2 * RADIUS
TILE_GROUP = 16
LOG2E = math.log2(math.e)
NEG = -1e30
VMEM_LIMIT = 56 * 1024 * 1024

QKV_WIDTH = 3 * ATTN_WIDTH
REST_WIDTH = PROJ_MAIN - QKV_WIDTH
COL_ZA = 0
COL_ZS = ATTN_WIDTH
COL_X = COL_ZS + SSM_WIDTH
COL_B = COL_X + SSM_WIDTH
COL_C = COL_B + SSM_GROUPS * SSM_STATE

f32 = jnp.float32
bf16 = jnp.bfloat16


def _params(*sem):
    return pltpu.CompilerParams(dimension_semantics=sem, vmem_limit_bytes=VMEM_LIMIT)


def _bucket_tiles():
    qi = np.arange(QBLK)[:, None]
    kj = np.arange(KBLK)[None, :]
    rel = kj - RADIUS - qi
    nb = N_BUCKETS // 2
    max_exact = nb // 2
    tiles = []
    for dil in DILATIONS:
        r = rel * dil
        n = np.abs(r)
        nf = np.maximum(n, 1).astype(np.float32)
        large = max_exact + (np.log(nf / np.float32(max_exact)) / np.float32(math.log(BUCKET_MAX_DIST / max_exact))
                             * np.float32(nb - max_exact)).astype(np.int32)
        large = np.minimum(large, nb - 1)
        bucket = np.where(r > 0, nb, 0) + np.where(n < max_exact, n, large)
        tiles.append(np.where(np.abs(rel) <= RADIUS, bucket, -1))
    return np.stack(tiles).astype(np.int32)


def _relbias_kernel(tab_ref, idx_ref, out_ref):
    h = pl.program_id(0)
    for p in range(len(DILATIONS)):
        idx = idx_ref[p]
        acc = jnp.full((QBLK, KBLK), NEG, f32)
        for b in range(N_BUCKETS):
            acc = jnp.where(idx == b, tab_ref[b, h] * LOG2E, acc)
        out_ref[p, 0] = acc


def _relbias(rel_table):
    idx = jnp.asarray(_bucket_tiles())
    np_ = len(DILATIONS)
    return pl.pallas_call(
        _relbias_kernel,
        grid=(ATTN_HEADS,),
        in_specs=[pl.BlockSpec(memory_space=pltpu.SMEM),
                  pl.BlockSpec((np_, QBLK, KBLK), lambda h: (0, 0, 0))],
        out_specs=pl.BlockSpec((np_, 1, QBLK, KBLK), lambda h: (0, h, 0, 0)),
        out_shape=jax.ShapeDtypeStruct((np_, ATTN_HEADS, QBLK, KBLK), f32),
        compiler_params=_params("arbitrary"),
        name="relbias",
    )(rel_table, idx)


INPROJ_TM = 1024
INPROJ_TN = 1024
NORM_ROWS = 128


QKV_TILES = QKV_WIDTH // INPROJ_TN
HEADS_PER_TILE = INPROJ_TN // HEAD_DIM


NORM_STEPS = INPROJ_TM // NORM_ROWS


def _inproj_kernel(x_ref, g_ref, w_ref, wdt_ref, qkv_ref, rest_ref, dt_ref, h_ref):
    i = pl.program_id(0)
    j = pl.program_id(1)
    cur = lax.rem(i, 2)

    def normalise(rows, slot):
        x = x_ref[rows, :]
        y = x * lax.rsqrt(jnp.mean(x * x, axis=-1, keepdims=True) + EPS)
        h_ref[slot, rows, :] = (y * g_ref[...]).astype(bf16)

    def matmul(with_norm):
        tile = jnp.dot(h_ref[cur], w_ref[...], preferred_element_type=f32).astype(bf16)
        if with_norm:
            normalise(pl.ds(pl.multiple_of((j - 1) * NORM_ROWS, NORM_ROWS), NORM_ROWS), 1 - cur)
        return tile

    def store_qkv(tile):
        for hh in range(HEADS_PER_TILE):
            qkv_ref[hh] = tile[:, hh * HEAD_DIM:(hh + 1) * HEAD_DIM]

    @pl.when((i == 0) & (j == 0))
    def _():
        def rows(c, carry):
            normalise(pl.ds(pl.multiple_of(c * NORM_ROWS, NORM_ROWS), NORM_ROWS), 0)
            return carry

        lax.fori_loop(0, NORM_STEPS, rows, 0)

    @pl.when(j == 0)
    def _():
        dt_ref[...] = jnp.dot(h_ref[cur], wdt_ref[...], preferred_element_type=f32)
        store_qkv(matmul(False))

    @pl.when((j >= 1) & (j < QKV_TILES))
    def _():
        store_qkv(matmul(True))

    @pl.when((j >= QKV_TILES) & (j <= NORM_STEPS))
    def _():
        rest_ref[...] = matmul(True)

    @pl.when(j > NORM_STEPS)
    def _():
        rest_ref[...] = matmul(False)


def _inproj(x2, g, w_main, w_dt):
    m = x2.shape[0]
    last = m // INPROJ_TM - 1
    return pl.pallas_call(
        _inproj_kernel,
        grid=(m // INPROJ_TM, PROJ_MAIN // INPROJ_TN),
        in_specs=[pl.BlockSpec((INPROJ_TM, D_MODEL),
                               lambda i, j: (jnp.where((i == 0) & (j == 0), 0, jnp.minimum(i + 1, last)), 0)),
                  pl.BlockSpec((1, D_MODEL), lambda i, j: (0, 0)),
                  pl.BlockSpec((D_MODEL, INPROJ_TN), lambda i, j: (0, j)),
                  pl.BlockSpec((D_MODEL, LANES), lambda i, j: (0, 0))],
        out_specs=[pl.BlockSpec((HEADS_PER_TILE, INPROJ_TM, HEAD_DIM),
                                lambda i, j: (jnp.minimum(j, QKV_TILES - 1), i, 0)),
                   pl.BlockSpec((INPROJ_TM, INPROJ_TN), lambda i, j: (i, jnp.maximum(j - QKV_TILES, 0))),
                   pl.BlockSpec((INPROJ_TM, LANES), lambda i, j: (i, 0))],
        out_shape=[jax.ShapeDtypeStruct((3 * ATTN_HEADS, m, HEAD_DIM), bf16),
                   jax.ShapeDtypeStruct((m, REST_WIDTH), bf16),
                   jax.ShapeDtypeStruct((m, LANES), f32)],
        scratch_shapes=[pltpu.VMEM((2, INPROJ_TM, D_MODEL), bf16)],
        compiler_params=_params("arbitrary", "arbitrary"),
        name="inproj",
    )(x2, g, w_main, w_dt)


SCORE_SCALE = LOG2E / math.sqrt(HEAD_DIM)
CLASS_STEP = 4


def _attn_kernel(*refs, seq):
    views = {dil: refs[3 * x:3 * x + 3] for x, dil in enumerate(DILATIONS)}
    (bias_ref, o_ref, qd_ref, kd_ref, vd_ref, bias4_ref,
     acc4_ref, m4_ref, l4_ref, acc1_ref, m1_ref, l1_ref,
     s_buf, p_buf, mn_buf, rs_buf) = refs[3 * len(DILATIONS):]
    nblk = seq // QBLK
    n4 = seq // 4
    dsts = ((qd_ref, 0), (kd_ref, RADIUS), (vd_ref, RADIUS))

    @pl.when(pl.program_id(1) == 0)
    def _():
        col = lax.broadcasted_iota(jnp.int32, (QBLK, KBLK), 1)
        before = col < RADIUS
        after = col >= QBLK + RADIUS
        for p in range(len(DILATIONS)):
            b = bias_ref[p, 0]
            bias4_ref[p, 0] = b
            bias4_ref[p, 1] = jnp.where(before, NEG, b)
            bias4_ref[p, 2] = jnp.where(after, NEG, b)
            bias4_ref[p, 3] = jnp.where(before | after, NEG, b)
        pad = jnp.zeros((RADIUS, HEAD_DIM), bf16)
        for ref in (kd_ref, vd_ref):
            ref[0:RADIUS, :] = pad
            ref[RADIUS + seq:2 * RADIUS + seq, :] = pad

    def load_classes(dil):
        n = seq // dil
        for src, (dst, off) in zip(views[dil], dsts):
            for c in range(dil):
                dst[off + c * n:off + (c + 1) * n, :] = src[:, c * HEAD_DIM:(c + 1) * HEAD_DIM]

    def tile_rows(t):
        return pl.ds(pl.multiple_of(t * QBLK, QBLK), QBLK)

    def scores(t, u, p, tiles_per_class):
        r0 = pl.multiple_of(t * QBLK, QBLK)
        q = qd_ref[pl.ds(r0, QBLK), :]
        k = kd_ref[pl.ds(r0, KBLK), :]
        jj = lax.rem(t, tiles_per_class)
        variant = (jj == 0).astype(jnp.int32) + 2 * (jj == tiles_per_class - 1).astype(jnp.int32)
        s = lax.dot_general(q, k, (((1,), (1,)), ((), ())), preferred_element_type=f32)
        s_buf[u] = s * SCORE_SCALE + bias4_ref[p, variant]

    def softmax(t, u, prev):
        s = s_buf[u]
        top = jnp.max(s, axis=-1, keepdims=True)
        if prev is None:
            mn = jnp.broadcast_to(top, (QBLK, HEAD_DIM))
        else:
            mn = jnp.maximum(prev[1][tile_rows(t), :], top)
        e_lo = jnp.exp2(s[:, :HEAD_DIM] - mn)
        e_hi = jnp.exp2(s[:, HEAD_DIM:] - mn)
        p_buf[u, :, :HEAD_DIM] = e_lo.astype(bf16)
        p_buf[u, :, HEAD_DIM:] = e_hi.astype(bf16)
        mn_buf[u] = mn
        rs_buf[u] = jnp.broadcast_to(jnp.sum(e_lo + e_hi, axis=-1, keepdims=True), (QBLK, HEAD_DIM))

    def update(t, u, prev, nxt, dst_rows):
        r = tile_rows(t)
        v = vd_ref[pl.ds(pl.multiple_of(t * QBLK, QBLK), KBLK), :]
        acc = jnp.dot(p_buf[u], v, preferred_element_type=f32)
        mn = mn_buf[u]
        l = rs_buf[u]
        if prev is not None:
            alpha = jnp.exp2(prev[1][r, :] - mn)
            l = alpha * prev[2][r, :] + l
            acc = alpha * prev[0][r, :] + acc
        if nxt is None:
            o_ref[r, :] = (acc / l).astype(o_ref.dtype)
        else:
            nxt[0][dst_rows, :] = acc
            nxt[1][dst_rows, :] = mn
            nxt[2][dst_rows, :] = l

    def run_pattern(p, prev, nxt, dst_rows_fn):
        tiles_per_class = seq // DILATIONS[p] // QBLK

        def group(i, carry):
            tiles = [i * TILE_GROUP + u for u in range(TILE_GROUP)]
            for u, t in enumerate(tiles):
                scores(t, u, p, tiles_per_class)
            for u, t in enumerate(tiles):
                softmax(t, u, prev)
            for u, t in enumerate(tiles):
                dst = None if nxt is None else dst_rows_fn(t // tiles_per_class, lax.rem(t, tiles_per_class))
                update(t, u, prev, nxt, dst)
            return carry

        lax.fori_loop(0, nblk // TILE_GROUP, group, 0)

    load_classes(16)
    state4 = (acc4_ref, m4_ref, l4_ref)
    state1 = (acc1_ref, m1_ref, l1_ref)
    run_pattern(2, None, state4,
                lambda c16, jj: pl.ds(lax.rem(c16, 4) * n4 + c16 // 4 + jj * (QBLK * CLASS_STEP), QBLK,
                                      stride=CLASS_STEP))

    load_classes(4)
    run_pattern(1, state4, state1,
                lambda c4, jj: pl.ds(c4 + jj * (QBLK * CLASS_STEP), QBLK, stride=CLASS_STEP))

    load_classes(1)
    run_pattern(0, state1, None, None)


def _attention(qkv4, bias):
    _, b, seq, _ = qkv4.shape
    np_ = len(DILATIONS)
    head = lambda first: pl.BlockSpec((None, None, seq, HEAD_DIM), lambda h, i: (first + h, i, 0, 0))
    firsts = (0, ATTN_HEADS, 2 * ATTN_HEADS)
    views = [qkv4.reshape(3 * ATTN_HEADS, b, seq // dil, dil * HEAD_DIM) for dil in DILATIONS]
    view_spec = lambda dil, first: pl.BlockSpec((None, None, seq // dil, dil * HEAD_DIM),
                                                lambda h, i: (first + h, i, 0, 0))
    return pl.pallas_call(
        functools.partial(_attn_kernel, seq=seq),
        grid=(ATTN_HEADS, b),
        in_specs=[view_spec(dil, first) for dil in DILATIONS for first in firsts]
                 + [pl.BlockSpec((np_, 1, QBLK, KBLK), lambda h, i: (0, h, 0, 0))],
        out_specs=head(0),
        out_shape=jax.ShapeDtypeStruct((ATTN_HEADS, b, seq, HEAD_DIM), bf16),
        scratch_shapes=[pltpu.VMEM((seq, HEAD_DIM), bf16)]
                       + [pltpu.VMEM((seq + 2 * RADIUS, HEAD_DIM), bf16)] * 2
                       + [pltpu.VMEM((np_, 4, QBLK, KBLK), f32)]
                       + [pltpu.VMEM((seq, HEAD_DIM), f32)] * 6
                       + [pltpu.VMEM((TILE_GROUP, QBLK, KBLK), f32),
                          pltpu.VMEM((TILE_GROUP, QBLK, KBLK), bf16),
                          pltpu.VMEM((TILE_GROUP, QBLK, HEAD_DIM), f32),
                          pltpu.VMEM((TILE_GROUP, QBLK, HEAD_DIM), f32)],
        compiler_params=_params("arbitrary", "arbitrary"),
        name="attn",
    )(*[view for view in views for _ in firsts], bias)


HALO = BF16_ROWS


def _silu(x):
    h = 0.5 * x
    return h + h * jnp.tanh(h)


def _softplus(x):
    return jnp.maximum(x, 0.0) + jnp.log1p(jnp.exp(-jnp.abs(x)))


def _ssd_kernel(z_ref, x_ref, bm_ref, cm_ref, dt_ref, cwx_ref, cbx_ref, cwb_ref, cbb_ref, cwc_ref, cbc_ref,
                hp_ref, dskip_ref, ng_ref, out_ref,
                xs_ref, bs_ref, cs_ref, bt_ref, y_ref, acs_ref, rows_ref, decx_ref, hf_ref, hb_ref,
                *stage_refs, seq):
    nc = seq // CHUNK
    g = pl.program_id(1)
    half = CONV_WIDTH // 2
    nh = HEADS_PER_GROUP
    n_slabs = len(stage_refs) // 2

    lane = lax.broadcasted_iota(jnp.int32, (CHUNK, LANES), 1)
    left = lane < SSM_HEAD_DIM
    mask_lo = jnp.where(left, 1.0, 0.0).astype(bf16)
    mask_hi = jnp.where(left, 0.0, 1.0).astype(bf16)
    li = lax.broadcasted_iota(jnp.int32, (CHUNK, CHUNK), 0)
    si = lax.broadcasted_iota(jnp.int32, (CHUNK, CHUNK), 1)
    lower = li >= si
    upper = si >= li
    tril = lower.astype(f32)
    dt_bias = hp_ref[0, 0:1, :]
    a_row = -jnp.exp(hp_ref[0, 1:2, :]) * LOG2E
    shift = lax.rem(LANES - 2 * nh * g, LANES)
    head_of_col = lax.broadcasted_iota(jnp.int32, (LANES, GROUP_WIDTH), 1) // SSM_HEAD_DIM
    head_lane = lax.broadcasted_iota(jnp.int32, (LANES, GROUP_WIDTH), 0)
    spread_f = (head_lane == head_of_col).astype(bf16)
    spread_b = (head_lane == head_of_col + nh).astype(bf16)
    tril_bf = tril.astype(bf16)

    def split3(x):
        hi = x.astype(bf16)
        rest = x - hi.astype(f32)
        mid = rest.astype(bf16)
        return hi, mid, (rest - mid.astype(f32)).astype(bf16)

    def dot3(lhs, rhs):
        return sum(jnp.dot(a, b, preferred_element_type=f32) for a in lhs for b in rhs)

    half_rows = CHUNK // 2

    def prepare(i, carry):
        r0 = pl.multiple_of(i * CHUNK, CHUNK)
        r = pl.ds(r0, CHUNK)
        lo = pl.multiple_of(jnp.maximum(r0 - HALO, 0), HALO)
        hi = pl.multiple_of(jnp.minimum(r0 + CHUNK, seq - HALO), HALO)
        has_lo = jnp.where(i > 0, 1.0, 0.0)
        has_hi = jnp.where(i < nc - 1, 1.0, 0.0)
        slab = 0
        for src, w_ref, b_ref, dst in ((x_ref, cwx_ref, cbx_ref, xs_ref),
                                       (bm_ref, cwb_ref, cbb_ref, bs_ref),
                                       (cm_ref, cwc_ref, cbc_ref, cs_ref)):
            for blk in range(src.shape[-1] // LANES):
                cols = slice(blk * LANES, (blk + 1) * LANES)
                st, ost = stage_refs[slab], stage_refs[n_slabs + slab]
                st[0:HALO, :] = src[0, pl.ds(lo, HALO), cols].astype(f32) * has_lo
                st[HALO:HALO + CHUNK, :] = src[0, r, cols].astype(f32)
                st[HALO + CHUNK:2 * HALO + CHUNK, :] = src[0, pl.ds(hi, HALO), cols].astype(f32) * has_hi
                planes = [st[pl.ds(HALO - half + s, half_rows, stride=2), :]
                          for s in range(CONV_WIDTH + 1)]
                for parity in range(2):
                    acc = b_ref[:, cols]
                    for tap in range(CONV_WIDTH):
                        acc = acc + w_ref[tap:tap + 1, cols] * planes[tap + parity]
                    ost[pl.ds(parity, half_rows, stride=2), :] = _silu(acc)
                out = ost[...].astype(bf16)
                dst[r, cols] = out
                if dst is bs_ref:
                    bt_ref[r, :] = out.astype(f32).T.astype(bf16)
                slab += 1

        raw = pltpu.roll(dt_ref[0, r, :], shift, 1)
        dtv = _softplus(raw + dt_bias)
        adt = dtv * a_row
        cum = dot3((tril_bf,), split3(adt))
        total = cum[CHUNK - 1:CHUNK, :]
        acs = jnp.where(lane < nh, cum, total - cum + adt)
        acs_ref[r, :] = acs
        rows_ref[i, 0] = (acs - jnp.log2(dtv)).T[0:2 * nh, :]
        rows_ref[i, 1] = (jnp.exp2(total - acs) * dtv).T[0:2 * nh, :]
        decay = split3(jnp.broadcast_to(jnp.exp2(total), (8, LANES)))
        decx_ref[i, 0] = dot3(decay, (spread_f,))
        decx_ref[i, 1] = dot3(decay, (spread_b,))
        return carry

    prepare(jnp.int32(0), 0)

    hf_ref[...] = jnp.zeros_like(hf_ref)
    hb_ref[...] = jnp.zeros_like(hb_ref)

    def column(acs, j):
        return jnp.broadcast_to(acs[:, j:j + 1], (CHUNK, CHUNK))

    def forward(c, carry):
        r = pl.ds(pl.multiple_of(c * CHUNK, CHUNK), CHUNK)
        acs = acs_ref[r, :]
        cmat = cs_ref[r, :]
        btf = bt_ref[r, :].astype(f32)
        cb = lax.dot_general(cmat, bs_ref[r, :], (((1,), (1,)), ((), ())), preferred_element_type=f32)
        y_off = jnp.dot(cmat, hf_ref[...].astype(bf16), preferred_element_type=f32)
        for k in range(nh // 2):
            cols = slice(k * LANES, (k + 1) * LANES)
            xp = xs_ref[r, cols]
            halves = (xp * mask_lo, xp * mask_hi)
            y_diag = snew = None
            e_f = []
            for side in range(2):
                j = 2 * k + side
                a_f, a_b = column(acs, j), column(acs, nh + j)
                w = (jnp.exp2(jnp.where(lower, a_f - rows_ref[c, 0, j:j + 1, :], NEG))
                     + jnp.exp2(jnp.where(upper, a_b - rows_ref[c, 0, nh + j:nh + j + 1, :], NEG)))
                d = jnp.dot((cb * w).astype(bf16), halves[side], preferred_element_type=f32)
                inject = (btf * rows_ref[c, 1, j:j + 1, :]).astype(bf16)
                s = jnp.dot(inject, halves[side], preferred_element_type=f32)
                y_diag = d if y_diag is None else y_diag + d
                snew = s if snew is None else snew + s
                e_f.append(jnp.exp2(a_f))
            y_ref[r, cols] = (y_diag + y_off[:, cols] * jnp.where(left, e_f[0], e_f[1])
                              + dskip_ref[:, cols] * xp.astype(f32))
            hf_ref[:, cols] = hf_ref[:, cols] * decx_ref[c, 0, 0:1, cols] + snew
        return carry

    def forward_and_prepare(c, carry):
        forward(c, carry)
        return prepare(jnp.minimum(c + 1, nc - 1), carry)

    lax.fori_loop(0, nc, forward_and_prepare, 0)

    def backward(i, carry):
        c = nc - 1 - i
        r = pl.ds(pl.multiple_of(c * CHUNK, CHUNK), CHUNK)
        acs = acs_ref[r, :]
        btf = bt_ref[r, :].astype(f32)
        y_off = jnp.dot(cs_ref[r, :], hb_ref[...].astype(bf16), preferred_element_type=f32)
        ys = []
        for k in range(nh // 2):
            cols = slice(k * LANES, (k + 1) * LANES)
            xp = xs_ref[r, cols]
            halves = (xp * mask_lo, xp * mask_hi)
            snew = None
            e_b = []
            for side in range(2):
                jb = nh + 2 * k + side
                inject = (btf * rows_ref[c, 1, jb:jb + 1, :]).astype(bf16)
                s = jnp.dot(inject, halves[side], preferred_element_type=f32)
                snew = s if snew is None else snew + s
                e_b.append(jnp.exp2(column(acs, jb)))
            ys.append(y_ref[r, cols] + y_off[:, cols] * jnp.where(left, e_b[0], e_b[1]))
            hb_ref[:, cols] = hb_ref[:, cols] * decx_ref[c, 1, 0:1, cols] + snew
        y = jnp.concatenate(ys, axis=1) * _silu(z_ref[0, r, :].astype(f32))
        y = y * lax.rsqrt(jnp.mean(y * y, axis=-1, keepdims=True) + EPS)
        out_ref[0, r, :] = (y * ng_ref[...]).astype(out_ref.dtype)
        return carry

    lax.fori_loop(0, nc, backward, 0, unroll=2)


def _ssd(proj3, dt3, conv_w, conv_b, head_params, dskip_cols, norm_g):
    b, seq, _ = proj3.shape
    gw, ns = GROUP_WIDTH, SSM_STATE
    wide = lambda off: pl.BlockSpec((1, seq, gw), lambda i, g: (i, 0, off // gw + g))
    narrow = lambda off: pl.BlockSpec((1, seq, ns), lambda i, g: (i, 0, off // ns + g))
    xoff, boff, coff = 0, SSM_WIDTH, SSM_WIDTH + SSM_GROUPS * ns
    cw = lambda off, w: pl.BlockSpec((CONV_WIDTH, w), lambda i, g: (0, off // w + g))
    cbias = lambda off, w: pl.BlockSpec((1, w), lambda i, g: (0, off // w + g))
    return pl.pallas_call(
        functools.partial(_ssd_kernel, seq=seq),
        grid=(b, SSM_GROUPS),
        in_specs=[wide(COL_ZS), wide(COL_X), narrow(COL_B), narrow(COL_C),
                  pl.BlockSpec((1, seq, LANES), lambda i, g: (i, 0, 0)),
                  cw(xoff, gw), cbias(xoff, gw), cw(boff, ns), cbias(boff, ns), cw(coff, ns), cbias(coff, ns),
                  pl.BlockSpec((1, 8, LANES), lambda i, g: (g, 0, 0)),
                  pl.BlockSpec((1, gw), lambda i, g: (0, g)),
                  pl.BlockSpec((1, gw), lambda i, g: (0, g))],
        out_specs=pl.BlockSpec((1, seq, gw), lambda i, g: (i, 0, g)),
        out_shape=jax.ShapeDtypeStruct((b, seq, SSM_WIDTH), bf16),
        scratch_shapes=[pltpu.VMEM((seq, gw), bf16),
                        pltpu.VMEM((seq, ns), bf16),
                        pltpu.VMEM((seq, ns), bf16),
                        pltpu.VMEM((seq, ns), bf16),
                        pltpu.VMEM((seq, gw), f32),
                        pltpu.VMEM((seq, LANES), f32),
                        pltpu.VMEM((seq // CHUNK, 2, 2 * HEADS_PER_GROUP, CHUNK), f32),
                        pltpu.VMEM((seq // CHUNK, 2, 8, gw), f32),
                        pltpu.VMEM((ns, gw), f32),
                        pltpu.VMEM((ns, gw), f32),
                        ]
                       + [pltpu.VMEM((CHUNK + 2 * HALO, LANES), f32)] * ((gw + 2 * ns) // LANES)
                       + [pltpu.VMEM((CHUNK, LANES), f32)] * ((gw + 2 * ns) // LANES),
        compiler_params=_params("arbitrary", "arbitrary"),
        name="ssd",
    )(proj3, proj3, proj3, proj3, dt3, conv_w, conv_b, conv_w, conv_b, conv_w, conv_b,
      head_params, dskip_cols, norm_g)


OUTPROJ_TM = 512


def _outproj_kernel(o_ref, z_ref, s_ref, x_ref, w_ref, ga_ref, gp_ref, y_ref):
    o = jnp.concatenate([o_ref[h] for h in range(ATTN_HEADS)], axis=1).astype(f32)
    a = o * lax.rsqrt(jnp.mean(o * o, axis=-1, keepdims=True) + EPS) * ga_ref[...]
    a = a * _silu(z_ref[...].astype(f32))
    mix = jnp.dot(a.astype(bf16), w_ref[0:ATTN_WIDTH, :], preferred_element_type=f32)
    mix = mix + jnp.dot(s_ref[...], w_ref[ATTN_WIDTH:MIX_WIDTH, :], preferred_element_type=f32)
    mix = mix * lax.rsqrt(jnp.mean(mix * mix, axis=-1, keepdims=True) + EPS) * gp_ref[...]
    y_ref[...] = x_ref[...] + mix


def _outproj(o3, rest2, s2, x2, w_out, g_attn, g_post):
    m = x2.shape[0]
    tm = OUTPROJ_TM
    row = lambda width, blk: pl.BlockSpec((tm, width), lambda i: (i, blk))
    const = lambda shape: pl.BlockSpec(shape, lambda i: (0, 0))
    return pl.pallas_call(
        _outproj_kernel,
        grid=(m // tm,),
        in_specs=[pl.BlockSpec((ATTN_HEADS, tm, HEAD_DIM), lambda i: (0, i, 0)),
                  row(ATTN_WIDTH, COL_ZA // ATTN_WIDTH), row(SSM_WIDTH, 0), row(D_MODEL, 0),
                  pl.BlockSpec((MIX_WIDTH, D_MODEL), lambda i: (0, 0), pipeline_mode=pl.Buffered(1)),
                  const((1, ATTN_WIDTH)), const((1, D_MODEL))],
        out_specs=row(D_MODEL, 0),
        out_shape=jax.ShapeDtypeStruct((m, D_MODEL), f32),
        compiler_params=_params("arbitrary"),
        name="outproj",
    )(o3, rest2, s2, x2, w_out, g_attn, g_post)


def _group_head_rows(fwd, bwd):
    rows = jnp.concatenate([fwd.reshape(SSM_GROUPS, HEADS_PER_GROUP), bwd.reshape(SSM_GROUPS, HEADS_PER_GROUP)],
                           axis=1)
    return jnp.pad(rows, ((0, 0), (0, LANES - 2 * HEADS_PER_GROUP)))


def _layer(x, bias, pre_g, w_main, w_dt, attn_g, conv_w, conv_b, head_params, dskip_cols, ssm_g, w_out, post_g):
    b, seq, _ = x.shape
    x2 = x.reshape(b * seq, D_MODEL)
    qkv, rest2, dt2 = _inproj(x2, pre_g, w_main, w_dt)
    o = _attention(qkv.reshape(3 * ATTN_HEADS, b, seq, HEAD_DIM), bias)
    s = _ssd(rest2.reshape(b, seq, REST_WIDTH), dt2.reshape(b, seq, LANES), conv_w, conv_b, head_params,
             dskip_cols, ssm_g)
    y = _outproj(o.reshape(ATTN_HEADS, b * seq, HEAD_DIM), rest2, s.reshape(b * seq, SSM_WIDTH), x2, w_out,
                 attn_g, post_g)
    return y.reshape(b, seq, D_MODEL)


def kernel(x_prompt, x_sample, pre_norm_g, w_in, rel_bias_table, attn_norm_g, conv_w, conv_b, dt_bias_fwd,
           dt_bias_bwd, a_log_fwd, a_log_bwd, d_skip, ssm_norm_g, w_out, post_norm_g):
    depth = w_in.shape[0]
    bias = _relbias(rel_bias_table)
    y_prompt, y_sample = x_prompt, x_sample
    for i in range(depth):
        w_main = w_in[i, :, :PROJ_MAIN].astype(bf16)
        w_dt = w_in[i, :, PROJ_MAIN:]
        w_dt = jnp.concatenate([w_dt[:, :SSM_HEADS].reshape(D_MODEL, SSM_GROUPS, HEADS_PER_GROUP),
                                w_dt[:, SSM_HEADS:].reshape(D_MODEL, SSM_GROUPS, HEADS_PER_GROUP)], axis=2)
        w_dt = jnp.pad(w_dt.reshape(D_MODEL, 2 * SSM_HEADS), ((0, 0), (0, LANES - 2 * SSM_HEADS))).astype(bf16)
        head_params = jnp.stack([_group_head_rows(dt_bias_fwd[i], dt_bias_bwd[i]),
                                 _group_head_rows(a_log_fwd[i], a_log_bwd[i])], axis=1)
        head_params = jnp.pad(head_params, ((0, 0), (0, 6), (0, 0)))
        args = (bias, pre_norm_g[i].reshape(1, D_MODEL), w_main, w_dt, attn_norm_g[i].reshape(1, ATTN_WIDTH),
                conv_w[i], conv_b[i].reshape(1, CONV_CH), head_params,
                jnp.repeat(d_skip[i], SSM_HEAD_DIM).reshape(1, SSM_WIDTH), ssm_norm_g[i].reshape(1, SSM_WIDTH),
                w_out[i].astype(bf16), post_norm_g[i].reshape(1, D_MODEL))
        y_prompt = _layer(y_prompt, *args)
        y_sample = _layer(y_sample, *args)
    return (y_prompt, y_sample)
```

```python
import functools
import math

import numpy as np
import jax
import jax.numpy as jnp
from jax import lax
from jax.experimental import pallas as pl
from jax.experimental.pallas import tpu as pltpu

D_MODEL = 2048
ATTN_HEADS = 16
HEAD_DIM = 128
ATTN_WIDTH = ATTN_HEADS * HEAD_DIM
DILATIONS = (1, 4, 16)
RADIUS = 64
N_BUCKETS = 32
BUCKET_MAX_DIST = 1024
SSM_HEADS = 32
SSM_HEAD_DIM = 64
SSM_WIDTH = SSM_HEADS * SSM_HEAD_DIM
SSM_GROUPS = 4
HEADS_PER_GROUP = SSM_HEADS // SSM_GROUPS
GROUP_WIDTH = SSM_WIDTH // SSM_GROUPS
SSM_STATE = 128
CONV_WIDTH = 5
CHUNK = 128
CONV_CH = SSM_WIDTH + 2 * SSM_GROUPS * SSM_STATE
MIX_WIDTH = ATTN_WIDTH + SSM_WIDTH
PROJ_MAIN = 4 * ATTN_WIDTH + SSM_WIDTH + CONV_CH
EPS = 1e-6

LANES = 128
BF16_ROWS = 16
QBLK = 128
KBLK = QBLK + 2 * RADIUS
TILE_GROUP = 16
LOG2E = math.log2(math.e)
NEG = -1e30
VMEM_LIMIT = 56 * 1024 * 1024

QKV_WIDTH = 3 * ATTN_WIDTH
REST_WIDTH = PROJ_MAIN - QKV_WIDTH
COL_ZA = 0
COL_ZS = ATTN_WIDTH
COL_X = COL_ZS + SSM_WIDTH
COL_B = COL_X + SSM_WIDTH
COL_C = COL_B + SSM_GROUPS * SSM_STATE

f32 = jnp.float32
bf16 = jnp.bfloat16


def _params(*sem):
    return pltpu.CompilerParams(dimension_semantics=sem, vmem_limit_bytes=VMEM_LIMIT)


def _bucket_tiles():
    qi = np.arange(QBLK)[:, None]
    kj = np.arange(KBLK)[None, :]
    rel = kj - RADIUS - qi
    nb = N_BUCKETS // 2
    max_exact = nb // 2
    tiles = []
    for dil in DILATIONS:
        r = rel * dil
        n = np.abs(r)
        nf = np.maximum(n, 1).astype(np.float32)
        large = max_exact + (np.log(nf / np.float32(max_exact)) / np.float32(math.log(BUCKET_MAX_DIST / max_exact))
                             * np.float32(nb - max_exact)).astype(np.int32)
        large = np.minimum(large, nb - 1)
        bucket = np.where(r > 0, nb, 0) + np.where(n < max_exact, n, large)
        tiles.append(np.where(np.abs(rel) <= RADIUS, bucket, -1))
    return np.stack(tiles).astype(np.int32)


def _relbias_kernel(tab_ref, idx_ref, out_ref):
    h = pl.program_id(0)
    for p in range(len(DILATIONS)):
        idx = idx_ref[p]
        acc = jnp.full((QBLK, KBLK), NEG, f32)
        for b in range(N_BUCKETS):
            acc = jnp.where(idx == b, tab_ref[b, h] * LOG2E, acc)
        out_ref[p, 0] = acc


def _relbias(rel_table):
    idx = jnp.asarray(_bucket_tiles())
    np_ = len(DILATIONS)
    return pl.pallas_call(
        _relbias_kernel,
        grid=(ATTN_HEADS,),
        in_specs=[pl.BlockSpec(memory_space=pltpu.SMEM),
                  pl.BlockSpec((np_, QBLK, KBLK), lambda h: (0, 0, 0))],
        out_specs=pl.BlockSpec((np_, 1, QBLK, KBLK), lambda h: (0, h, 0, 0)),
        out_shape=jax.ShapeDtypeStruct((np_, ATTN_HEADS, QBLK, KBLK), f32),
        compiler_params=_params("arbitrary"),
        name="relbias",
    )(rel_table, idx)


INPROJ_TM = 1024
INPROJ_TN = 1024
NORM_ROWS = 128


QKV_TILES = QKV_WIDTH // INPROJ_TN
HEADS_PER_TILE = INPROJ_TN // HEAD_DIM


NORM_STEPS = INPROJ_TM // NORM_ROWS


def _inproj_kernel(x_ref, g_ref, w_ref, wdt_ref, qkv_ref, rest_ref, dt_ref, h_ref):
    i = pl.program_id(0)
    j = pl.program_id(1)
    cur = lax.rem(i, 2)

    def normalise(rows, slot):
        x = x_ref[rows, :]
        y = x * lax.rsqrt(jnp.mean(x * x, axis=-1, keepdims=True) + EPS)
        h_ref[slot, rows, :] = (y * g_ref[...]).astype(bf16)

    def matmul(with_norm):
        tile = jnp.dot(h_ref[cur], w_ref[...], preferred_element_type=f32).astype(bf16)
        if with_norm:
            normalise(pl.ds(pl.multiple_of((j - 1) * NORM_ROWS, NORM_ROWS), NORM_ROWS), 1 - cur)
        return tile

    def store_qkv(tile):
        for hh in range(HEADS_PER_TILE):
            qkv_ref[hh] = tile[:, hh * HEAD_DIM:(hh + 1) * HEAD_DIM]

    @pl.when((i == 0) & (j == 0))
    def _():
        def rows(c, carry):
            normalise(pl.ds(pl.multiple_of(c * NORM_ROWS, NORM_ROWS), NORM_ROWS), 0)
            return carry

        lax.fori_loop(0, NORM_STEPS, rows, 0)

    @pl.when(j == 0)
    def _():
        dt_ref[...] = jnp.dot(h_ref[cur], wdt_ref[...], preferred_element_type=f32)
        store_qkv(matmul(False))

    @pl.when((j >= 1) & (j < QKV_TILES))
    def _():
        store_qkv(matmul(True))

    @pl.when((j >= QKV_TILES) & (j <= NORM_STEPS))
    def _():
        rest_ref[...] = matmul(True)

    @pl.when(j > NORM_STEPS)
    def _():
        rest_ref[...] = matmul(False)


def _inproj(x2, g, w_main, w_dt):
    m = x2.shape[0]
    last = m // INPROJ_TM - 1
    return pl.pallas_call(
        _inproj_kernel,
        grid=(m // INPROJ_TM, PROJ_MAIN // INPROJ_TN),
        in_specs=[pl.BlockSpec((INPROJ_TM, D_MODEL),
                               lambda i, j: (jnp.where((i == 0) & (j == 0), 0, jnp.minimum(i + 1, last)), 0)),
                  pl.BlockSpec((1, D_MODEL), lambda i, j: (0, 0)),
                  pl.BlockSpec((D_MODEL, INPROJ_TN), lambda i, j: (0, j)),
                  pl.BlockSpec((D_MODEL, LANES), lambda i, j: (0, 0))],
        out_specs=[pl.BlockSpec((HEADS_PER_TILE, INPROJ_TM, HEAD_DIM),
                                lambda i, j: (jnp.minimum(j, QKV_TILES - 1), i, 0)),
                   pl.BlockSpec((INPROJ_TM, INPROJ_TN), lambda i, j: (i, jnp.maximum(j - QKV_TILES, 0))),
                   pl.BlockSpec((INPROJ_TM, LANES), lambda i, j: (i, 0))],
        out_shape=[jax.ShapeDtypeStruct((3 * ATTN_HEADS, m, HEAD_DIM), bf16),
                   jax.ShapeDtypeStruct((m, REST_WIDTH), bf16),
                   jax.ShapeDtypeStruct((m, LANES), f32)],
        scratch_shapes=[pltpu.VMEM((2, INPROJ_TM, D_MODEL), bf16)],
        compiler_params=_params("arbitrary", "arbitrary"),
        name="inproj",
    )(x2, g, w_main, w_dt)


SCORE_SCALE = LOG2E / math.sqrt(HEAD_DIM)
CLASS_STEP = 4


def _attn_kernel(q_ref, k_ref, v_ref, bias_ref, o_ref,
                 tok_ref, cls4_ref, qd_ref, kd_ref, vd_ref, bias4_ref,
                 acc4_ref, m4_ref, l4_ref, acc1_ref, m1_ref, l1_ref,
                 s_buf, p_buf, mn_buf, rs_buf, *, seq):
    nblk = seq // QBLK
    n4 = seq // 4
    n16 = seq // 16
    srcs = (q_ref, k_ref, v_ref)
    dsts = ((qd_ref, 0), (kd_ref, RADIUS), (vd_ref, RADIUS))

    @pl.when(pl.program_id(1) == 0)
    def _():
        col = lax.broadcasted_iota(jnp.int32, (QBLK, KBLK), 1)
        before = col < RADIUS
        after = col >= QBLK + RADIUS
        for p in range(len(DILATIONS)):
            b = bias_ref[p, 0]
            bias4_ref[p, 0] = b
            bias4_ref[p, 1] = jnp.where(before, NEG, b)
            bias4_ref[p, 2] = jnp.where(after, NEG, b)
            bias4_ref[p, 3] = jnp.where(before | after, NEG, b)
        pad = jnp.zeros((RADIUS, HEAD_DIM), bf16)
        for ref in (kd_ref, vd_ref):
            ref[0:RADIUS, :] = pad
            ref[RADIUS + seq:2 * RADIUS + seq, :] = pad

    def stage(i, carry):
        r = pl.ds(pl.multiple_of(i * QBLK, QBLK), QBLK)
        for x, src in enumerate(srcs):
            tok_ref[x, r, :] = src[r, :].astype(f32)
        return carry

    lax.fori_loop(0, nblk, stage, 0)
    for x in range(3):
        for c4 in range(CLASS_STEP):
            cls4_ref[x, c4 * n4:(c4 + 1) * n4, :] = tok_ref[x, pl.ds(c4, n4, stride=CLASS_STEP), :]

    def tile_rows(t):
        return pl.ds(pl.multiple_of(t * QBLK, QBLK), QBLK)

    def scores(t, u, p, tiles_per_class):
        r0 = pl.multiple_of(t * QBLK, QBLK)
        q = (q_ref if DILATIONS[p] == 1 else qd_ref)[pl.ds(r0, QBLK), :]
        k = kd_ref[pl.ds(r0, KBLK), :]
        jj = lax.rem(t, tiles_per_class)
        variant = (jj == 0).astype(jnp.int32) + 2 * (jj == tiles_per_class - 1).astype(jnp.int32)
        s = lax.dot_general(q, k, (((1,), (1,)), ((), ())), preferred_element_type=f32)
        s_buf[u] = s + bias4_ref[p, variant]

    def softmax(t, u, prev):
        s = s_buf[u]
        top = jnp.max(s, axis=-1, keepdims=True)
        if prev is None:
            mn = jnp.broadcast_to(top, (QBLK, HEAD_DIM))
        else:
            mn = jnp.maximum(prev[1][tile_rows(t), :], top)
        e_lo = jnp.exp2(s[:, :HEAD_DIM] - mn)
        e_hi = jnp.exp2(s[:, HEAD_DIM:] - mn)
        p_buf[u, :, :HEAD_DIM] = e_lo.astype(bf16)
        p_buf[u, :, HEAD_DIM:] = e_hi.astype(bf16)
        mn_buf[u] = mn
        rs_buf[u] = jnp.broadcast_to(jnp.sum(e_lo + e_hi, axis=-1, keepdims=True), (QBLK, HEAD_DIM))

    def update(t, u, prev, nxt, dst_rows):
        r = tile_rows(t)
        v = vd_ref[pl.ds(pl.multiple_of(t * QBLK, QBLK), KBLK), :]
        acc = jnp.dot(p_buf[u], v, preferred_element_type=f32)
        mn = mn_buf[u]
        l = rs_buf[u]
        if prev is not None:
            alpha = jnp.exp2(prev[1][r, :] - mn)
            l = alpha * prev[2][r, :] + l
            acc = alpha * prev[0][r, :] + acc
        if nxt is None:
            o_ref[r, :] = (acc / l).astype(o_ref.dtype)
        else:
            nxt[0][dst_rows, :] = acc
            nxt[1][dst_rows, :] = mn
            nxt[2][dst_rows, :] = l

    def run_pattern(p, prev, nxt, dst_rows_fn):
        tiles_per_class = seq // DILATIONS[p] // QBLK

        def group(i, carry):
            tiles = [i * TILE_GROUP + u for u in range(TILE_GROUP)]
            for u, t in enumerate(tiles):
                scores(t, u, p, tiles_per_class)
            for u, t in enumerate(tiles):
                softmax(t, u, prev)
            for u, t in enumerate(tiles):
                dst = None if nxt is None else dst_rows_fn(t // tiles_per_class, lax.rem(t, tiles_per_class))
                update(t, u, prev, nxt, dst)
            return carry

        lax.fori_loop(0, nblk // TILE_GROUP, group, 0)

    for c16 in range(16):
        rows = pl.ds((c16 % 4) * n4 + c16 // 4, n16, stride=CLASS_STEP)
        for x, (dst, off) in enumerate(dsts):
            dst[off + c16 * n16:off + (c16 + 1) * n16, :] = cls4_ref[x, rows, :].astype(bf16)
    state4 = (acc4_ref, m4_ref, l4_ref)
    state1 = (acc1_ref, m1_ref, l1_ref)
    run_pattern(2, None, state4,
                lambda c16, jj: pl.ds(lax.rem(c16, 4) * n4 + c16 // 4 + jj * (QBLK * CLASS_STEP), QBLK,
                                      stride=CLASS_STEP))

    def copy_cls4(i, carry):
        r0 = pl.multiple_of(i * QBLK, QBLK)
        for x, (dst, off) in enumerate(dsts):
            dst[pl.ds(r0 + off, QBLK), :] = cls4_ref[x, pl.ds(r0, QBLK), :].astype(bf16)
        return carry

    lax.fori_loop(0, nblk, copy_cls4, 0)
    run_pattern(1, state4, state1,
                lambda c4, jj: pl.ds(c4 + jj * (QBLK * CLASS_STEP), QBLK, stride=CLASS_STEP))

    def copy_tok(i, carry):
        r0 = pl.multiple_of(i * QBLK, QBLK)
        for src, (dst, off) in zip(srcs[1:], dsts[1:]):
            dst[pl.ds(r0 + off, QBLK), :] = src[pl.ds(r0, QBLK), :]
        return carry

    lax.fori_loop(0, nblk, copy_tok, 0)
    run_pattern(0, state1, None, None)


def _attention(qkv4, bias):
    _, b, seq, _ = qkv4.shape
    np_ = len(DILATIONS)
    head = lambda first: pl.BlockSpec((None, None, seq, HEAD_DIM), lambda h, i: (first + h, i, 0, 0))
    return pl.pallas_call(
        functools.partial(_attn_kernel, seq=seq),
        grid=(ATTN_HEADS, b),
        in_specs=[head(0), head(ATTN_HEADS), head(2 * ATTN_HEADS),
                  pl.BlockSpec((np_, 1, QBLK, KBLK), lambda h, i: (0, h, 0, 0))],
        out_specs=head(0),
        out_shape=jax.ShapeDtypeStruct((ATTN_HEADS, b, seq, HEAD_DIM), bf16),
        scratch_shapes=[pltpu.VMEM((3, seq, HEAD_DIM), f32)] * 2
                       + [pltpu.VMEM((seq, HEAD_DIM), bf16)]
                       + [pltpu.VMEM((seq + 2 * RADIUS, HEAD_DIM), bf16)] * 2
                       + [pltpu.VMEM((np_, 4, QBLK, KBLK), f32)]
                       + [pltpu.VMEM((seq, HEAD_DIM), f32)] * 6
                       + [pltpu.VMEM((TILE_GROUP, QBLK, KBLK), f32),
                          pltpu.VMEM((TILE_GROUP, QBLK, KBLK), bf16),
                          pltpu.VMEM((TILE_GROUP, QBLK, HEAD_DIM), f32),
                          pltpu.VMEM((TILE_GROUP, QBLK, HEAD_DIM), f32)],
        compiler_params=_params("arbitrary", "arbitrary"),
        name="attn",
    )(qkv4, qkv4, qkv4, bias)


HALO = BF16_ROWS


def _silu(x):
    h = 0.5 * x
    return h + h * jnp.tanh(h)


def _softplus(x):
    return jnp.maximum(x, 0.0) + jnp.log1p(jnp.exp(-jnp.abs(x)))


def _ssd_kernel(z_ref, x_ref, bm_ref, cm_ref, dt_ref, cwx_ref, cbx_ref, cwb_ref, cbb_ref, cwc_ref, cbc_ref,
                hp_ref, dskip_ref, ng_ref, out_ref,
                xs_ref, bs_ref, cs_ref, bt_ref, y_ref, acs_ref, rows_ref, decx_ref, hf_ref, hb_ref,
                *stage_refs, seq):
    nc = seq // CHUNK
    g = pl.program_id(1)
    half = CONV_WIDTH // 2
    nh = HEADS_PER_GROUP
    n_slabs = len(stage_refs) // 2

    lane = lax.broadcasted_iota(jnp.int32, (CHUNK, LANES), 1)
    left = lane < SSM_HEAD_DIM
    mask_lo = jnp.where(left, 1.0, 0.0).astype(bf16)
    mask_hi = jnp.where(left, 0.0, 1.0).astype(bf16)
    li = lax.broadcasted_iota(jnp.int32, (CHUNK, CHUNK), 0)
    si = lax.broadcasted_iota(jnp.int32, (CHUNK, CHUNK), 1)
    lower = li >= si
    upper = si >= li
    tril = lower.astype(f32)
    dt_bias = hp_ref[0, 0:1, :]
    a_row = -jnp.exp(hp_ref[0, 1:2, :]) * LOG2E
    shift = lax.rem(LANES - 2 * nh * g, LANES)
    head_of_col = lax.broadcasted_iota(jnp.int32, (LANES, GROUP_WIDTH), 1) // SSM_HEAD_DIM
    head_lane = lax.broadcasted_iota(jnp.int32, (LANES, GROUP_WIDTH), 0)
    spread_f = (head_lane == head_of_col).astype(bf16)
    spread_b = (head_lane == head_of_col + nh).astype(bf16)
    tril_bf = tril.astype(bf16)

    def split3(x):
        hi = x.astype(bf16)
        rest = x - hi.astype(f32)
        mid = rest.astype(bf16)
        return hi, mid, (rest - mid.astype(f32)).astype(bf16)

    def dot3(lhs, rhs):
        return sum(jnp.dot(a, b, preferred_element_type=f32) for a in lhs for b in rhs)

    half_rows = CHUNK // 2

    def prepare(i, carry):
        r0 = pl.multiple_of(i * CHUNK, CHUNK)
        r = pl.ds(r0, CHUNK)
        lo = pl.multiple_of(jnp.maximum(r0 - HALO, 0), HALO)
        hi = pl.multiple_of(jnp.minimum(r0 + CHUNK, seq - HALO), HALO)
        has_lo = jnp.where(i > 0, 1.0, 0.0)
        has_hi = jnp.where(i < nc - 1, 1.0, 0.0)
        slab = 0
        for src, w_ref, b_ref, dst in ((x_ref, cwx_ref, cbx_ref, xs_ref),
                                       (bm_ref, cwb_ref, cbb_ref, bs_ref),
                                       (cm_ref, cwc_ref, cbc_ref, cs_ref)):
            for blk in range(src.shape[-1] // LANES):
                cols = slice(blk * LANES, (blk + 1) * LANES)
                st, ost = stage_refs[slab], stage_refs[n_slabs + slab]
                st[0:HALO, :] = src[0, pl.ds(lo, HALO), cols].astype(f32) * has_lo
                st[HALO:HALO + CHUNK, :] = src[0, r, cols].astype(f32)
                st[HALO + CHUNK:2 * HALO + CHUNK, :] = src[0, pl.ds(hi, HALO), cols].astype(f32) * has_hi
                planes = [st[pl.ds(HALO - half + s, half_rows, stride=2), :]
                          for s in range(CONV_WIDTH + 1)]
                for parity in range(2):
                    acc = b_ref[:, cols]
                    for tap in range(CONV_WIDTH):
                        acc = acc + w_ref[tap:tap + 1, cols] * planes[tap + parity]
                    ost[pl.ds(parity, half_rows, stride=2), :] = _silu(acc)
                out = ost[...].astype(bf16)
                dst[r, cols] = out
                if dst is bs_ref:
                    bt_ref[r, :] = out.astype(f32).T.astype(bf16)
                slab += 1

        raw = pltpu.roll(dt_ref[0, r, :], shift, 1)
        dtv = _softplus(raw + dt_bias)
        adt = dtv * a_row
        cum = dot3((tril_bf,), split3(adt))
        total = cum[CHUNK - 1:CHUNK, :]
        acs = jnp.where(lane < nh, cum, total - cum + adt)
        acs_ref[r, :] = acs
        rows_ref[i, 0] = (acs - jnp.log2(dtv)).T[0:2 * nh, :]
        rows_ref[i, 1] = (jnp.exp2(total - acs) * dtv).T[0:2 * nh, :]
        decay = split3(jnp.broadcast_to(jnp.exp2(total), (8, LANES)))
        decx_ref[i, 0] = dot3(decay, (spread_f,))
        decx_ref[i, 1] = dot3(decay, (spread_b,))
        return carry

    prepare(jnp.int32(0), 0)

    hf_ref[...] = jnp.zeros_like(hf_ref)
    hb_ref[...] = jnp.zeros_like(hb_ref)

    def column(acs, j):
        return jnp.broadcast_to(acs[:, j:j + 1], (CHUNK, CHUNK))

    def forward(c, carry):
        r = pl.ds(pl.multiple_of(c * CHUNK, CHUNK), CHUNK)
        acs = acs_ref[r, :]
        cmat = cs_ref[r, :]
        btf = bt_ref[r, :].astype(f32)
        cb = lax.dot_general(cmat, bs_ref[r, :], (((1,), (1,)), ((), ())), preferred_element_type=f32)
        y_off = jnp.dot(cmat, hf_ref[...].astype(bf16), preferred_element_type=f32)
        for k in range(nh // 2):
            cols = slice(k * LANES, (k + 1) * LANES)
            xp = xs_ref[r, cols]
            halves = (xp * mask_lo, xp * mask_hi)
            y_diag = snew = None
            e_f = []
            for side in range(2):
                j = 2 * k + side
                a_f, a_b = column(acs, j), column(acs, nh + j)
                w = (jnp.exp2(jnp.where(lower, a_f - rows_ref[c, 0, j:j + 1, :], NEG))
                     + jnp.exp2(jnp.where(upper, a_b - rows_ref[c, 0, nh + j:nh + j + 1, :], NEG)))
                d = jnp.dot((cb * w).astype(bf16), halves[side], preferred_element_type=f32)
                inject = (btf * rows_ref[c, 1, j:j + 1, :]).astype(bf16)
                s = jnp.dot(inject, halves[side], preferred_element_type=f32)
                y_diag = d if y_diag is None else y_diag + d
                snew = s if snew is None else snew + s
                e_f.append(a_f)
            y_ref[r, cols] = (y_diag + y_off[:, cols] * jnp.exp2(jnp.where(left, e_f[0], e_f[1]))
                              + dskip_ref[:, cols] * xp.astype(f32))
            hf_ref[:, cols] = hf_ref[:, cols] * decx_ref[c, 0, 0:1, cols] + snew
        return carry

    def forward_and_prepare(c, carry):
        forward(c, carry)
        return prepare(jnp.minimum(c + 1, nc - 1), carry)

    lax.fori_loop(0, nc, forward_and_prepare, 0)

    def backward(i, carry):
        c = nc - 1 - i
        r = pl.ds(pl.multiple_of(c * CHUNK, CHUNK), CHUNK)
        acs = acs_ref[r, :]
        btf = bt_ref[r, :].astype(f32)
        y_off = jnp.dot(cs_ref[r, :], hb_ref[...].astype(bf16), preferred_element_type=f32)
        ys = []
        for k in range(nh // 2):
            cols = slice(k * LANES, (k + 1) * LANES)
            xp = xs_ref[r, cols]
            halves = (xp * mask_lo, xp * mask_hi)
            snew = None
            e_b = []
            for side in range(2):
                jb = nh + 2 * k + side
                inject = (btf * rows_ref[c, 1, jb:jb + 1, :]).astype(bf16)
                s = jnp.dot(inject, halves[side], preferred_element_type=f32)
                snew = s if snew is None else snew + s
                e_b.append(column(acs, jb))
            ys.append(y_ref[r, cols] + y_off[:, cols] * jnp.exp2(jnp.where(left, e_b[0], e_b[1])))
            hb_ref[:, cols] = hb_ref[:, cols] * decx_ref[c, 1, 0:1, cols] + snew
        y = jnp.concatenate(ys, axis=1) * _silu(z_ref[0, r, :].astype(f32))
        y = y * lax.rsqrt(jnp.mean(y * y, axis=-1, keepdims=True) + EPS)
        out_ref[0, r, :] = (y * ng_ref[...]).astype(out_ref.dtype)
        return carry

    lax.fori_loop(0, nc, backward, 0, unroll=2)


def _ssd(proj3, dt3, conv_w, conv_b, head_params, dskip_cols, norm_g):
    b, seq, _ = proj3.shape
    gw, ns = GROUP_WIDTH, SSM_STATE
    wide = lambda off: pl.BlockSpec((1, seq, gw), lambda i, g: (i, 0, off // gw + g))
    narrow = lambda off: pl.BlockSpec((1, seq, ns), lambda i, g: (i, 0, off // ns + g))
    xoff, boff, coff = 0, SSM_WIDTH, SSM_WIDTH + SSM_GROUPS * ns
    cw = lambda off, w: pl.BlockSpec((CONV_WIDTH, w), lambda i, g: (0, off // w + g))
    cbias = lambda off, w: pl.BlockSpec((1, w), lambda i, g: (0, off // w + g))
    return pl.pallas_call(
        functools.partial(_ssd_kernel, seq=seq),
        grid=(b, SSM_GROUPS),
        in_specs=[wide(COL_ZS), wide(COL_X), narrow(COL_B), narrow(COL_C),
                  pl.BlockSpec((1, seq, LANES), lambda i, g: (i, 0, 0)),
                  cw(xoff, gw), cbias(xoff, gw), cw(boff, ns), cbias(boff, ns), cw(coff, ns), cbias(coff, ns),
                  pl.BlockSpec((1, 8, LANES), lambda i, g: (g, 0, 0)),
                  pl.BlockSpec((1, gw), lambda i, g: (0, g)),
                  pl.BlockSpec((1, gw), lambda i, g: (0, g))],
        out_specs=pl.BlockSpec((1, seq, gw), lambda i, g: (i, 0, g)),
        out_shape=jax.ShapeDtypeStruct((b, seq, SSM_WIDTH), bf16),
        scratch_shapes=[pltpu.VMEM((seq, gw), bf16),
                        pltpu.VMEM((seq, ns), bf16),
                        pltpu.VMEM((seq, ns), bf16),
                        pltpu.VMEM((seq, ns), bf16),
                        pltpu.VMEM((seq, gw), f32),
                        pltpu.VMEM((seq, LANES), f32),
                        pltpu.VMEM((seq // CHUNK, 2, 2 * HEADS_PER_GROUP, CHUNK), f32),
                        pltpu.VMEM((seq // CHUNK, 2, 8, gw), f32),
                        pltpu.VMEM((ns, gw), f32),
                        pltpu.VMEM((ns, gw), f32),
                        ]
                       + [pltpu.VMEM((CHUNK + 2 * HALO, LANES), f32)] * ((gw + 2 * ns) // LANES)
                       + [pltpu.VMEM((CHUNK, LANES), f32)] * ((gw + 2 * ns) // LANES),
        compiler_params=_params("arbitrary", "arbitrary"),
        name="ssd",
    )(proj3, proj3, proj3, proj3, dt3, conv_w, conv_b, conv_w, conv_b, conv_w, conv_b,
      head_params, dskip_cols, norm_g)


OUTPROJ_TM = 512


def _outproj_kernel(o_ref, z_ref, s_ref, x_ref, w_ref, ga_ref, gp_ref, y_ref):
    o = jnp.concatenate([o_ref[h] for h in range(ATTN_HEADS)], axis=1).astype(f32)
    a = o * lax.rsqrt(jnp.mean(o * o, axis=-1, keepdims=True) + EPS) * ga_ref[...]
    a = a * _silu(z_ref[...].astype(f32))
    mix = jnp.dot(a.astype(bf16), w_ref[0:ATTN_WIDTH, :], preferred_element_type=f32)
    mix = mix + jnp.dot(s_ref[...], w_ref[ATTN_WIDTH:MIX_WIDTH, :], preferred_element_type=f32)
    mix = mix * lax.rsqrt(jnp.mean(mix * mix, axis=-1, keepdims=True) + EPS) * gp_ref[...]
    y_ref[...] = x_ref[...] + mix


def _outproj(o3, rest2, s2, x2, w_out, g_attn, g_post):
    m = x2.shape[0]
    tm = OUTPROJ_TM
    row = lambda width, blk: pl.BlockSpec((tm, width), lambda i: (i, blk))
    const = lambda shape: pl.BlockSpec(shape, lambda i: (0, 0))
    return pl.pallas_call(
        _outproj_kernel,
        grid=(m // tm,),
        in_specs=[pl.BlockSpec((ATTN_HEADS, tm, HEAD_DIM), lambda i: (0, i, 0)),
                  row(ATTN_WIDTH, COL_ZA // ATTN_WIDTH), row(SSM_WIDTH, 0), row(D_MODEL, 0),
                  pl.BlockSpec((MIX_WIDTH, D_MODEL), lambda i: (0, 0), pipeline_mode=pl.Buffered(1)),
                  const((1, ATTN_WIDTH)), const((1, D_MODEL))],
        out_specs=row(D_MODEL, 0),
        out_shape=jax.ShapeDtypeStruct((m, D_MODEL), f32),
        compiler_params=_params("arbitrary"),
        name="outproj",
    )(o3, rest2, s2, x2, w_out, g_attn, g_post)


def _group_head_rows(fwd, bwd):
    rows = jnp.concatenate([fwd.reshape(SSM_GROUPS, HEADS_PER_GROUP), bwd.reshape(SSM_GROUPS, HEADS_PER_GROUP)],
                           axis=1)
    return jnp.pad(rows, ((0, 0), (0, LANES - 2 * HEADS_PER_GROUP)))


def _layer(x, bias, pre_g, w_main, w_dt, attn_g, conv_w, conv_b, head_params, dskip_cols, ssm_g, w_out, post_g):
    b, seq, _ = x.shape
    x2 = x.reshape(b * seq, D_MODEL)
    qkv, rest2, dt2 = _inproj(x2, pre_g, w_main, w_dt)
    o = _attention(qkv.reshape(3 * ATTN_HEADS, b, seq, HEAD_DIM), bias)
    s = _ssd(rest2.reshape(b, seq, REST_WIDTH), dt2.reshape(b, seq, LANES), conv_w, conv_b, head_params,
             dskip_cols, ssm_g)
    y = _outproj(o.reshape(ATTN_HEADS, b * seq, HEAD_DIM), rest2, s.reshape(b * seq, SSM_WIDTH), x2, w_out,
                 attn_g, post_g)
    return y.reshape(b, seq, D_MODEL)


def kernel(x_prompt, x_sample, pre_norm_g, w_in, rel_bias_table, attn_norm_g, conv_w, conv_b, dt_bias_fwd,
           dt_bias_bwd, a_log_fwd, a_log_bwd, d_skip, ssm_norm_g, w_out, post_norm_g):
    depth = w_in.shape[0]
    bias = _relbias(rel_bias_table)
    y_prompt, y_sample = x_prompt, x_sample
    for i in range(depth):
        col_scale = jnp.where(jnp.arange(PROJ_MAIN) < ATTN_WIDTH, SCORE_SCALE, 1.0).astype(f32)
        w_main = (w_in[i, :, :PROJ_MAIN] * col_scale).astype(bf16)
        w_dt = w_in[i, :, PROJ_MAIN:]
        w_dt = jnp.concatenate([w_dt[:, :SSM_HEADS].reshape(D_MODEL, SSM_GROUPS, HEADS_PER_GROUP),
                                w_dt[:, SSM_HEADS:].reshape(D_MODEL, SSM_GROUPS, HEADS_PER_GROUP)], axis=2)
        w_dt = jnp.pad(w_dt.reshape(D_MODEL, 2 * SSM_HEADS), ((0, 0), (0, LANES - 2 * SSM_HEADS))).astype(bf16)
        head_params = jnp.stack([_group_head_rows(dt_bias_fwd[i], dt_bias_bwd[i]),
                                 _group_head_rows(a_log_fwd[i], a_log_bwd[i])], axis=1)
        head_params = jnp.pad(head_params, ((0, 0), (0, 6), (0, 0)))
        args = (bias, pre_norm_g[i].reshape(1, D_MODEL), w_main, w_dt, attn_norm_g[i].reshape(1, ATTN_WIDTH),
                conv_w[i], conv_b[i].reshape(1, CONV_CH), head_params,
                jnp.repeat(d_skip[i], SSM_HEAD_DIM).reshape(1, SSM_WIDTH), ssm_norm_g[i].reshape(1, SSM_WIDTH),
                w_out[i].astype(bf16), post_norm_g[i].reshape(1, D_MODEL))
        y_prompt = _layer(y_prompt, *args)
        y_sample = _layer(y_sample, *args)
    return (y_prompt, y_sample)
```

```python
import functools
import math

import numpy as np
import jax
import jax.numpy as jnp
from jax import lax
from jax.experimental import pallas as pl
from jax.experimental.pallas import tpu as pltpu

D_MODEL = 2048
ATTN_HEADS = 16
HEAD_DIM = 128
ATTN_WIDTH = ATTN_HEADS * HEAD_DIM
DILATIONS = (1, 4, 16)
RADIUS = 64
N_BUCKETS = 32
BUCKET_MAX_DIST = 1024
SSM_HEADS = 32
SSM_HEAD_DIM = 64
SSM_WIDTH = SSM_HEADS * SSM_HEAD_DIM
SSM_GROUPS = 4
HEADS_PER_GROUP = SSM_HEADS // SSM_GROUPS
GROUP_WIDTH = SSM_WIDTH // SSM_GROUPS
SSM_STATE = 128
CONV_WIDTH = 5
CHUNK = 128
CONV_CH = SSM_WIDTH + 2 * SSM_GROUPS * SSM_STATE
MIX_WIDTH = ATTN_WIDTH + SSM_WIDTH
PROJ_MAIN = 4 * ATTN_WIDTH + SSM_WIDTH + CONV_CH
EPS = 1e-6

LANES = 128
BF16_ROWS = 16
QBLK = 128
KBLK = QBLK + 2 * RADIUS
TILE_GROUP = 16
LOG2E = math.log2(math.e)
NEG = -1e30
VMEM_LIMIT = 56 * 1024 * 1024

QKV_WIDTH = 3 * ATTN_WIDTH
REST_WIDTH = PROJ_MAIN - QKV_WIDTH
COL_ZA = 0
COL_ZS = ATTN_WIDTH
COL_X = COL_ZS + SSM_WIDTH
COL_B = COL_X + SSM_WIDTH
COL_C = COL_B + SSM_GROUPS * SSM_STATE

f32 = jnp.float32
bf16 = jnp.bfloat16


def _params(*sem):
    return pltpu.CompilerParams(dimension_semantics=sem, vmem_limit_bytes=VMEM_LIMIT)


def _bucket_tiles():
    qi = np.arange(QBLK)[:, None]
    kj = np.arange(KBLK)[None, :]
    rel = kj - RADIUS - qi
    nb = N_BUCKETS // 2
    max_exact = nb // 2
    tiles = []
    for dil in DILATIONS:
        r = rel * dil
        n = np.abs(r)
        nf = np.maximum(n, 1).astype(np.float32)
        large = max_exact + (np.log(nf / np.float32(max_exact)) / np.float32(math.log(BUCKET_MAX_DIST / max_exact))
                             * np.float32(nb - max_exact)).astype(np.int32)
        large = np.minimum(large, nb - 1)
        bucket = np.where(r > 0, nb, 0) + np.where(n < max_exact, n, large)
        tiles.append(np.where(np.abs(rel) <= RADIUS, bucket, -1))
    return np.stack(tiles).astype(np.int32)


def _relbias_kernel(tab_ref, idx_ref, out_ref):
    h = pl.program_id(0)
    for p in range(len(DILATIONS)):
        idx = idx_ref[p]
        acc = jnp.full((QBLK, KBLK), NEG, f32)
        for b in range(N_BUCKETS):
            acc = jnp.where(idx == b, tab_ref[b, h] * LOG2E, acc)
        out_ref[p, 0] = acc


def _relbias(rel_table):
    idx = jnp.asarray(_bucket_tiles())
    np_ = len(DILATIONS)
    return pl.pallas_call(
        _relbias_kernel,
        grid=(ATTN_HEADS,),
        in_specs=[pl.BlockSpec(memory_space=pltpu.SMEM),
                  pl.BlockSpec((np_, QBLK, KBLK), lambda h: (0, 0, 0))],
        out_specs=pl.BlockSpec((np_, 1, QBLK, KBLK), lambda h: (0, h, 0, 0)),
        out_shape=jax.ShapeDtypeStruct((np_, ATTN_HEADS, QBLK, KBLK), f32),
        compiler_params=_params("arbitrary"),
        name="relbias",
    )(rel_table, idx)


INPROJ_TM = 1024
INPROJ_TN = 1024
NORM_ROWS = 128


QKV_TILES = QKV_WIDTH // INPROJ_TN
HEADS_PER_TILE = INPROJ_TN // HEAD_DIM


NORM_STEPS = INPROJ_TM // NORM_ROWS


def _inproj_kernel(x_ref, g_ref, w_ref, wdt_ref, qkv_ref, rest_ref, dt_ref, h_ref):
    i = pl.program_id(0)
    j = pl.program_id(1)
    cur = lax.rem(i, 2)

    def normalise(rows, slot):
        x = x_ref[rows, :]
        y = x * lax.rsqrt(jnp.mean(x * x, axis=-1, keepdims=True) + EPS)
        h_ref[slot, rows, :] = (y * g_ref[...]).astype(bf16)

    def matmul(with_norm):
        tile = jnp.dot(h_ref[cur], w_ref[...], preferred_element_type=f32).astype(bf16)
        if with_norm:
            normalise(pl.ds(pl.multiple_of((j - 1) * NORM_ROWS, NORM_ROWS), NORM_ROWS), 1 - cur)
        return tile

    def store_qkv(tile):
        for hh in range(HEADS_PER_TILE):
            qkv_ref[hh] = tile[:, hh * HEAD_DIM:(hh + 1) * HEAD_DIM]

    @pl.when((i == 0) & (j == 0))
    def _():
        def rows(c, carry):
            normalise(pl.ds(pl.multiple_of(c * NORM_ROWS, NORM_ROWS), NORM_ROWS), 0)
            return carry

        lax.fori_loop(0, NORM_STEPS, rows, 0)

    @pl.when(j == 0)
    def _():
        dt_ref[...] = jnp.dot(h_ref[cur], wdt_ref[...], preferred_element_type=f32)
        store_qkv(matmul(False))

    @pl.when((j >= 1) & (j < QKV_TILES))
    def _():
        store_qkv(matmul(True))

    @pl.when((j >= QKV_TILES) & (j <= NORM_STEPS))
    def _():
        rest_ref[...] = matmul(True)

    @pl.when(j > NORM_STEPS)
    def _():
        rest_ref[...] = matmul(False)


def _inproj(x2, g, w_main, w_dt):
    m = x2.shape[0]
    last = m // INPROJ_TM - 1
    return pl.pallas_call(
        _inproj_kernel,
        grid=(m // INPROJ_TM, PROJ_MAIN // INPROJ_TN),
        in_specs=[pl.BlockSpec((INPROJ_TM, D_MODEL),
                               lambda i, j: (jnp.where((i == 0) & (j == 0), 0, jnp.minimum(i + 1, last)), 0)),
                  pl.BlockSpec((1, D_MODEL), lambda i, j: (0, 0)),
                  pl.BlockSpec((D_MODEL, INPROJ_TN), lambda i, j: (0, j)),
                  pl.BlockSpec((D_MODEL, LANES), lambda i, j: (0, 0))],
        out_specs=[pl.BlockSpec((HEADS_PER_TILE, INPROJ_TM, HEAD_DIM),
                                lambda i, j: (jnp.minimum(j, QKV_TILES - 1), i, 0)),
                   pl.BlockSpec((INPROJ_TM, INPROJ_TN), lambda i, j: (i, jnp.maximum(j - QKV_TILES, 0))),
                   pl.BlockSpec((INPROJ_TM, LANES), lambda i, j: (i, 0))],
        out_shape=[jax.ShapeDtypeStruct((3 * ATTN_HEADS, m, HEAD_DIM), bf16),
                   jax.ShapeDtypeStruct((m, REST_WIDTH), bf16),
                   jax.ShapeDtypeStruct((m, LANES), f32)],
        scratch_shapes=[pltpu.VMEM((2, INPROJ_TM, D_MODEL), bf16)],
        compiler_params=_params("arbitrary", "arbitrary"),
        name="inproj",
    )(x2, g, w_main, w_dt)


SCORE_SCALE = LOG2E / math.sqrt(HEAD_DIM)
CLASS_STEP = 4


def _attn_kernel(q_ref, k_ref, v_ref, bias_ref, o_ref,
                 tok_ref, cls4_ref, qd_ref, kd_ref, vd_ref, bias4_ref,
                 acc4_ref, m4_ref, l4_ref, acc1_ref, m1_ref, l1_ref,
                 s_buf, p_buf, mn_buf, rs_buf, *, seq):
    nblk = seq // QBLK
    n4 = seq // 4
    n16 = seq // 16
    srcs = (q_ref, k_ref, v_ref)
    dsts = ((qd_ref, 0), (kd_ref, RADIUS), (vd_ref, RADIUS))

    @pl.when(pl.program_id(1) == 0)
    def _():
        col = lax.broadcasted_iota(jnp.int32, (QBLK, KBLK), 1)
        before = col < RADIUS
        after = col >= QBLK + RADIUS
        for p in range(len(DILATIONS)):
            b = bias_ref[p, 0]
            bias4_ref[p, 0] = b
            bias4_ref[p, 1] = jnp.where(before, NEG, b)
            bias4_ref[p, 2] = jnp.where(after, NEG, b)
            bias4_ref[p, 3] = jnp.where(before | after, NEG, b)
        pad = jnp.zeros((RADIUS, HEAD_DIM), bf16)
        for ref in (kd_ref, vd_ref):
            ref[0:RADIUS, :] = pad
            ref[RADIUS + seq:2 * RADIUS + seq, :] = pad

    def stage(i, carry):
        r = pl.ds(pl.multiple_of(i * QBLK, QBLK), QBLK)
        for x, src in enumerate(srcs):
            tok_ref[x, r, :] = src[r, :].astype(f32)
        return carry

    lax.fori_loop(0, nblk, stage, 0)
    for x in range(3):
        for c4 in range(CLASS_STEP):
            cls4_ref[x, c4 * n4:(c4 + 1) * n4, :] = tok_ref[x, pl.ds(c4, n4, stride=CLASS_STEP), :]

    def tile_rows(t):
        return pl.ds(pl.multiple_of(t * QBLK, QBLK), QBLK)

    def scores(t, u, p, tiles_per_class):
        r0 = pl.multiple_of(t * QBLK, QBLK)
        q = (q_ref if DILATIONS[p] == 1 else qd_ref)[pl.ds(r0, QBLK), :]
        k = kd_ref[pl.ds(r0, KBLK), :]
        jj = lax.rem(t, tiles_per_class)
        variant = (jj == 0).astype(jnp.int32) + 2 * (jj == tiles_per_class - 1).astype(jnp.int32)
        s = lax.dot_general(q, k, (((1,), (1,)), ((), ())), preferred_element_type=f32)
        s_buf[u] = s + bias4_ref[p, variant]

    def softmax(t, u, prev):
        s = s_buf[u]
        top = jnp.max(s, axis=-1, keepdims=True)
        if prev is None:
            mn = jnp.broadcast_to(top, (QBLK, HEAD_DIM))
        else:
            mn = jnp.maximum(prev[1][tile_rows(t), :], top)
        e_lo = jnp.exp2(s[:, :HEAD_DIM] - mn)
        e_hi = jnp.exp2(s[:, HEAD_DIM:] - mn)
        p_buf[u, :, :HEAD_DIM] = e_lo.astype(bf16)
        p_buf[u, :, HEAD_DIM:] = e_hi.astype(bf16)
        mn_buf[u] = mn
        rs_buf[u] = jnp.broadcast_to(jnp.sum(e_lo + e_hi, axis=-1, keepdims=True), (QBLK, HEAD_DIM))

    def update(t, u, prev, nxt, dst_rows):
        r = tile_rows(t)
        v = vd_ref[pl.ds(pl.multiple_of(t * QBLK, QBLK), KBLK), :]
        acc = jnp.dot(p_buf[u], v, preferred_element_type=f32)
        mn = mn_buf[u]
        l = rs_buf[u]
        if prev is not None:
            alpha = jnp.exp2(prev[1][r, :] - mn)
            l = alpha * prev[2][r, :] + l
            acc = alpha * prev[0][r, :] + acc
        if nxt is None:
            o_ref[r, :] = (acc / l).astype(o_ref.dtype)
        else:
            nxt[0][dst_rows, :] = acc
            nxt[1][dst_rows, :] = mn
            nxt[2][dst_rows, :] = l

    def run_pattern(p, prev, nxt, dst_rows_fn):
        tiles_per_class = seq // DILATIONS[p] // QBLK

        def group(i, carry):
            tiles = [i * TILE_GROUP + u for u in range(TILE_GROUP)]
            for u, t in enumerate(tiles):
                scores(t, u, p, tiles_per_class)
            for u, t in enumerate(tiles):
                softmax(t, u, prev)
            for u, t in enumerate(tiles):
                dst = None if nxt is None else dst_rows_fn(t // tiles_per_class, lax.rem(t, tiles_per_class))
                update(t, u, prev, nxt, dst)
            return carry

        lax.fori_loop(0, nblk // TILE_GROUP, group, 0)

    for c16 in range(16):
        rows = pl.ds((c16 % 4) * n4 + c16 // 4, n16, stride=CLASS_STEP)
        for x, (dst, off) in enumerate(dsts):
            dst[off + c16 * n16:off + (c16 + 1) * n16, :] = cls4_ref[x, rows, :].astype(bf16)
    state4 = (acc4_ref, m4_ref, l4_ref)
    state1 = (acc1_ref, m1_ref, l1_ref)
    run_pattern(2, None, state4,
                lambda c16, jj: pl.ds(lax.rem(c16, 4) * n4 + c16 // 4 + jj * (QBLK * CLASS_STEP), QBLK,
                                      stride=CLASS_STEP))

    def copy_cls4(i, carry):
        r0 = pl.multiple_of(i * QBLK, QBLK)
        for x, (dst, off) in enumerate(dsts):
            dst[pl.ds(r0 + off, QBLK), :] = cls4_ref[x, pl.ds(r0, QBLK), :].astype(bf16)
        return carry

    lax.fori_loop(0, nblk, copy_cls4, 0)
    run_pattern(1, state4, state1,
                lambda c4, jj: pl.ds(c4 + jj * (QBLK * CLASS_STEP), QBLK, stride=CLASS_STEP))

    def copy_tok(i, carry):
        r0 = pl.multiple_of(i * QBLK, QBLK)
        for src, (dst, off) in zip(srcs[1:], dsts[1:]):
            dst[pl.ds(r0 + off, QBLK), :] = src[pl.ds(r0, QBLK), :]
        return carry

    lax.fori_loop(0, nblk, copy_tok, 0)
    run_pattern(0, state1, None, None)


def _attention(qkv4, bias):
    _, b, seq, _ = qkv4.shape
    np_ = len(DILATIONS)
    head = lambda first: pl.BlockSpec((None, None, seq, HEAD_DIM), lambda h, i: (first + h, i, 0, 0))
    return pl.pallas_call(
        functools.partial(_attn_kernel, seq=seq),
        grid=(ATTN_HEADS, b),
        in_specs=[head(0), head(ATTN_HEADS), head(2 * ATTN_HEADS),
                  pl.BlockSpec((np_, 1, QBLK, KBLK), lambda h, i: (0, h, 0, 0))],
        out_specs=head(0),
        out_shape=jax.ShapeDtypeStruct((ATTN_HEADS, b, seq, HEAD_DIM), bf16),
        scratch_shapes=[pltpu.VMEM((3, seq, HEAD_DIM), f32)] * 2
                       + [pltpu.VMEM((seq, HEAD_DIM), bf16)]
                       + [pltpu.VMEM((seq + 2 * RADIUS, HEAD_DIM), bf16)] * 2
                       + [pltpu.VMEM((np_, 4, QBLK, KBLK), f32)]
                       + [pltpu.VMEM((seq, HEAD_DIM), f32)] * 6
                       + [pltpu.VMEM((TILE_GROUP, QBLK, KBLK), f32),
                          pltpu.VMEM((TILE_GROUP, QBLK, KBLK), bf16),
                          pltpu.VMEM((TILE_GROUP, QBLK, HEAD_DIM), f32),
                          pltpu.VMEM((TILE_GROUP, QBLK, HEAD_DIM), f32)],
        compiler_params=_params("arbitrary", "arbitrary"),
        name="attn",
    )(qkv4, qkv4, qkv4, bias)


HALO = BF16_ROWS


def _silu(x):
    h = 0.5 * x
    return h + h * jnp.tanh(h)


def _softplus(x):
    return jnp.maximum(x, 0.0) + jnp.log1p(jnp.exp(-jnp.abs(x)))


def _ssd_kernel(z_ref, x_ref, bm_ref, cm_ref, dt_ref, cwx_ref, cbx_ref, cwb_ref, cbb_ref, cwc_ref, cbc_ref,
                hp_ref, dskip_ref, ng_ref, out_ref,
                xs_ref, bs_ref, cs_ref, bt_ref, y_ref, acs_ref, rows_ref, decx_ref, hf_ref, hb_ref,
                *stage_refs, seq):
    nc = seq // CHUNK
    g = pl.program_id(1)
    half = CONV_WIDTH // 2
    nh = HEADS_PER_GROUP
    n_slabs = len(stage_refs) // 2

    lane = lax.broadcasted_iota(jnp.int32, (CHUNK, LANES), 1)
    left = lane < SSM_HEAD_DIM
    mask_lo = jnp.where(left, 1.0, 0.0).astype(bf16)
    mask_hi = jnp.where(left, 0.0, 1.0).astype(bf16)
    li = lax.broadcasted_iota(jnp.int32, (CHUNK, CHUNK), 0)
    si = lax.broadcasted_iota(jnp.int32, (CHUNK, CHUNK), 1)
    lower = li >= si
    below = li > si
    above = si > li
    tril = lower.astype(f32)
    dt_bias = hp_ref[0, 0:1, :]
    a_row = -jnp.exp(hp_ref[0, 1:2, :]) * LOG2E
    shift = lax.rem(LANES - 2 * nh * g, LANES)
    head_of_col = lax.broadcasted_iota(jnp.int32, (LANES, GROUP_WIDTH), 1) // SSM_HEAD_DIM
    head_lane = lax.broadcasted_iota(jnp.int32, (LANES, GROUP_WIDTH), 0)
    spread_f = (head_lane == head_of_col).astype(bf16)
    spread_b = (head_lane == head_of_col + nh).astype(bf16)
    tril_bf = tril.astype(bf16)

    def split3(x):
        hi = x.astype(bf16)
        rest = x - hi.astype(f32)
        mid = rest.astype(bf16)
        return hi, mid, (rest - mid.astype(f32)).astype(bf16)

    def dot3(lhs, rhs):
        return sum(jnp.dot(a, b, preferred_element_type=f32) for a in lhs for b in rhs)

    half_rows = CHUNK // 2

    def prepare(i, carry):
        r0 = pl.multiple_of(i * CHUNK, CHUNK)
        r = pl.ds(r0, CHUNK)
        lo = pl.multiple_of(jnp.maximum(r0 - HALO, 0), HALO)
        hi = pl.multiple_of(jnp.minimum(r0 + CHUNK, seq - HALO), HALO)
        has_lo = jnp.where(i > 0, 1.0, 0.0)
        has_hi = jnp.where(i < nc - 1, 1.0, 0.0)
        slab = 0
        for src, w_ref, b_ref, dst in ((x_ref, cwx_ref, cbx_ref, xs_ref),
                                       (bm_ref, cwb_ref, cbb_ref, bs_ref),
                                       (cm_ref, cwc_ref, cbc_ref, cs_ref)):
            for blk in range(src.shape[-1] // LANES):
                cols = slice(blk * LANES, (blk + 1) * LANES)
                st, ost = stage_refs[slab], stage_refs[n_slabs + slab]
                st[0:HALO, :] = src[0, pl.ds(lo, HALO), cols].astype(f32) * has_lo
                st[HALO:HALO + CHUNK, :] = src[0, r, cols].astype(f32)
                st[HALO + CHUNK:2 * HALO + CHUNK, :] = src[0, pl.ds(hi, HALO), cols].astype(f32) * has_hi
                planes = [st[pl.ds(HALO - half + s, half_rows, stride=2), :]
                          for s in range(CONV_WIDTH + 1)]
                for parity in range(2):
                    acc = b_ref[:, cols]
                    for tap in range(CONV_WIDTH):
                        acc = acc + w_ref[tap:tap + 1, cols] * planes[tap + parity]
                    ost[pl.ds(parity, half_rows, stride=2), :] = _silu(acc)
                out = ost[...].astype(bf16)
                dst[r, cols] = out
                if dst is bs_ref:
                    bt_ref[r, :] = out.astype(f32).T.astype(bf16)
                slab += 1

        raw = pltpu.roll(dt_ref[0, r, :], shift, 1)
        dtv = _softplus(raw + dt_bias)
        adt = dtv * a_row
        cum = dot3((tril_bf,), split3(adt))
        total = cum[CHUNK - 1:CHUNK, :]
        acs = jnp.where(lane < nh, cum, total - cum + adt)
        acs_ref[r, :] = acs
        rows_ref[i, 0] = (acs - jnp.log2(dtv)).T[0:2 * nh, :]
        rows_ref[i, 1] = (jnp.exp2(total - acs) * dtv).T[0:2 * nh, :]
        rows_ref[i, 2] = jnp.log2(dtv + pltpu.roll(dtv, LANES - nh, 1)).T[0:2 * nh, :]
        decay = split3(jnp.broadcast_to(jnp.exp2(total), (8, LANES)))
        decx_ref[i, 0] = dot3(decay, (spread_f,))
        decx_ref[i, 1] = dot3(decay, (spread_b,))
        return carry

    prepare(jnp.int32(0), 0)

    hf_ref[...] = jnp.zeros_like(hf_ref)
    hb_ref[...] = jnp.zeros_like(hb_ref)

    def column(acs, j):
        return jnp.broadcast_to(acs[:, j:j + 1], (CHUNK, CHUNK))

    def forward(c, carry):
        r = pl.ds(pl.multiple_of(c * CHUNK, CHUNK), CHUNK)
        acs = acs_ref[r, :]
        cmat = cs_ref[r, :]
        btf = bt_ref[r, :].astype(f32)
        cb = lax.dot_general(cmat, bs_ref[r, :], (((1,), (1,)), ((), ())), preferred_element_type=f32)
        y_off = jnp.dot(cmat, hf_ref[...].astype(bf16), preferred_element_type=f32)
        for k in range(nh // 2):
            cols = slice(k * LANES, (k + 1) * LANES)
            xp = xs_ref[r, cols]
            halves = (xp * mask_lo, xp * mask_hi)
            y_diag = snew = None
            e_f = []
            for side in range(2):
                j = 2 * k + side
                a_f, a_b = column(acs, j), column(acs, nh + j)
                w = jnp.exp2(jnp.where(below, a_f - rows_ref[c, 0, j:j + 1, :],
                                       jnp.where(above, a_b - rows_ref[c, 0, nh + j:nh + j + 1, :],
                                                 rows_ref[c, 2, j:j + 1, :])))
                d = jnp.dot((cb * w).astype(bf16), halves[side], preferred_element_type=f32)
                inject = (btf * rows_ref[c, 1, j:j + 1, :]).astype(bf16)
                s = jnp.dot(inject, halves[side], preferred_element_type=f32)
                y_diag = d if y_diag is None else y_diag + d
                snew = s if snew is None else snew + s
                e_f.append(a_f)
            y_ref[r, cols] = (y_diag + y_off[:, cols] * jnp.exp2(jnp.where(left, e_f[0], e_f[1]))
                              + dskip_ref[:, cols] * xp.astype(f32))
            hf_ref[:, cols] = hf_ref[:, cols] * decx_ref[c, 0, 0:1, cols] + snew
        return carry

    def forward_and_prepare(c, carry):
        forward(c, carry)
        return prepare(jnp.minimum(c + 1, nc - 1), carry)

    lax.fori_loop(0, nc, forward_and_prepare, 0)

    def backward(i, carry):
        c = nc - 1 - i
        r = pl.ds(pl.multiple_of(c * CHUNK, CHUNK), CHUNK)
        acs = acs_ref[r, :]
        btf = bt_ref[r, :].astype(f32)
        y_off = jnp.dot(cs_ref[r, :], hb_ref[...].astype(bf16), preferred_element_type=f32)
        ys = []
        for k in range(nh // 2):
            cols = slice(k * LANES, (k + 1) * LANES)
            xp = xs_ref[r, cols]
            halves = (xp * mask_lo, xp * mask_hi)
            snew = None
            e_b = []
            for side in range(2):
                jb = nh + 2 * k + side
                inject = (btf * rows_ref[c, 1, jb:jb + 1, :]).astype(bf16)
                s = jnp.dot(inject, halves[side], preferred_element_type=f32)
                snew = s if snew is None else snew + s
                e_b.append(column(acs, jb))
            ys.append(y_ref[r, cols] + y_off[:, cols] * jnp.exp2(jnp.where(left, e_b[0], e_b[1])))
            hb_ref[:, cols] = hb_ref[:, cols] * decx_ref[c, 1, 0:1, cols] + snew
        y = jnp.concatenate(ys, axis=1) * _silu(z_ref[0, r, :].astype(f32))
        y = y * lax.rsqrt(jnp.mean(y * y, axis=-1, keepdims=True) + EPS)
        out_ref[0, r, :] = (y * ng_ref[...]).astype(out_ref.dtype)
        return carry

    lax.fori_loop(0, nc, backward, 0, unroll=2)


def _ssd(proj3, dt3, conv_w, conv_b, head_params, dskip_cols, norm_g):
    b, seq, _ = proj3.shape
    gw, ns = GROUP_WIDTH, SSM_STATE
    wide = lambda off: pl.BlockSpec((1, seq, gw), lambda i, g: (i, 0, off // gw + g))
    narrow = lambda off: pl.BlockSpec((1, seq, ns), lambda i, g: (i, 0, off // ns + g))
    xoff, boff, coff = 0, SSM_WIDTH, SSM_WIDTH + SSM_GROUPS * ns
    cw = lambda off, w: pl.BlockSpec((CONV_WIDTH, w), lambda i, g: (0, off // w + g))
    cbias = lambda off, w: pl.BlockSpec((1, w), lambda i, g: (0, off // w + g))
    return pl.pallas_call(
        functools.partial(_ssd_kernel, seq=seq),
        grid=(b, SSM_GROUPS),
        in_specs=[wide(COL_ZS), wide(COL_X), narrow(COL_B), narrow(COL_C),
                  pl.BlockSpec((1, seq, LANES), lambda i, g: (i, 0, 0)),
                  cw(xoff, gw), cbias(xoff, gw), cw(boff, ns), cbias(boff, ns), cw(coff, ns), cbias(coff, ns),
                  pl.BlockSpec((1, 8, LANES), lambda i, g: (g, 0, 0)),
                  pl.BlockSpec((1, gw), lambda i, g: (0, g)),
                  pl.BlockSpec((1, gw), lambda i, g: (0, g))],
        out_specs=pl.BlockSpec((1, seq, gw), lambda i, g: (i, 0, g)),
        out_shape=jax.ShapeDtypeStruct((b, seq, SSM_WIDTH), bf16),
        scratch_shapes=[pltpu.VMEM((seq, gw), bf16),
                        pltpu.VMEM((seq, ns), bf16),
                        pltpu.VMEM((seq, ns), bf16),
                        pltpu.VMEM((seq, ns), bf16),
                        pltpu.VMEM((seq, gw), f32),
                        pltpu.VMEM((seq, LANES), f32),
                        pltpu.VMEM((seq // CHUNK, 3, 2 * HEADS_PER_GROUP, CHUNK), f32),
                        pltpu.VMEM((seq // CHUNK, 2, 8, gw), f32),
                        pltpu.VMEM((ns, gw), f32),
                        pltpu.VMEM((ns, gw), f32),
                        ]
                       + [pltpu.VMEM((CHUNK + 2 * HALO, LANES), f32)] * ((gw + 2 * ns) // LANES)
                       + [pltpu.VMEM((CHUNK, LANES), f32)] * ((gw + 2 * ns) // LANES),
        compiler_params=_params("arbitrary", "arbitrary"),
        name="ssd",
    )(proj3, proj3, proj3, proj3, dt3, conv_w, conv_b, conv_w, conv_b, conv_w, conv_b,
      head_params, dskip_cols, norm_g)


OUTPROJ_TM = 512


def _outproj_kernel(o_ref, z_ref, s_ref, x_ref, w_ref, ga_ref, gp_ref, y_ref):
    o = jnp.concatenate([o_ref[h] for h in range(ATTN_HEADS)], axis=1).astype(f32)
    a = o * lax.rsqrt(jnp.mean(o * o, axis=-1, keepdims=True) + EPS) * ga_ref[...]
    a = a * _silu(z_ref[...].astype(f32))
    mix = jnp.dot(a.astype(bf16), w_ref[0:ATTN_WIDTH, :], preferred_element_type=f32)
    mix = mix + jnp.dot(s_ref[...], w_ref[ATTN_WIDTH:MIX_WIDTH, :], preferred_element_type=f32)
    mix = mix * lax.rsqrt(jnp.mean(mix * mix, axis=-1, keepdims=True) + EPS) * gp_ref[...]
    y_ref[...] = x_ref[...] + mix


def _outproj(o3, rest2, s2, x2, w_out, g_attn, g_post):
    m = x2.shape[0]
    tm = OUTPROJ_TM
    row = lambda width, blk: pl.BlockSpec((tm, width), lambda i: (i, blk))
    const = lambda shape: pl.BlockSpec(shape, lambda i: (0, 0))
    return pl.pallas_call(
        _outproj_kernel,
        grid=(m // tm,),
        in_specs=[pl.BlockSpec((ATTN_HEADS, tm, HEAD_DIM), lambda i: (0, i, 0)),
                  row(ATTN_WIDTH, COL_ZA // ATTN_WIDTH), row(SSM_WIDTH, 0), row(D_MODEL, 0),
                  pl.BlockSpec((MIX_WIDTH, D_MODEL), lambda i: (0, 0), pipeline_mode=pl.Buffered(1)),
                  const((1, ATTN_WIDTH)), const((1, D_MODEL))],
        out_specs=row(D_MODEL, 0),
        out_shape=jax.ShapeDtypeStruct((m, D_MODEL), f32),
        compiler_params=_params("arbitrary"),
        name="outproj",
    )(o3, rest2, s2, x2, w_out, g_attn, g_post)


def _group_head_rows(fwd, bwd):
    rows = jnp.concatenate([fwd.reshape(SSM_GROUPS, HEADS_PER_GROUP), bwd.reshape(SSM_GROUPS, HEADS_PER_GROUP)],
                           axis=1)
    return jnp.pad(rows, ((0, 0), (0, LANES - 2 * HEADS_PER_GROUP)))


def _layer(x, bias, pre_g, w_main, w_dt, attn_g, conv_w, conv_b, head_params, dskip_cols, ssm_g, w_out, post_g):
    b, seq, _ = x.shape
    x2 = x.reshape(b * seq, D_MODEL)
    qkv, rest2, dt2 = _inproj(x2, pre_g, w_main, w_dt)
    o = _attention(qkv.reshape(3 * ATTN_HEADS, b, seq, HEAD_DIM), bias)
    s = _ssd(rest2.reshape(b, seq, REST_WIDTH), dt2.reshape(b, seq, LANES), conv_w, conv_b, head_params,
             dskip_cols, ssm_g)
    y = _outproj(o.reshape(ATTN_HEADS, b * seq, HEAD_DIM), rest2, s.reshape(b * seq, SSM_WIDTH), x2, w_out,
                 attn_g, post_g)
    return y.reshape(b, seq, D_MODEL)


def kernel(x_prompt, x_sample, pre_norm_g, w_in, rel_bias_table, attn_norm_g, conv_w, conv_b, dt_bias_fwd,
           dt_bias_bwd, a_log_fwd, a_log_bwd, d_skip, ssm_norm_g, w_out, post_norm_g):
    depth = w_in.shape[0]
    bias = _relbias(rel_bias_table)
    y_prompt, y_sample = x_prompt, x_sample
    for i in range(depth):
        col_scale = jnp.where(jnp.arange(PROJ_MAIN) < ATTN_WIDTH, SCORE_SCALE, 1.0).astype(f32)
        w_main = (w_in[i, :, :PROJ_MAIN] * col_scale).astype(bf16)
        w_dt = w_in[i, :, PROJ_MAIN:]
        w_dt = jnp.concatenate([w_dt[:, :SSM_HEADS].reshape(D_MODEL, SSM_GROUPS, HEADS_PER_GROUP),
                                w_dt[:, SSM_HEADS:].reshape(D_MODEL, SSM_GROUPS, HEADS_PER_GROUP)], axis=2)
        w_dt = jnp.pad(w_dt.reshape(D_MODEL, 2 * SSM_HEADS), ((0, 0), (0, LANES - 2 * SSM_HEADS))).astype(bf16)
        head_params = jnp.stack([_group_head_rows(dt_bias_fwd[i], dt_bias_bwd[i]),
                                 _group_head_rows(a_log_fwd[i], a_log_bwd[i])], axis=1)
        head_params = jnp.pad(head_params, ((0, 0), (0, 6), (0, 0)))
        args = (bias, pre_norm_g[i].reshape(1, D_MODEL), w_main, w_dt, attn_norm_g[i].reshape(1, ATTN_WIDTH),
                conv_w[i], conv_b[i].reshape(1, CONV_CH), head_params,
                jnp.repeat(d_skip[i], SSM_HEAD_DIM).reshape(1, SSM_WIDTH), ssm_norm_g[i].reshape(1, SSM_WIDTH),
                w_out[i].astype(bf16), post_norm_g[i].reshape(1, D_MODEL))
        y_prompt = _layer(y_prompt, *args)
        y_sample = _layer(y_sample, *args)
    return (y_prompt, y_sample)
```

```python
import functools
import math

import numpy as np
import jax
import jax.numpy as jnp
from jax import lax
from jax.experimental import pallas as pl
from jax.experimental.pallas import tpu as pltpu

D_MODEL = 2048
ATTN_HEADS = 16
HEAD_DIM = 128
ATTN_WIDTH = ATTN_HEADS * HEAD_DIM
DILATIONS = (1, 4, 16)
RADIUS = 64
N_BUCKETS = 32
BUCKET_MAX_DIST = 1024
SSM_HEADS = 32
SSM_HEAD_DIM = 64
SSM_WIDTH = SSM_HEADS * SSM_HEAD_DIM
SSM_GROUPS = 4
HEADS_PER_GROUP = SSM_HEADS // SSM_GROUPS
GROUP_WIDTH = SSM_WIDTH // SSM_GROUPS
SSM_STATE = 128
CONV_WIDTH = 5
CHUNK = 128
CONV_CH = SSM_WIDTH + 2 * SSM_GROUPS * SSM_STATE
MIX_WIDTH = ATTN_WIDTH + SSM_WIDTH
PROJ_MAIN = 4 * ATTN_WIDTH + SSM_WIDTH + CONV_CH
EPS = 1e-6

LANES = 128
BF16_ROWS = 16
QBLK = 128
KBLK = QBLK + 2 * RADIUS
TILE_GROUP = 16
LOG2E = math.log2(math.e)
NEG = -1e30
VMEM_LIMIT = 56 * 1024 * 1024

QKV_WIDTH = 3 * ATTN_WIDTH
REST_WIDTH = PROJ_MAIN - QKV_WIDTH
COL_ZA = 0
COL_ZS = ATTN_WIDTH
COL_X = COL_ZS + SSM_WIDTH
COL_B = COL_X + SSM_WIDTH
COL_C = COL_B + SSM_GROUPS * SSM_STATE

f32 = jnp.float32
bf16 = jnp.bfloat16


def _params(*sem):
    return pltpu.CompilerParams(dimension_semantics=sem, vmem_limit_bytes=VMEM_LIMIT)


def _bucket_tiles():
    qi = np.arange(QBLK)[:, None]
    kj = np.arange(KBLK)[None, :]
    rel = kj - RADIUS - qi
    nb = N_BUCKETS // 2
    max_exact = nb // 2
    tiles = []
    for dil in DILATIONS:
        r = rel * dil
        n = np.abs(r)
        nf = np.maximum(n, 1).astype(np.float32)
        large = max_exact + (np.log(nf / np.float32(max_exact)) / np.float32(math.log(BUCKET_MAX_DIST / max_exact))
                             * np.float32(nb - max_exact)).astype(np.int32)
        large = np.minimum(large, nb - 1)
        bucket = np.where(r > 0, nb, 0) + np.where(n < max_exact, n, large)
        tiles.append(np.where(np.abs(rel) <= RADIUS, bucket, -1))
    return np.stack(tiles).astype(np.int32)


def _relbias_kernel(tab_ref, idx_ref, out_ref):
    h = pl.program_id(0)
    for p in range(len(DILATIONS)):
        idx = idx_ref[p]
        acc = jnp.full((QBLK, KBLK), NEG, f32)
        for b in range(N_BUCKETS):
            acc = jnp.where(idx == b, tab_ref[b, h] * LOG2E, acc)
        out_ref[p, 0] = acc


def _relbias(rel_table):
    idx = jnp.asarray(_bucket_tiles())
    np_ = len(DILATIONS)
    return pl.pallas_call(
        _relbias_kernel,
        grid=(ATTN_HEADS,),
        in_specs=[pl.BlockSpec(memory_space=pltpu.SMEM),
                  pl.BlockSpec((np_, QBLK, KBLK), lambda h: (0, 0, 0))],
        out_specs=pl.BlockSpec((np_, 1, QBLK, KBLK), lambda h: (0, h, 0, 0)),
        out_shape=jax.ShapeDtypeStruct((np_, ATTN_HEADS, QBLK, KBLK), f32),
        compiler_params=_params("arbitrary"),
        name="relbias",
    )(rel_table, idx)


INPROJ_TM = 1024
INPROJ_TN = 1024
NORM_ROWS = 128


QKV_TILES = QKV_WIDTH // INPROJ_TN
HEADS_PER_TILE = INPROJ_TN // HEAD_DIM


NORM_STEPS = INPROJ_TM // NORM_ROWS


def _inproj_kernel(x_ref, g_ref, w_ref, wdt_ref, qkv_ref, rest_ref, dt_ref, h_ref):
    i = pl.program_id(0)
    j = pl.program_id(1)
    cur = lax.rem(i, 2)

    def normalise(rows, slot):
        x = x_ref[rows, :]
        y = x * lax.rsqrt(jnp.mean(x * x, axis=-1, keepdims=True) + EPS)
        h_ref[slot, rows, :] = (y * g_ref[...]).astype(bf16)

    def matmul(with_norm, scaled=False):
        tile = jnp.dot(h_ref[cur], w_ref[...], preferred_element_type=f32)
        if scaled:
            tile = tile * jnp.where(j < ATTN_WIDTH // INPROJ_TN, SCORE_SCALE, 1.0)
        tile = tile.astype(bf16)
        if with_norm:
            normalise(pl.ds(pl.multiple_of((j - 1) * NORM_ROWS, NORM_ROWS), NORM_ROWS), 1 - cur)
        return tile

    def store_qkv(tile):
        for hh in range(HEADS_PER_TILE):
            qkv_ref[hh] = tile[:, hh * HEAD_DIM:(hh + 1) * HEAD_DIM]

    @pl.when((i == 0) & (j == 0))
    def _():
        def rows(c, carry):
            normalise(pl.ds(pl.multiple_of(c * NORM_ROWS, NORM_ROWS), NORM_ROWS), 0)
            return carry

        lax.fori_loop(0, NORM_STEPS, rows, 0)

    @pl.when(j == 0)
    def _():
        dt_ref[...] = jnp.dot(h_ref[cur], wdt_ref[...], preferred_element_type=f32)
        store_qkv(matmul(False, scaled=True))

    @pl.when((j >= 1) & (j < QKV_TILES))
    def _():
        store_qkv(matmul(True, scaled=True))

    @pl.when((j >= QKV_TILES) & (j <= NORM_STEPS))
    def _():
        rest_ref[...] = matmul(True)

    @pl.when(j > NORM_STEPS)
    def _():
        rest_ref[...] = matmul(False)


def _inproj(x2, g, w_main, w_dt):
    m = x2.shape[0]
    last = m // INPROJ_TM - 1
    return pl.pallas_call(
        _inproj_kernel,
        grid=(m // INPROJ_TM, PROJ_MAIN // INPROJ_TN),
        in_specs=[pl.BlockSpec((INPROJ_TM, D_MODEL),
                               lambda i, j: (jnp.where((i == 0) & (j == 0), 0, jnp.minimum(i + 1, last)), 0)),
                  pl.BlockSpec((1, D_MODEL), lambda i, j: (0, 0)),
                  pl.BlockSpec((D_MODEL, INPROJ_TN), lambda i, j: (0, j)),
                  pl.BlockSpec((D_MODEL, LANES), lambda i, j: (0, 0))],
        out_specs=[pl.BlockSpec((HEADS_PER_TILE, INPROJ_TM, HEAD_DIM),
                                lambda i, j: (jnp.minimum(j, QKV_TILES - 1), i, 0)),
                   pl.BlockSpec((INPROJ_TM, INPROJ_TN), lambda i, j: (i, jnp.maximum(j - QKV_TILES, 0))),
                   pl.BlockSpec((INPROJ_TM, LANES), lambda i, j: (i, 0))],
        out_shape=[jax.ShapeDtypeStruct((3 * ATTN_HEADS, m, HEAD_DIM), bf16),
                   jax.ShapeDtypeStruct((m, REST_WIDTH), bf16),
                   jax.ShapeDtypeStruct((m, LANES), f32)],
        scratch_shapes=[pltpu.VMEM((2, INPROJ_TM, D_MODEL), bf16)],
        compiler_params=_params("arbitrary", "arbitrary"),
        name="inproj",
    )(x2, g, w_main, w_dt)


SCORE_SCALE = LOG2E / math.sqrt(HEAD_DIM)
CLASS_STEP = 4


def _attn_kernel(q_ref, k_ref, v_ref, bias_ref, o_ref,
                 tok_ref, cls4_ref, qd_ref, kd_ref, vd_ref, bias4_ref,
                 acc4_ref, m4_ref, l4_ref, acc1_ref, m1_ref, l1_ref,
                 s_buf, p_buf, mn_buf, rs_buf, *, seq):
    nblk = seq // QBLK
    n4 = seq // 4
    n16 = seq // 16
    srcs = (q_ref, k_ref, v_ref)
    dsts = ((qd_ref, 0), (kd_ref, RADIUS), (vd_ref, RADIUS))

    @pl.when(pl.program_id(1) == 0)
    def _():
        col = lax.broadcasted_iota(jnp.int32, (QBLK, KBLK), 1)
        before = col < RADIUS
        after = col >= QBLK + RADIUS
        for p in range(len(DILATIONS)):
            b = bias_ref[p, 0]
            bias4_ref[p, 0] = b
            bias4_ref[p, 1] = jnp.where(before, NEG, b)
            bias4_ref[p, 2] = jnp.where(after, NEG, b)
            bias4_ref[p, 3] = jnp.where(before | after, NEG, b)
        pad = jnp.zeros((RADIUS, HEAD_DIM), bf16)
        for ref in (kd_ref, vd_ref):
            ref[0:RADIUS, :] = pad
            ref[RADIUS + seq:2 * RADIUS + seq, :] = pad

    def stage(i, carry):
        r = pl.ds(pl.multiple_of(i * QBLK, QBLK), QBLK)
        for x, src in enumerate(srcs):
            tok_ref[x, r, :] = src[r, :].astype(f32)
        return carry

    lax.fori_loop(0, nblk, stage, 0)
    for x in range(3):
        for c4 in range(CLASS_STEP):
            cls4_ref[x, c4 * n4:(c4 + 1) * n4, :] = tok_ref[x, pl.ds(c4, n4, stride=CLASS_STEP), :]

    def tile_rows(t):
        return pl.ds(pl.multiple_of(t * QBLK, QBLK), QBLK)

    def scores(t, u, p, tiles_per_class):
        r0 = pl.multiple_of(t * QBLK, QBLK)
        q = (q_ref if DILATIONS[p] == 1 else qd_ref)[pl.ds(r0, QBLK), :]
        k = kd_ref[pl.ds(r0, KBLK), :]
        jj = lax.rem(t, tiles_per_class)
        variant = (jj == 0).astype(jnp.int32) + 2 * (jj == tiles_per_class - 1).astype(jnp.int32)
        s = lax.dot_general(q, k, (((1,), (1,)), ((), ())), preferred_element_type=f32)
        s_buf[u] = s + bias4_ref[p, variant]

    def softmax(t, u, prev):
        s = s_buf[u]
        top = jnp.max(s, axis=-1, keepdims=True)
        if prev is None:
            mn = jnp.broadcast_to(top, (QBLK, HEAD_DIM))
        else:
            mn = jnp.maximum(prev[1][tile_rows(t), :], top)
        e_lo = jnp.exp2(s[:, :HEAD_DIM] - mn)
        e_hi = jnp.exp2(s[:, HEAD_DIM:] - mn)
        p_buf[u, :, :HEAD_DIM] = e_lo.astype(bf16)
        p_buf[u, :, HEAD_DIM:] = e_hi.astype(bf16)
        mn_buf[u] = mn
        rs_buf[u] = jnp.broadcast_to(jnp.sum(e_lo + e_hi, axis=-1, keepdims=True), (QBLK, HEAD_DIM))

    def update(t, u, prev, nxt, dst_rows):
        r = tile_rows(t)
        v = vd_ref[pl.ds(pl.multiple_of(t * QBLK, QBLK), KBLK), :]
        acc = jnp.dot(p_buf[u], v, preferred_element_type=f32)
        mn = mn_buf[u]
        l = rs_buf[u]
        if prev is not None:
            alpha = jnp.exp2(prev[1][r, :] - mn)
            l = alpha * prev[2][r, :] + l
            acc = alpha * prev[0][r, :] + acc
        if nxt is None:
            o_ref[r, :] = (acc / l).astype(o_ref.dtype)
        else:
            nxt[0][dst_rows, :] = acc
            nxt[1][dst_rows, :] = mn
            nxt[2][dst_rows, :] = l

    def run_pattern(p, prev, nxt, dst_rows_fn):
        tiles_per_class = seq // DILATIONS[p] // QBLK

        def group(i, carry):
            tiles = [i * TILE_GROUP + u for u in range(TILE_GROUP)]
            for u, t in enumerate(tiles):
                scores(t, u, p, tiles_per_class)
            for u, t in enumerate(tiles):
                softmax(t, u, prev)
            for u, t in enumerate(tiles):
                dst = None if nxt is None else dst_rows_fn(t // tiles_per_class, lax.rem(t, tiles_per_class))
                update(t, u, prev, nxt, dst)
            return carry

        lax.fori_loop(0, nblk // TILE_GROUP, group, 0)

    for c16 in range(16):
        rows = pl.ds((c16 % 4) * n4 + c16 // 4, n16, stride=CLASS_STEP)
        for x, (dst, off) in enumerate(dsts):
            dst[off + c16 * n16:off + (c16 + 1) * n16, :] = cls4_ref[x, rows, :].astype(bf16)
    state4 = (acc4_ref, m4_ref, l4_ref)
    state1 = (acc1_ref, m1_ref, l1_ref)
    run_pattern(2, None, state4,
                lambda c16, jj: pl.ds(lax.rem(c16, 4) * n4 + c16 // 4 + jj * (QBLK * CLASS_STEP), QBLK,
                                      stride=CLASS_STEP))

    def copy_cls4(i, carry):
        r0 = pl.multiple_of(i * QBLK, QBLK)
        for x, (dst, off) in enumerate(dsts):
            dst[pl.ds(r0 + off, QBLK), :] = cls4_ref[x, pl.ds(r0, QBLK), :].astype(bf16)
        return carry

    lax.fori_loop(0, nblk, copy_cls4, 0)
    run_pattern(1, state4, state1,
                lambda c4, jj: pl.ds(c4 + jj * (QBLK * CLASS_STEP), QBLK, stride=CLASS_STEP))

    def copy_tok(i, carry):
        r0 = pl.multiple_of(i * QBLK, QBLK)
        for src, (dst, off) in zip(srcs[1:], dsts[1:]):
            dst[pl.ds(r0 + off, QBLK), :] = src[pl.ds(r0, QBLK), :]
        return carry

    lax.fori_loop(0, nblk, copy_tok, 0)
    run_pattern(0, state1, None, None)


def _attention(qkv4, bias):
    _, b, seq, _ = qkv4.shape
    np_ = len(DILATIONS)
    head = lambda first: pl.BlockSpec((None, None, seq, HEAD_DIM), lambda h, i: (first + h, i, 0, 0))
    return pl.pallas_call(
        functools.partial(_attn_kernel, seq=seq),
        grid=(ATTN_HEADS, b),
        in_specs=[head(0), head(ATTN_HEADS), head(2 * ATTN_HEADS),
                  pl.BlockSpec((np_, 1, QBLK, KBLK), lambda h, i: (0, h, 0, 0))],
        out_specs=head(0),
        out_shape=jax.ShapeDtypeStruct((ATTN_HEADS, b, seq, HEAD_DIM), bf16),
        scratch_shapes=[pltpu.VMEM((3, seq, HEAD_DIM), f32)] * 2
                       + [pltpu.VMEM((seq, HEAD_DIM), bf16)]
                       + [pltpu.VMEM((seq + 2 * RADIUS, HEAD_DIM), bf16)] * 2
                       + [pltpu.VMEM((np_, 4, QBLK, KBLK), f32)]
                       + [pltpu.VMEM((seq, HEAD_DIM), f32)] * 6
                       + [pltpu.VMEM((TILE_GROUP, QBLK, KBLK), f32),
                          pltpu.VMEM((TILE_GROUP, QBLK, KBLK), bf16),
                          pltpu.VMEM((TILE_GROUP, QBLK, HEAD_DIM), f32),
                          pltpu.VMEM((TILE_GROUP, QBLK, HEAD_DIM), f32)],
        compiler_params=_params("arbitrary", "arbitrary"),
        name="attn",
    )(qkv4, qkv4, qkv4, bias)


HALO = BF16_ROWS


def _silu(x):
    h = 0.5 * x
    return h + h * jnp.tanh(h)


def _softplus(x):
    return jnp.maximum(x, 0.0) + jnp.log1p(jnp.exp(-jnp.abs(x)))


def _ssd_kernel(z_ref, x_ref, bm_ref, cm_ref, dt_ref, cwx_ref, cbx_ref, cwb_ref, cbb_ref, cwc_ref, cbc_ref,
                hp_ref, dskip_ref, ng_ref, out_ref,
                xs_ref, bs_ref, cs_ref, bt_ref, y_ref, acs_ref, rows_ref, decx_ref, hf_ref, hb_ref,
                *stage_refs, seq):
    nc = seq // CHUNK
    g = pl.program_id(1)
    half = CONV_WIDTH // 2
    nh = HEADS_PER_GROUP
    n_slabs = len(stage_refs) // 2

    lane = lax.broadcasted_iota(jnp.int32, (CHUNK, LANES), 1)
    left = lane < SSM_HEAD_DIM
    mask_lo = jnp.where(left, 1.0, 0.0).astype(bf16)
    mask_hi = jnp.where(left, 0.0, 1.0).astype(bf16)
    li = lax.broadcasted_iota(jnp.int32, (CHUNK, CHUNK), 0)
    si = lax.broadcasted_iota(jnp.int32, (CHUNK, CHUNK), 1)
    lower = li >= si
    upper = si >= li
    tril = lower.astype(f32)
    dt_bias = hp_ref[0, 0:1, :]
    a_row = -jnp.exp(hp_ref[0, 1:2, :]) * LOG2E
    shift = lax.rem(LANES - 2 * nh * g, LANES)
    head_of_col = lax.broadcasted_iota(jnp.int32, (LANES, GROUP_WIDTH), 1) // SSM_HEAD_DIM
    head_lane = lax.broadcasted_iota(jnp.int32, (LANES, GROUP_WIDTH), 0)
    spread_f = (head_lane == head_of_col).astype(bf16)
    spread_b = (head_lane == head_of_col + nh).astype(bf16)
    tril_bf = tril.astype(bf16)

    def split3(x):
        hi = x.astype(bf16)
        rest = x - hi.astype(f32)
        mid = rest.astype(bf16)
        return hi, mid, (rest - mid.astype(f32)).astype(bf16)

    def dot3(lhs, rhs):
        return sum(jnp.dot(a, b, preferred_element_type=f32) for a in lhs for b in rhs)

    half_rows = CHUNK // 2

    def prepare(i, carry):
        r0 = pl.multiple_of(i * CHUNK, CHUNK)
        r = pl.ds(r0, CHUNK)
        lo = pl.multiple_of(jnp.maximum(r0 - HALO, 0), HALO)
        hi = pl.multiple_of(jnp.minimum(r0 + CHUNK, seq - HALO), HALO)
        has_lo = jnp.where(i > 0, 1.0, 0.0)
        has_hi = jnp.where(i < nc - 1, 1.0, 0.0)
        slab = 0
        for src, w_ref, b_ref, dst in ((x_ref, cwx_ref, cbx_ref, xs_ref),
                                       (bm_ref, cwb_ref, cbb_ref, bs_ref),
                                       (cm_ref, cwc_ref, cbc_ref, cs_ref)):
            for blk in range(src.shape[-1] // LANES):
                cols = slice(blk * LANES, (blk + 1) * LANES)
                st, ost = stage_refs[slab], stage_refs[n_slabs + slab]
                st[0:HALO, :] = src[0, pl.ds(lo, HALO), cols].astype(f32) * has_lo
                st[HALO:HALO + CHUNK, :] = src[0, r, cols].astype(f32)
                st[HALO + CHUNK:2 * HALO + CHUNK, :] = src[0, pl.ds(hi, HALO), cols].astype(f32) * has_hi
                planes = [st[pl.ds(HALO - half + s, half_rows, stride=2), :]
                          for s in range(CONV_WIDTH + 1)]
                for parity in range(2):
                    acc = b_ref[:, cols]
                    for tap in range(CONV_WIDTH):
                        acc = acc + w_ref[tap:tap + 1, cols] * planes[tap + parity]
                    ost[pl.ds(parity, half_rows, stride=2), :] = _silu(acc)
                out = ost[...].astype(bf16)
                dst[r, cols] = out
                if dst is bs_ref:
                    bt_ref[r, :] = out.astype(f32).T.astype(bf16)
                slab += 1

        raw = pltpu.roll(dt_ref[0, r, :], shift, 1)
        dtv = _softplus(raw + dt_bias)
        adt = dtv * a_row
        cum = dot3((tril_bf,), split3(adt))
        total = cum[CHUNK - 1:CHUNK, :]
        acs = jnp.where(lane < nh, cum, total - cum + adt)
        acs_ref[r, :] = acs
        rows_ref[i, 0] = (acs - jnp.log2(dtv)).T[0:2 * nh, :]
        rows_ref[i, 1] = (jnp.exp2(total - acs) * dtv).T[0:2 * nh, :]
        decay = split3(jnp.broadcast_to(jnp.exp2(total), (8, LANES)))
        decx_ref[i, 0] = dot3(decay, (spread_f,))
        decx_ref[i, 1] = dot3(decay, (spread_b,))
        return carry

    prepare(jnp.int32(0), 0)

    hf_ref[...] = jnp.zeros_like(hf_ref)
    hb_ref[...] = jnp.zeros_like(hb_ref)

    def column(acs, j):
        return jnp.broadcast_to(acs[:, j:j + 1], (CHUNK, CHUNK))

    def forward(c, carry):
        r = pl.ds(pl.multiple_of(c * CHUNK, CHUNK), CHUNK)
        acs = acs_ref[r, :]
        cmat = cs_ref[r, :]
        btf = bt_ref[r, :].astype(f32)
        cb = lax.dot_general(cmat, bs_ref[r, :], (((1,), (1,)), ((), ())), preferred_element_type=f32)
        y_off = jnp.dot(cmat, hf_ref[...].astype(bf16), preferred_element_type=f32)
        for k in range(nh // 2):
            cols = slice(k * LANES, (k + 1) * LANES)
            xp = xs_ref[r, cols]
            halves = (xp * mask_lo, xp * mask_hi)
            y_diag = snew = None
            e_f = []
            for side in range(2):
                j = 2 * k + side
                a_f, a_b = column(acs, j), column(acs, nh + j)
                w = (jnp.exp2(jnp.where(lower, a_f - rows_ref[c, 0, j:j + 1, :], NEG))
                     + jnp.exp2(jnp.where(upper, a_b - rows_ref[c, 0, nh + j:nh + j + 1, :], NEG)))
                d = jnp.dot((cb * w).astype(bf16), halves[side], preferred_element_type=f32)
                inject = (btf * rows_ref[c, 1, j:j + 1, :]).astype(bf16)
                s = jnp.dot(inject, halves[side], preferred_element_type=f32)
                y_diag = d if y_diag is None else y_diag + d
                snew = s if snew is None else snew + s
                e_f.append(a_f)
            y_ref[r, cols] = (y_diag + y_off[:, cols] * jnp.exp2(jnp.where(left, e_f[0], e_f[1]))
                              + dskip_ref[:, cols] * xp.astype(f32))
            hf_ref[:, cols] = hf_ref[:, cols] * decx_ref[c, 0, 0:1, cols] + snew
        return carry

    def forward_and_prepare(c, carry):
        forward(c, carry)
        return prepare(jnp.minimum(c + 1, nc - 1), carry)

    lax.fori_loop(0, nc, forward_and_prepare, 0)

    def backward(i, carry):
        c = nc - 1 - i
        r = pl.ds(pl.multiple_of(c * CHUNK, CHUNK), CHUNK)
        acs = acs_ref[r, :]
        btf = bt_ref[r, :].astype(f32)
        y_off = jnp.dot(cs_ref[r, :], hb_ref[...].astype(bf16), preferred_element_type=f32)
        ys = []
        for k in range(nh // 2):
            cols = slice(k * LANES, (k + 1) * LANES)
            xp = xs_ref[r, cols]
            halves = (xp * mask_lo, xp * mask_hi)
            snew = None
            e_b = []
            for side in range(2):
                jb = nh + 2 * k + side
                inject = (btf * rows_ref[c, 1, jb:jb + 1, :]).astype(bf16)
                s = jnp.dot(inject, halves[side], preferred_element_type=f32)
                snew = s if snew is None else snew + s
                e_b.append(column(acs, jb))
            ys.append(y_ref[r, cols] + y_off[:, cols] * jnp.exp2(jnp.where(left, e_b[0], e_b[1])))
            hb_ref[:, cols] = hb_ref[:, cols] * decx_ref[c, 1, 0:1, cols] + snew
        y = jnp.concatenate(ys, axis=1) * _silu(z_ref[0, r, :].astype(f32))
        y = y * lax.rsqrt(jnp.mean(y * y, axis=-1, keepdims=True) + EPS)
        out_ref[0, r, :] = (y * ng_ref[...]).astype(out_ref.dtype)
        return carry

    lax.fori_loop(0, nc, backward, 0, unroll=2)


def _ssd(proj3, dt3, conv_w, conv_b, head_params, dskip_cols, norm_g):
    b, seq, _ = proj3.shape
    gw, ns = GROUP_WIDTH, SSM_STATE
    wide = lambda off: pl.BlockSpec((1, seq, gw), lambda i, g: (i, 0, off // gw + g))
    narrow = lambda off: pl.BlockSpec((1, seq, ns), lambda i, g: (i, 0, off // ns + g))
    xoff, boff, coff = 0, SSM_WIDTH, SSM_WIDTH + SSM_GROUPS * ns
    cw = lambda off, w: pl.BlockSpec((CONV_WIDTH, w), lambda i, g: (0, off // w + g))
    cbias = lambda off, w: pl.BlockSpec((1, w), lambda i, g: (0, off // w + g))
    return pl.pallas_call(
        functools.partial(_ssd_kernel, seq=seq),
        grid=(b, SSM_GROUPS),
        in_specs=[wide(COL_ZS), wide(COL_X), narrow(COL_B), narrow(COL_C),
                  pl.BlockSpec((1, seq, LANES), lambda i, g: (i, 0, 0)),
                  cw(xoff, gw), cbias(xoff, gw), cw(boff, ns), cbias(boff, ns), cw(coff, ns), cbias(coff, ns),
                  pl.BlockSpec((1, 8, LANES), lambda i, g: (g, 0, 0)),
                  pl.BlockSpec((1, gw), lambda i, g: (0, g)),
                  pl.BlockSpec((1, gw), lambda i, g: (0, g))],
        out_specs=pl.BlockSpec((1, seq, gw), lambda i, g: (i, 0, g)),
        out_shape=jax.ShapeDtypeStruct((b, seq, SSM_WIDTH), bf16),
        scratch_shapes=[pltpu.VMEM((seq, gw), bf16),
                        pltpu.VMEM((seq, ns), bf16),
                        pltpu.VMEM((seq, ns), bf16),
                        pltpu.VMEM((seq, ns), bf16),
                        pltpu.VMEM((seq, gw), f32),
                        pltpu.VMEM((seq, LANES), f32),
                        pltpu.VMEM((seq // CHUNK, 2, 2 * HEADS_PER_GROUP, CHUNK), f32),
                        pltpu.VMEM((seq // CHUNK, 2, 8, gw), f32),
                        pltpu.VMEM((ns, gw), f32),
                        pltpu.VMEM((ns, gw), f32),
                        ]
                       + [pltpu.VMEM((CHUNK + 2 * HALO, LANES), f32)] * ((gw + 2 * ns) // LANES)
                       + [pltpu.VMEM((CHUNK, LANES), f32)] * ((gw + 2 * ns) // LANES),
        compiler_params=_params("arbitrary", "arbitrary"),
        name="ssd",
    )(proj3, proj3, proj3, proj3, dt3, conv_w, conv_b, conv_w, conv_b, conv_w, conv_b,
      head_params, dskip_cols, norm_g)


OUTPROJ_TM = 512


def _outproj_kernel(o_ref, z_ref, s_ref, x_ref, w_ref, ga_ref, gp_ref, y_ref):
    o = jnp.concatenate([o_ref[h] for h in range(ATTN_HEADS)], axis=1).astype(f32)
    a = o * lax.rsqrt(jnp.mean(o * o, axis=-1, keepdims=True) + EPS) * ga_ref[...]
    a = a * _silu(z_ref[...].astype(f32))
    mix = jnp.dot(a.astype(bf16), w_ref[0:ATTN_WIDTH, :], preferred_element_type=f32)
    mix = mix + jnp.dot(s_ref[...], w_ref[ATTN_WIDTH:MIX_WIDTH, :], preferred_element_type=f32)
    mix = mix * lax.rsqrt(jnp.mean(mix * mix, axis=-1, keepdims=True) + EPS) * gp_ref[...]
    y_ref[...] = x_ref[...] + mix


def _outproj(o3, rest2, s2, x2, w_out, g_attn, g_post):
    m = x2.shape[0]
    tm = OUTPROJ_TM
    row = lambda width, blk: pl.BlockSpec((tm, width), lambda i: (i, blk))
    const = lambda shape: pl.BlockSpec(shape, lambda i: (0, 0))
    return pl.pallas_call(
        _outproj_kernel,
        grid=(m // tm,),
        in_specs=[pl.BlockSpec((ATTN_HEADS, tm, HEAD_DIM), lambda i: (0, i, 0)),
                  row(ATTN_WIDTH, COL_ZA // ATTN_WIDTH), row(SSM_WIDTH, 0), row(D_MODEL, 0),
                  pl.BlockSpec((MIX_WIDTH, D_MODEL), lambda i: (0, 0), pipeline_mode=pl.Buffered(1)),
                  const((1, ATTN_WIDTH)), const((1, D_MODEL))],
        out_specs=row(D_MODEL, 0),
        out_shape=jax.ShapeDtypeStruct((m, D_MODEL), f32),
        compiler_params=_params("arbitrary"),
        name="outproj",
    )(o3, rest2, s2, x2, w_out, g_attn, g_post)


def _group_head_rows(fwd, bwd):
    rows = jnp.concatenate([fwd.reshape(SSM_GROUPS, HEADS_PER_GROUP), bwd.reshape(SSM_GROUPS, HEADS_PER_GROUP)],
                           axis=1)
    return jnp.pad(rows, ((0, 0), (0, LANES - 2 * HEADS_PER_GROUP)))


def _layer(x, bias, pre_g, w_main, w_dt, attn_g, conv_w, conv_b, head_params, dskip_cols, ssm_g, w_out, post_g):
    b, seq, _ = x.shape
    x2 = x.reshape(b * seq, D_MODEL)
    qkv, rest2, dt2 = _inproj(x2, pre_g, w_main, w_dt)
    o = _attention(qkv.reshape(3 * ATTN_HEADS, b, seq, HEAD_DIM), bias)
    s = _ssd(rest2.reshape(b, seq, REST_WIDTH), dt2.reshape(b, seq, LANES), conv_w, conv_b, head_params,
             dskip_cols, ssm_g)
    y = _outproj(o.reshape(ATTN_HEADS, b * seq, HEAD_DIM), rest2, s.reshape(b * seq, SSM_WIDTH), x2, w_out,
                 attn_g, post_g)
    return y.reshape(b, seq, D_MODEL)


def kernel(x_prompt, x_sample, pre_norm_g, w_in, rel_bias_table, attn_norm_g, conv_w, conv_b, dt_bias_fwd,
           dt_bias_bwd, a_log_fwd, a_log_bwd, d_skip, ssm_norm_g, w_out, post_norm_g):
    depth = w_in.shape[0]
    bias = _relbias(rel_bias_table)
    y_prompt, y_sample = x_prompt, x_sample
    for i in range(depth):
        w_main = w_in[i, :, :PROJ_MAIN].astype(bf16)
        w_dt = w_in[i, :, PROJ_MAIN:]
        w_dt = jnp.concatenate([w_dt[:, :SSM_HEADS].reshape(D_MODEL, SSM_GROUPS, HEADS_PER_GROUP),
                                w_dt[:, SSM_HEADS:].reshape(D_MODEL, SSM_GROUPS, HEADS_PER_GROUP)], axis=2)
        w_dt = jnp.pad(w_dt.reshape(D_MODEL, 2 * SSM_HEADS), ((0, 0), (0, LANES - 2 * SSM_HEADS))).astype(bf16)
        head_params = jnp.stack([_group_head_rows(dt_bias_fwd[i], dt_bias_bwd[i]),
                                 _group_head_rows(a_log_fwd[i], a_log_bwd[i])], axis=1)
        head_params = jnp.pad(head_params, ((0, 0), (0, 6), (0, 0)))
        args = (bias, pre_norm_g[i].reshape(1, D_MODEL), w_main, w_dt, attn_norm_g[i].reshape(1, ATTN_WIDTH),
                conv_w[i], conv_b[i].reshape(1, CONV_CH), head_params,
                jnp.repeat(d_skip[i], SSM_HEAD_DIM).reshape(1, SSM_WIDTH), ssm_norm_g[i].reshape(1, SSM_WIDTH),
                w_out[i].astype(bf16), post_norm_g[i].reshape(1, D_MODEL))
        y_prompt = _layer(y_prompt, *args)
        y_sample = _layer(y_sample, *args)
    return (y_prompt, y_sample)
```

```python
import functools
import math

import numpy as np
import jax
import jax.numpy as jnp
from jax import lax
from jax.experimental import pallas as pl
from jax.experimental.pallas import tpu as pltpu

D_MODEL = 2048
ATTN_HEADS = 16
HEAD_DIM = 128
ATTN_WIDTH = ATTN_HEADS * HEAD_DIM
DILATIONS = (1, 4, 16)
RADIUS = 64
N_BUCKETS = 32
BUCKET_MAX_DIST = 1024
SSM_HEADS = 32
SSM_HEAD_DIM = 64
SSM_WIDTH = SSM_HEADS * SSM_HEAD_DIM
SSM_GROUPS = 4
HEADS_PER_GROUP = SSM_HEADS // SSM_GROUPS
GROUP_WIDTH = SSM_WIDTH // SSM_GROUPS
SSM_STATE = 128
CONV_WIDTH = 5
CHUNK = 128
CONV_CH = SSM_WIDTH + 2 * SSM_GROUPS * SSM_STATE
MIX_WIDTH = ATTN_WIDTH + SSM_WIDTH
PROJ_MAIN = 4 * ATTN_WIDTH + SSM_WIDTH + CONV_CH
EPS = 1e-6

LANES = 128
BF16_ROWS = 16
QBLK = 128
KBLK = QBLK + 2 * RADIUS
TILE_GROUP = 16
LOG2E = math.log2(math.e)
NEG = -1e30
VMEM_LIMIT = 56 * 1024 * 1024

QKV_WIDTH = 3 * ATTN_WIDTH
REST_WIDTH = PROJ_MAIN - QKV_WIDTH
COL_ZA = 0
COL_ZS = ATTN_WIDTH
COL_X = COL_ZS + SSM_WIDTH
COL_B = COL_X + SSM_WIDTH
COL_C = COL_B + SSM_GROUPS * SSM_STATE

f32 = jnp.float32
bf16 = jnp.bfloat16


def _params(*sem):
    return pltpu.CompilerParams(dimension_semantics=sem, vmem_limit_bytes=VMEM_LIMIT)


def _bucket_tiles():
    qi = np.arange(QBLK)[:, None]
    kj = np.arange(KBLK)[None, :]
    rel = kj - RADIUS - qi
    nb = N_BUCKETS // 2
    max_exact = nb // 2
    tiles = []
    for dil in DILATIONS:
        r = rel * dil
        n = np.abs(r)
        nf = np.maximum(n, 1).astype(np.float32)
        large = max_exact + (np.log(nf / np.float32(max_exact)) / np.float32(math.log(BUCKET_MAX_DIST / max_exact))
                             * np.float32(nb - max_exact)).astype(np.int32)
        large = np.minimum(large, nb - 1)
        bucket = np.where(r > 0, nb, 0) + np.where(n < max_exact, n, large)
        tiles.append(np.where(np.abs(rel) <= RADIUS, bucket, -1))
    return np.stack(tiles).astype(np.int32)


def _relbias_kernel(tab_ref, idx_ref, out_ref):
    h = pl.program_id(0)
    for p in range(len(DILATIONS)):
        idx = idx_ref[p]
        acc = jnp.full((QBLK, KBLK), NEG, f32)
        for b in range(N_BUCKETS):
            acc = jnp.where(idx == b, tab_ref[b, h] * LOG2E, acc)
        out_ref[p, 0] = acc


def _relbias(rel_table):
    idx = jnp.asarray(_bucket_tiles())
    np_ = len(DILATIONS)
    return pl.pallas_call(
        _relbias_kernel,
        grid=(ATTN_HEADS,),
        in_specs=[pl.BlockSpec(memory_space=pltpu.SMEM),
                  pl.BlockSpec((np_, QBLK, KBLK), lambda h: (0, 0, 0))],
        out_specs=pl.BlockSpec((np_, 1, QBLK, KBLK), lambda h: (0, h, 0, 0)),
        out_shape=jax.ShapeDtypeStruct((np_, ATTN_HEADS, QBLK, KBLK), f32),
        compiler_params=_params("arbitrary"),
        name="relbias",
    )(rel_table, idx)


INPROJ_TM = 1024
INPROJ_TN = 1024
NORM_ROWS = 128


QKV_TILES = QKV_WIDTH // INPROJ_TN
HEADS_PER_TILE = INPROJ_TN // HEAD_DIM


NORM_STEPS = INPROJ_TM // NORM_ROWS


def _inproj_kernel(x_ref, g_ref, w_ref, wdt_ref, qkv_ref, rest_ref, dt_ref, h_ref):
    i = pl.program_id(0)
    j = pl.program_id(1)
    cur = lax.rem(i, 2)

    def normalise(rows, slot):
        x = x_ref[rows, :]
        y = x * lax.rsqrt(jnp.mean(x * x, axis=-1, keepdims=True) + EPS)
        h_ref[slot, rows, :] = (y * g_ref[...]).astype(bf16)

    def matmul(with_norm, scaled=False):
        tile = jnp.dot(h_ref[cur], w_ref[...], preferred_element_type=f32)
        if scaled:
            tile = tile * jnp.where(j < ATTN_WIDTH // INPROJ_TN, SCORE_SCALE, 1.0)
        tile = tile.astype(bf16)
        if with_norm:
            normalise(pl.ds(pl.multiple_of((j - 1) * NORM_ROWS, NORM_ROWS), NORM_ROWS), 1 - cur)
        return tile

    def store_qkv(tile):
        for hh in range(HEADS_PER_TILE):
            qkv_ref[hh] = tile[:, hh * HEAD_DIM:(hh + 1) * HEAD_DIM]

    @pl.when((i == 0) & (j == 0))
    def _():
        def rows(c, carry):
            normalise(pl.ds(pl.multiple_of(c * NORM_ROWS, NORM_ROWS), NORM_ROWS), 0)
            return carry

        lax.fori_loop(0, NORM_STEPS, rows, 0)

    @pl.when(j == 0)
    def _():
        dt_ref[...] = jnp.dot(h_ref[cur], wdt_ref[...], preferred_element_type=f32)
        store_qkv(matmul(False, scaled=True))

    @pl.when((j >= 1) & (j < QKV_TILES))
    def _():
        store_qkv(matmul(True, scaled=True))

    @pl.when((j >= QKV_TILES) & (j <= NORM_STEPS))
    def _():
        rest_ref[...] = matmul(True)

    @pl.when(j > NORM_STEPS)
    def _():
        rest_ref[...] = matmul(False)


def _inproj(x2, g, w_main, w_dt):
    m = x2.shape[0]
    last = m // INPROJ_TM - 1
    return pl.pallas_call(
        _inproj_kernel,
        grid=(m // INPROJ_TM, PROJ_MAIN // INPROJ_TN),
        in_specs=[pl.BlockSpec((INPROJ_TM, D_MODEL),
                               lambda i, j: (jnp.where((i == 0) & (j == 0), 0, jnp.minimum(i + 1, last)), 0)),
                  pl.BlockSpec((1, D_MODEL), lambda i, j: (0, 0)),
                  pl.BlockSpec((D_MODEL, INPROJ_TN), lambda i, j: (0, j)),
                  pl.BlockSpec((D_MODEL, LANES), lambda i, j: (0, 0))],
        out_specs=[pl.BlockSpec((HEADS_PER_TILE, INPROJ_TM, HEAD_DIM),
                                lambda i, j: (jnp.minimum(j, QKV_TILES - 1), i, 0)),
                   pl.BlockSpec((INPROJ_TM, INPROJ_TN), lambda i, j: (i, jnp.maximum(j - QKV_TILES, 0))),
                   pl.BlockSpec((INPROJ_TM, LANES), lambda i, j: (i, 0))],
        out_shape=[jax.ShapeDtypeStruct((3 * ATTN_HEADS, m, HEAD_DIM), bf16),
                   jax.ShapeDtypeStruct((m, REST_WIDTH), bf16),
                   jax.ShapeDtypeStruct((m, LANES), f32)],
        scratch_shapes=[pltpu.VMEM((2, INPROJ_TM, D_MODEL), bf16)],
        compiler_params=_params("arbitrary", "arbitrary"),
        name="inproj",
    )(x2, g, w_main, w_dt)


SCORE_SCALE = LOG2E / math.sqrt(HEAD_DIM)
CLASS_STEP = 4


def _attn_kernel(q_ref, k_ref, v_ref, bias_ref, o_ref,
                 tok_ref, cls4_ref, qd_ref, kd_ref, vd_ref, bias4_ref,
                 acc4_ref, m4_ref, l4_ref, acc1_ref, m1_ref, l1_ref,
                 s_buf, p_buf, mn_buf, rs_buf, *, seq):
    nblk = seq // QBLK
    n4 = seq // 4
    n16 = seq // 16
    srcs = (q_ref, k_ref, v_ref)
    dsts = ((qd_ref, 0), (kd_ref, RADIUS), (vd_ref, RADIUS))

    @pl.when(pl.program_id(1) == 0)
    def _():
        col = lax.broadcasted_iota(jnp.int32, (QBLK, KBLK), 1)
        before = col < RADIUS
        after = col >= QBLK + RADIUS
        for p in range(len(DILATIONS)):
            b = bias_ref[p, 0]
            bias4_ref[p, 0] = b
            bias4_ref[p, 1] = jnp.where(before, NEG, b)
            bias4_ref[p, 2] = jnp.where(after, NEG, b)
            bias4_ref[p, 3] = jnp.where(before | after, NEG, b)
        pad = jnp.zeros((RADIUS, HEAD_DIM), bf16)
        for ref in (kd_ref, vd_ref):
            ref[0:RADIUS, :] = pad
            ref[RADIUS + seq:2 * RADIUS + seq, :] = pad

    def stage(i, carry):
        r = pl.ds(pl.multiple_of(i * QBLK, QBLK), QBLK)
        for x, src in enumerate(srcs):
            tok_ref[x, r, :] = src[r, :].astype(f32)
        return carry

    lax.fori_loop(0, nblk, stage, 0)
    for x in range(3):
        for c4 in range(CLASS_STEP):
            cls4_ref[x, c4 * n4:(c4 + 1) * n4, :] = tok_ref[x, pl.ds(c4, n4, stride=CLASS_STEP), :]

    def tile_rows(t):
        return pl.ds(pl.multiple_of(t * QBLK, QBLK), QBLK)

    def scores(t, u, p, tiles_per_class):
        r0 = pl.multiple_of(t * QBLK, QBLK)
        q = (q_ref if DILATIONS[p] == 1 else qd_ref)[pl.ds(r0, QBLK), :]
        k = kd_ref[pl.ds(r0, KBLK), :]
        jj = lax.rem(t, tiles_per_class)
        variant = (jj == 0).astype(jnp.int32) + 2 * (jj == tiles_per_class - 1).astype(jnp.int32)
        s = lax.dot_general(q, k, (((1,), (1,)), ((), ())), preferred_element_type=f32)
        s_buf[u] = s + bias4_ref[p, variant]

    def softmax(t, u, prev):
        s = s_buf[u]
        top = jnp.max(s, axis=-1, keepdims=True)
        if prev is None:
            mn = jnp.broadcast_to(top, (QBLK, HEAD_DIM))
        else:
            mn = jnp.maximum(prev[1][tile_rows(t), :], top)
        e_lo = jnp.exp2(s[:, :HEAD_DIM] - mn)
        e_hi = jnp.exp2(s[:, HEAD_DIM:] - mn)
        p_buf[u, :, :HEAD_DIM] = e_lo.astype(bf16)
        p_buf[u, :, HEAD_DIM:] = e_hi.astype(bf16)
        mn_buf[u] = mn
        rs_buf[u] = jnp.broadcast_to(jnp.sum(e_lo + e_hi, axis=-1, keepdims=True), (QBLK, HEAD_DIM))

    def update(t, u, prev, nxt, dst_rows):
        r = tile_rows(t)
        v = vd_ref[pl.ds(pl.multiple_of(t * QBLK, QBLK), KBLK), :]
        acc = jnp.dot(p_buf[u], v, preferred_element_type=f32)
        mn = mn_buf[u]
        l = rs_buf[u]
        if prev is not None:
            alpha = jnp.exp2(prev[1][r, :] - mn)
            l = alpha * prev[2][r, :] + l
            acc = alpha * prev[0][r, :] + acc
        if nxt is None:
            o_ref[r, :] = (acc / l).astype(o_ref.dtype)
        else:
            nxt[0][dst_rows, :] = acc
            nxt[1][dst_rows, :] = mn
            nxt[2][dst_rows, :] = l

    def run_pattern(p, prev, nxt, dst_rows_fn):
        tiles_per_class = seq // DILATIONS[p] // QBLK

        def group(i, carry):
            tiles = [i * TILE_GROUP + u for u in range(TILE_GROUP)]
            for u, t in enumerate(tiles):
                scores(t, u, p, tiles_per_class)
            for u, t in enumerate(tiles):
                softmax(t, u, prev)
            for u, t in enumerate(tiles):
                dst = None if nxt is None else dst_rows_fn(t // tiles_per_class, lax.rem(t, tiles_per_class))
                update(t, u, prev, nxt, dst)
            return carry

        lax.fori_loop(0, nblk // TILE_GROUP, group, 0)

    for c16 in range(16):
        rows = pl.ds((c16 % 4) * n4 + c16 // 4, n16, stride=CLASS_STEP)
        for x, (dst, off) in enumerate(dsts):
            dst[off + c16 * n16:off + (c16 + 1) * n16, :] = cls4_ref[x, rows, :].astype(bf16)
    state4 = (acc4_ref, m4_ref, l4_ref)
    state1 = (acc1_ref, m1_ref, l1_ref)
    run_pattern(2, None, state4,
                lambda c16, jj: pl.ds(lax.rem(c16, 4) * n4 + c16 // 4 + jj * (QBLK * CLASS_STEP), QBLK,
                                      stride=CLASS_STEP))

    def copy_cls4(i, carry):
        r0 = pl.multiple_of(i * QBLK, QBLK)
        for x, (dst, off) in enumerate(dsts):
            dst[pl.ds(r0 + off, QBLK), :] = cls4_ref[x, pl.ds(r0, QBLK), :].astype(bf16)
        return carry

    lax.fori_loop(0, nblk, copy_cls4, 0)
    run_pattern(1, state4, state1,
                lambda c4, jj: pl.ds(c4 + jj * (QBLK * CLASS_STEP), QBLK, stride=CLASS_STEP))

    def copy_tok(i, carry):
        r0 = pl.multiple_of(i * QBLK, QBLK)
        for src, (dst, off) in zip(srcs[1:], dsts[1:]):
            dst[pl.ds(r0 + off, QBLK), :] = src[pl.ds(r0, QBLK), :]
        return carry

    lax.fori_loop(0, nblk, copy_tok, 0)
    run_pattern(0, state1, None, None)


def _attention(qkv4, bias):
    _, b, seq, _ = qkv4.shape
    np_ = len(DILATIONS)
    head = lambda first: pl.BlockSpec((None, None, seq, HEAD_DIM), lambda h, i: (first + h, i, 0, 0))
    return pl.pallas_call(
        functools.partial(_attn_kernel, seq=seq),
        grid=(ATTN_HEADS, b),
        in_specs=[head(0), head(ATTN_HEADS), head(2 * ATTN_HEADS),
                  pl.BlockSpec((np_, 1, QBLK, KBLK), lambda h, i: (0, h, 0, 0))],
        out_specs=head(0),
        out_shape=jax.ShapeDtypeStruct((ATTN_HEADS, b, seq, HEAD_DIM), bf16),
        scratch_shapes=[pltpu.VMEM((3, seq, HEAD_DIM), f32)] * 2
                       + [pltpu.VMEM((seq, HEAD_DIM), bf16)]
                       + [pltpu.VMEM((seq + 2 * RADIUS, HEAD_DIM), bf16)] * 2
                       + [pltpu.VMEM((np_, 4, QBLK, KBLK), f32)]
                       + [pltpu.VMEM((seq, HEAD_DIM), f32)] * 6
                       + [pltpu.VMEM((TILE_GROUP, QBLK, KBLK), f32),
                          pltpu.VMEM((TILE_GROUP, QBLK, KBLK), bf16),
                          pltpu.VMEM((TILE_GROUP, QBLK, HEAD_DIM), f32),
                          pltpu.VMEM((TILE_GROUP, QBLK, HEAD_DIM), f32)],
        compiler_params=_params("arbitrary", "arbitrary"),
        name="attn",
    )(qkv4, qkv4, qkv4, bias)


HALO = BF16_ROWS


def _silu(x):
    h = 0.5 * x
    return h + h * jnp.tanh(h)


def _softplus(x):
    return jnp.maximum(x, 0.0) + jnp.log1p(jnp.exp(-jnp.abs(x)))


def _ssd_kernel(z_ref, x_ref, bm_ref, cm_ref, dt_ref, cwx_ref, cbx_ref, cwb_ref, cbb_ref, cwc_ref, cbc_ref,
                hp_ref, dskip_ref, ng_ref, out_ref,
                xs_ref, bs_ref, cs_ref, bt_ref, y_ref, acs_ref, rows_ref, decx_ref, hf_ref, hb_ref,
                *stage_refs, seq):
    nc = seq // CHUNK
    g = pl.program_id(1)
    half = CONV_WIDTH // 2
    nh = HEADS_PER_GROUP
    n_slabs = len(stage_refs) // 2

    lane = lax.broadcasted_iota(jnp.int32, (CHUNK, LANES), 1)
    left = lane < SSM_HEAD_DIM
    mask_lo = jnp.where(left, 1.0, 0.0).astype(bf16)
    mask_hi = jnp.where(left, 0.0, 1.0).astype(bf16)
    li = lax.broadcasted_iota(jnp.int32, (CHUNK, CHUNK), 0)
    si = lax.broadcasted_iota(jnp.int32, (CHUNK, CHUNK), 1)
    lower = li >= si
    upper = si >= li
    tril = lower.astype(f32)
    dt_bias = hp_ref[0, 0:1, :]
    a_row = -jnp.exp(hp_ref[0, 1:2, :]) * LOG2E
    shift = lax.rem(LANES - 2 * nh * g, LANES)
    head_of_col = lax.broadcasted_iota(jnp.int32, (LANES, GROUP_WIDTH), 1) // SSM_HEAD_DIM
    head_lane = lax.broadcasted_iota(jnp.int32, (LANES, GROUP_WIDTH), 0)
    spread_f = (head_lane == head_of_col).astype(bf16)
    spread_b = (head_lane == head_of_col + nh).astype(bf16)
    tril_bf = tril.astype(bf16)

    def split3(x):
        hi = x.astype(bf16)
        rest = x - hi.astype(f32)
        mid = rest.astype(bf16)
        return hi, mid, (rest - mid.astype(f32)).astype(bf16)

    def dot3(lhs, rhs):
        return sum(jnp.dot(a, b, preferred_element_type=f32) for a in lhs for b in rhs)

    half_rows = CHUNK // 2

    def prepare(i, carry):
        r0 = pl.multiple_of(i * CHUNK, CHUNK)
        r = pl.ds(r0, CHUNK)
        lo = pl.multiple_of(jnp.maximum(r0 - HALO, 0), HALO)
        hi = pl.multiple_of(jnp.minimum(r0 + CHUNK, seq - HALO), HALO)
        has_lo = jnp.where(i > 0, 1.0, 0.0)
        has_hi = jnp.where(i < nc - 1, 1.0, 0.0)
        slab = 0
        for src, w_ref, b_ref, dst in ((x_ref, cwx_ref, cbx_ref, xs_ref),
                                       (bm_ref, cwb_ref, cbb_ref, bs_ref),
                                       (cm_ref, cwc_ref, cbc_ref, cs_ref)):
            for blk in range(src.shape[-1] // LANES):
                cols = slice(blk * LANES, (blk + 1) * LANES)
                st, ost = stage_refs[slab], stage_refs[n_slabs + slab]
                st[0:HALO, :] = src[0, pl.ds(lo, HALO), cols].astype(f32) * has_lo
                st[HALO:HALO + CHUNK, :] = src[0, r, cols].astype(f32)
                st[HALO + CHUNK:2 * HALO + CHUNK, :] = src[0, pl.ds(hi, HALO), cols].astype(f32) * has_hi
                planes = [st[pl.ds(HALO - half + s, half_rows, stride=2), :]
                          for s in range(CONV_WIDTH + 1)]
                for parity in range(2):
                    acc = b_ref[:, cols]
                    for tap in range(CONV_WIDTH):
                        acc = acc + w_ref[tap:tap + 1, cols] * planes[tap + parity]
                    ost[pl.ds(parity, half_rows, stride=2), :] = _silu(acc)
                out = ost[...].astype(bf16)
                dst[r, cols] = out
                if dst is bs_ref:
                    bt_ref[r, :] = out.astype(f32).T.astype(bf16)
                slab += 1

        raw = pltpu.roll(dt_ref[0, r, :], shift, 1)
        dtv = _softplus(raw + dt_bias)
        adt = dtv * a_row
        cum = dot3((tril_bf,), split3(adt))
        total = cum[CHUNK - 1:CHUNK, :]
        acs = jnp.where(lane < nh, cum, total - cum + adt)
        acs_ref[r, :] = acs
        rows_ref[i, 0] = (acs - jnp.log2(dtv)).T[0:2 * nh, :]
        rows_ref[i, 1] = (jnp.exp2(total - acs) * dtv).T[0:2 * nh, :]
        decay = split3(jnp.broadcast_to(jnp.exp2(total), (8, LANES)))
        decx_ref[i, 0] = dot3(decay, (spread_f,))
        decx_ref[i, 1] = dot3(decay, (spread_b,))
        return carry

    prepare(jnp.int32(0), 0)

    hf_ref[...] = jnp.zeros_like(hf_ref)
    hb_ref[...] = jnp.zeros_like(hb_ref)

    def column(acs, j):
        return jnp.broadcast_to(acs[:, j:j + 1], (CHUNK, CHUNK))

    def forward(c, carry):
        r = pl.ds(pl.multiple_of(c * CHUNK, CHUNK), CHUNK)
        acs = acs_ref[r, :]
        cmat = cs_ref[r, :]
        btf = bt_ref[r, :].astype(f32)
        cb = lax.dot_general(cmat, bs_ref[r, :], (((1,), (1,)), ((), ())), preferred_element_type=f32)
        y_off = jnp.dot(cmat, hf_ref[...].astype(bf16), preferred_element_type=f32)
        for k in range(nh // 2):
            cols = slice(k * LANES, (k + 1) * LANES)
            xp = xs_ref[r, cols]
            halves = (xp * mask_lo, xp * mask_hi)
            y_diag = snew = None
            e_f = []
            for side in range(2):
                j = 2 * k + side
                a_f, a_b = column(acs, j), column(acs, nh + j)
                w = (jnp.exp2(jnp.where(lower, a_f - rows_ref[c, 0, j:j + 1, :], NEG))
                     + jnp.exp2(jnp.where(upper, a_b - rows_ref[c, 0, nh + j:nh + j + 1, :], NEG)))
                d = jnp.dot((cb * w).astype(bf16), halves[side], preferred_element_type=f32)
                inject = (btf * rows_ref[c, 1, j:j + 1, :]).astype(bf16)
                s = jnp.dot(inject, halves[side], preferred_element_type=f32)
                y_diag = d if y_diag is None else y_diag + d
                snew = s if snew is None else snew + s
                e_f.append(a_f)
            y_ref[r, cols] = (y_diag + y_off[:, cols] * jnp.exp2(jnp.where(left, e_f[0], e_f[1]))
                              + dskip_ref[:, cols] * xp.astype(f32))
            hf_ref[:, cols] = hf_ref[:, cols] * decx_ref[c, 0, 0:1, cols] + snew
        return carry

    def forward_and_prepare(c, carry):
        forward(c, carry)
        return prepare(jnp.minimum(c + 1, nc - 1), carry)

    lax.fori_loop(0, nc, forward_and_prepare, 0)

    def backward(i, carry):
        c = nc - 1 - i
        r = pl.ds(pl.multiple_of(c * CHUNK, CHUNK), CHUNK)
        scale_b = jnp.exp2(acs_ref[r, :])
        btf = bt_ref[r, :].astype(f32)
        y_off = jnp.dot(cs_ref[r, :], hb_ref[...].astype(bf16), preferred_element_type=f32)
        ys = []
        for k in range(nh // 2):
            cols = slice(k * LANES, (k + 1) * LANES)
            xp = xs_ref[r, cols]
            halves = (xp * mask_lo, xp * mask_hi)
            snew = None
            e_b = []
            for side in range(2):
                jb = nh + 2 * k + side
                inject = (btf * rows_ref[c, 1, jb:jb + 1, :]).astype(bf16)
                s = jnp.dot(inject, halves[side], preferred_element_type=f32)
                snew = s if snew is None else snew + s
                e_b.append(column(scale_b, jb))
            ys.append(y_ref[r, cols] + y_off[:, cols] * jnp.where(left, e_b[0], e_b[1]))
            hb_ref[:, cols] = hb_ref[:, cols] * decx_ref[c, 1, 0:1, cols] + snew
        y = jnp.concatenate(ys, axis=1) * _silu(z_ref[0, r, :].astype(f32))
        y = y * lax.rsqrt(jnp.mean(y * y, axis=-1, keepdims=True) + EPS)
        out_ref[0, r, :] = (y * ng_ref[...]).astype(out_ref.dtype)
        return carry

    lax.fori_loop(0, nc, backward, 0, unroll=2)


def _ssd(proj3, dt3, conv_w, conv_b, head_params, dskip_cols, norm_g):
    b, seq, _ = proj3.shape
    gw, ns = GROUP_WIDTH, SSM_STATE
    wide = lambda off: pl.BlockSpec((1, seq, gw), lambda i, g: (i, 0, off // gw + g))
    narrow = lambda off: pl.BlockSpec((1, seq, ns), lambda i, g: (i, 0, off // ns + g))
    xoff, boff, coff = 0, SSM_WIDTH, SSM_WIDTH + SSM_GROUPS * ns
    cw = lambda off, w: pl.BlockSpec((CONV_WIDTH, w), lambda i, g: (0, off // w + g))
    cbias = lambda off, w: pl.BlockSpec((1, w), lambda i, g: (0, off // w + g))
    return pl.pallas_call(
        functools.partial(_ssd_kernel, seq=seq),
        grid=(b, SSM_GROUPS),
        in_specs=[wide(COL_ZS), wide(COL_X), narrow(COL_B), narrow(COL_C),
                  pl.BlockSpec((1, seq, LANES), lambda i, g: (i, 0, 0)),
                  cw(xoff, gw), cbias(xoff, gw), cw(boff, ns), cbias(boff, ns), cw(coff, ns), cbias(coff, ns),
                  pl.BlockSpec((1, 8, LANES), lambda i, g: (g, 0, 0)),
                  pl.BlockSpec((1, gw), lambda i, g: (0, g)),
                  pl.BlockSpec((1, gw), lambda i, g: (0, g))],
        out_specs=pl.BlockSpec((1, seq, gw), lambda i, g: (i, 0, g)),
        out_shape=jax.ShapeDtypeStruct((b, seq, SSM_WIDTH), bf16),
        scratch_shapes=[pltpu.VMEM((seq, gw), bf16),
                        pltpu.VMEM((seq, ns), bf16),
                        pltpu.VMEM((seq, ns), bf16),
                        pltpu.VMEM((seq, ns), bf16),
                        pltpu.VMEM((seq, gw), f32),
                        pltpu.VMEM((seq, LANES), f32),
                        pltpu.VMEM((seq // CHUNK, 2, 2 * HEADS_PER_GROUP, CHUNK), f32),
                        pltpu.VMEM((seq // CHUNK, 2, 8, gw), f32),
                        pltpu.VMEM((ns, gw), f32),
                        pltpu.VMEM((ns, gw), f32),
                        ]
                       + [pltpu.VMEM((CHUNK + 2 * HALO, LANES), f32)] * ((gw + 2 * ns) // LANES)
                       + [pltpu.VMEM((CHUNK, LANES), f32)] * ((gw + 2 * ns) // LANES),
        compiler_params=_params("arbitrary", "arbitrary"),
        name="ssd",
    )(proj3, proj3, proj3, proj3, dt3, conv_w, conv_b, conv_w, conv_b, conv_w, conv_b,
      head_params, dskip_cols, norm_g)


OUTPROJ_TM = 512


def _outproj_kernel(o_ref, z_ref, s_ref, x_ref, w_ref, ga_ref, gp_ref, y_ref):
    o = jnp.concatenate([o_ref[h] for h in range(ATTN_HEADS)], axis=1).astype(f32)
    a = o * lax.rsqrt(jnp.mean(o * o, axis=-1, keepdims=True) + EPS) * ga_ref[...]
    a = a * _silu(z_ref[...].astype(f32))
    mix = jnp.dot(a.astype(bf16), w_ref[0:ATTN_WIDTH, :], preferred_element_type=f32)
    mix = mix + jnp.dot(s_ref[...], w_ref[ATTN_WIDTH:MIX_WIDTH, :], preferred_element_type=f32)
    mix = mix * lax.rsqrt(jnp.mean(mix * mix, axis=-1, keepdims=True) + EPS) * gp_ref[...]
    y_ref[...] = x_ref[...] + mix


def _outproj(o3, rest2, s2, x2, w_out, g_attn, g_post):
    m = x2.shape[0]
    tm = OUTPROJ_TM
    row = lambda width, blk: pl.BlockSpec((tm, width), lambda i: (i, blk))
    const = lambda shape: pl.BlockSpec(shape, lambda i: (0, 0))
    return pl.pallas_call(
        _outproj_kernel,
        grid=(m // tm,),
        in_specs=[pl.BlockSpec((ATTN_HEADS, tm, HEAD_DIM), lambda i: (0, i, 0)),
                  row(ATTN_WIDTH, COL_ZA // ATTN_WIDTH), row(SSM_WIDTH, 0), row(D_MODEL, 0),
                  pl.BlockSpec((MIX_WIDTH, D_MODEL), lambda i: (0, 0), pipeline_mode=pl.Buffered(1)),
                  const((1, ATTN_WIDTH)), const((1, D_MODEL))],
        out_specs=row(D_MODEL, 0),
        out_shape=jax.ShapeDtypeStruct((m, D_MODEL), f32),
        compiler_params=_params("arbitrary"),
        name="outproj",
    )(o3, rest2, s2, x2, w_out, g_attn, g_post)


def _group_head_rows(fwd, bwd):
    rows = jnp.concatenate([fwd.reshape(SSM_GROUPS, HEADS_PER_GROUP), bwd.reshape(SSM_GROUPS, HEADS_PER_GROUP)],
                           axis=1)
    return jnp.pad(rows, ((0, 0), (0, LANES - 2 * HEADS_PER_GROUP)))


def _layer(x, bias, pre_g, w_main, w_dt, attn_g, conv_w, conv_b, head_params, dskip_cols, ssm_g, w_out, post_g):
    b, seq, _ = x.shape
    x2 = x.reshape(b * seq, D_MODEL)
    qkv, rest2, dt2 = _inproj(x2, pre_g, w_main, w_dt)
    o = _attention(qkv.reshape(3 * ATTN_HEADS, b, seq, HEAD_DIM), bias)
    s = _ssd(rest2.reshape(b, seq, REST_WIDTH), dt2.reshape(b, seq, LANES), conv_w, conv_b, head_params,
             dskip_cols, ssm_g)
    y = _outproj(o.reshape(ATTN_HEADS, b * seq, HEAD_DIM), rest2, s.reshape(b * seq, SSM_WIDTH), x2, w_out,
                 attn_g, post_g)
    return y.reshape(b, seq, D_MODEL)


def kernel(x_prompt, x_sample, pre_norm_g, w_in, rel_bias_table, attn_norm_g, conv_w, conv_b, dt_bias_fwd,
           dt_bias_bwd, a_log_fwd, a_log_bwd, d_skip, ssm_norm_g, w_out, post_norm_g):
    depth = w_in.shape[0]
    bias = _relbias(rel_bias_table)
    y_prompt, y_sample = x_prompt, x_sample
    for i in range(depth):
        w_main = w_in[i, :, :PROJ_MAIN].astype(bf16)
        w_dt = w_in[i, :, PROJ_MAIN:]
        w_dt = jnp.concatenate([w_dt[:, :SSM_HEADS].reshape(D_MODEL, SSM_GROUPS, HEADS_PER_GROUP),
                                w_dt[:, SSM_HEADS:].reshape(D_MODEL, SSM_GROUPS, HEADS_PER_GROUP)], axis=2)
        w_dt = jnp.pad(w_dt.reshape(D_MODEL, 2 * SSM_HEADS), ((0, 0), (0, LANES - 2 * SSM_HEADS))).astype(bf16)
        head_params = jnp.stack([_group_head_rows(dt_bias_fwd[i], dt_bias_bwd[i]),
                                 _group_head_rows(a_log_fwd[i], a_log_bwd[i])], axis=1)
        head_params = jnp.pad(head_params, ((0, 0), (0, 6), (0, 0)))
        args = (bias, pre_norm_g[i].reshape(1, D_MODEL), w_main, w_dt, attn_norm_g[i].reshape(1, ATTN_WIDTH),
                conv_w[i], conv_b[i].reshape(1, CONV_CH), head_params,
                jnp.repeat(d_skip[i], SSM_HEAD_DIM).reshape(1, SSM_WIDTH), ssm_norm_g[i].reshape(1, SSM_WIDTH),
                w_out[i].astype(bf16), post_norm_g[i].reshape(1, D_MODEL))
        y_prompt = _layer(y_prompt, *args)
        y_sample = _layer(y_sample, *args)
    return (y_prompt, y_sample)
```
